```python
import jax, jax.numpy as jnp
from jax import lax
import numpy as np


D_MODEL = 1024
BATCH = 4
SEQ = 4096
DEPTH = 1

GRID_W = 64
CTX_LEN = 256

RET_HEADS = 8
RET_QK_DIM = 128
RET_V_DIM = 256
RET_QK = RET_HEADS * RET_QK_DIM
RET_V = RET_HEADS * RET_V_DIM
RET_CHUNK = 128
ROPE_BASE = 10000.0

CONV_DIM = D_MODEL
CONV_WIDTH = 31
N_BRANCHES = 2

N_GROUPS = 4
EXPERTS_PER_GROUP = 8
N_EXPERTS = N_GROUPS * EXPERTS_PER_GROUP
EXPERT_TOP_K = 2
EXPERT_FF = 512

LN_EPS = 1e-5
ALPHA = (2.0 * DEPTH) ** 0.25
BETA = (8.0 * DEPTH) ** -0.25

Q_OFF = 0
K_OFF = Q_OFF + RET_QK
V_OFF = K_OFF + RET_QK
G_OFF = V_OFF + RET_V
GLU_OFF = G_OFF + RET_V
GATE_OFF = GLU_OFF + 2 * CONV_DIM
IN_COLS = GATE_OFF + N_BRANCHES * D_MODEL

kernel_name = 'hybrid_retention_conformer_hmoe_block'


def _layernorm(x, g, b):
    xf = x.astype(jnp.float32)
    mu = xf.mean(-1, keepdims=True)
    var = jnp.square(xf - mu).mean(-1, keepdims=True)
    return (xf - mu) * lax.rsqrt(var + LN_EPS) * g + b


def _head_norm(o):
    of = o.astype(jnp.float32)
    mu = of.mean(-1, keepdims=True)
    var = jnp.square(of - mu).mean(-1, keepdims=True)
    return (of - mu) * lax.rsqrt(var + LN_EPS)


def _modulate(h, shift, scale):
    return h * (1.0 + scale) + shift


def _rope_2d(rows):
    r, col = jnp.meshgrid(jnp.arange(rows), jnp.arange(GRID_W), indexing='ij')
    r = r.reshape(-1).astype(jnp.float32)
    col = col.reshape(-1).astype(jnp.float32)
    n_freq = RET_QK_DIM // 4
    inv = ROPE_BASE ** (-jnp.arange(n_freq, dtype=jnp.float32) / n_freq)
    ang = jnp.concatenate([r[:, None] * inv, col[:, None] * inv], axis=-1)
    ang = jnp.concatenate([ang, ang], axis=-1)
    return jnp.cos(ang), jnp.sin(ang)


def _apply_rope(t, cos, sin):
    t1, t2 = jnp.split(t, 2, axis=-1)
    rot = jnp.concatenate([-t2, t1], axis=-1)
    return t * cos[None, :, None, :] + rot * sin[None, :, None, :]


def _split_proj(proj):
    b_, l_ = proj.shape[:2]
    q = proj[..., Q_OFF:K_OFF].reshape(b_, l_, RET_HEADS, RET_QK_DIM)
    k = proj[..., K_OFF:V_OFF].reshape(b_, l_, RET_HEADS, RET_QK_DIM) * RET_QK_DIM ** -0.5
    v = proj[..., V_OFF:G_OFF].reshape(b_, l_, RET_HEADS, RET_V_DIM)
    g = proj[..., G_OFF:GLU_OFF]
    glu = proj[..., GLU_OFF:GATE_OFF]
    gates = proj[..., GATE_OFF:]
    return q, k, v, g, glu, gates


def _retention_chunked(q, k, v, log_gamma, s0):
    b_, l_, h_, _ = q.shape
    dv = v.shape[-1]
    n_chunks = l_ // RET_CHUNK
    pos = jnp.arange(RET_CHUNK, dtype=jnp.float32)
    diff = pos[:, None] - pos[None, :]
    intra = jnp.where(diff >= 0, jnp.exp(jnp.maximum(diff, 0.0)[None] * log_gamma[:, None, None]), 0.0)
    q_dec = jnp.exp((pos[:, None] + 1.0) * log_gamma[None, :])
    k_dec = jnp.exp((RET_CHUNK - 1.0 - pos)[:, None] * log_gamma[None, :])
    c_dec = jnp.exp(RET_CHUNK * log_gamma)

    def to_chunks(t):
        return jnp.moveaxis(t.reshape(b_, n_chunks, RET_CHUNK, h_, t.shape[-1]), 1, 0)

    def step(s, qkv):
        qc, kc, vc = qkv
        scores = jnp.einsum('bihd,bjhd->bhij', qc, kc) * intra
        o = (jnp.einsum('bhij,bjhv->bihv', scores, vc)
             + jnp.einsum('bihd,bhdv->bihv', qc * q_dec[None, :, :, None], s))
        s = s * c_dec[None, :, None, None] + jnp.einsum('bjhd,bjhv->bhdv', kc * k_dec[None, :, :, None], vc)
        return s, o

    s_fin, o = lax.scan(step, s0, (to_chunks(q), to_chunks(k), to_chunks(v)))
    o = jnp.moveaxis(o, 0, 1).reshape(b_, l_, h_, dv)
    return o, s_fin


def _retention_bidir(q, k, v, lg_f, lg_b, s0_f, s0_b):
    o_f, s_f = _retention_chunked(q, k, v, lg_f, s0_f)
    o_b, s_b = _retention_chunked(q[:, ::-1], k[:, ::-1], v[:, ::-1], lg_b, s0_b)
    return o_f + o_b[:, ::-1], s_f, s_b


def _retention_final_state(k, v, log_gamma):
    l_ = k.shape[1]
    w = jnp.exp((l_ - 1.0 - jnp.arange(l_, dtype=jnp.float32))[:, None] * log_gamma[None, :])
    return jnp.einsum('blhd,blhv->bhdv', k * w[None, :, :, None], v)


def _depthwise_conv(z, w, b):
    out = lax.conv_general_dilated(
        z, w[:, None, :].astype(z.dtype), window_strides=(1,),
        padding=[(CONV_WIDTH // 2, CONV_WIDTH // 2)],
        dimension_numbers=('NWC', 'WIO', 'NWC'), feature_group_count=z.shape[-1])
    return out + b


def _merge_branches(o_ret, g, glu, gates, p):
    b_, l_ = g.shape[:2]
    ret = jax.nn.silu(g.astype(jnp.float32)) * _head_norm(o_ret).reshape(b_, l_, RET_V)
    y_a = ret @ p['w_ret_out']
    za, zb = jnp.split(glu, 2, axis=-1)
    z = za * jax.nn.sigmoid(zb)
    z = _depthwise_conv(z, p['conv_dw'], p['conv_dw_b'])
    z = jax.nn.silu(_layernorm(z, p['conv_ln_g'], p['conv_ln_b']))
    y_b = z @ p['w_conv_out'] + p['b_conv_out']
    ga, gb = jnp.split(jax.nn.sigmoid(gates.astype(jnp.float32)), 2, axis=-1)
    return (ga * y_a + gb * y_b) @ p['w_mix_out'] + p['b_mix_out']


def _hier_moe(u, p):
    b_, l_, d_ = u.shape
    t = u.reshape(b_ * l_, d_)
    grp_p = jax.nn.softmax((t @ p['w_router_grp'] + p['b_router_grp']).astype(jnp.float32), axis=-1)
    grp_w, grp_idx = lax.top_k(grp_p, 1)
    exp_logits = (t @ p['w_router_exp'] + p['b_router_exp']).astype(jnp.float32)
    exp_logits = exp_logits.reshape(-1, N_GROUPS, EXPERTS_PER_GROUP)
    in_grp = jnp.take_along_axis(exp_logits, grp_idx[:, :, None], axis=1)[:, 0]
    top_w, top_i = lax.top_k(jax.nn.softmax(in_grp, axis=-1), EXPERT_TOP_K)
    top_w = top_w / top_w.sum(-1, keepdims=True) * grp_w
    eid = grp_idx * EXPERTS_PER_GROUP + top_i
    combine = jnp.einsum('nk,nke->ne', top_w, jax.nn.one_hot(eid, N_EXPERTS, dtype=jnp.float32))
    y = jnp.zeros(t.shape, jnp.float32)
    for e in range(N_EXPERTS):
        h = jax.nn.silu(t @ p['w_exp_gate'][e]) * (t @ p['w_exp_up'][e])
        y = y + combine[:, e:e + 1] * (h @ p['w_exp_down'][e])
    return y.reshape(b_, l_, d_)


def _layer(h_lat, h_ctx, s_lat, s_ctx, p, cos, sin, update_ctx):
    mod_lat = s_lat @ p['w_ada'] + p['b_ada']
    mod_ctx = s_ctx @ p['w_ada'] + p['b_ada']
    sh_m, sc_m, gt_m, sh_f, sc_f, gt_f = jnp.split(mod_lat[:, None, :], 6, axis=-1)
    csh_m, csc_m, cgt_m, csh_f, csc_f, cgt_f = jnp.split(mod_ctx, 6, axis=-1)
    lg_f = jax.nn.log_sigmoid(p['ret_decay_fwd'].astype(jnp.float32))
    lg_b = jax.nn.log_sigmoid(p['ret_decay_bwd'].astype(jnp.float32))
    b_ = h_lat.shape[0]

    u_ctx = _modulate(h_ctx, csh_m, csc_m)
    if update_ctx:
        q_c, k_c, v_c, g_c, glu_c, gates_c = _split_proj(u_ctx @ p['w_in'] + p['b_in'])
        zeros = jnp.zeros((b_, RET_HEADS, RET_QK_DIM, RET_V_DIM), jnp.float32)
        o_c, s_f, s_b = _retention_bidir(q_c, k_c, v_c, lg_f, lg_b, zeros, zeros)
    else:
        kv = u_ctx @ p['w_in'][:, K_OFF:G_OFF] + p['b_in'][K_OFF:G_OFF]
        k_c = kv[..., :RET_QK].reshape(b_, -1, RET_HEADS, RET_QK_DIM) * RET_QK_DIM ** -0.5
        v_c = kv[..., RET_QK:].reshape(b_, -1, RET_HEADS, RET_V_DIM)
        s_f = _retention_final_state(k_c, v_c, lg_f)
        s_b = _retention_final_state(k_c[:, ::-1], v_c[:, ::-1], lg_b)

    u_lat = _modulate(h_lat, sh_m, sc_m)
    q, k, v, g, glu, gates = _split_proj(u_lat @ p['w_in'] + p['b_in'])
    q = _apply_rope(q, cos, sin)
    k = _apply_rope(k, cos, sin)
    o, _, _ = _retention_bidir(q, k, v, lg_f, lg_b, s_f, s_b)
    mix = _merge_branches(o, g, glu, gates, p)
    h_lat = _layernorm(ALPHA * h_lat + gt_m * mix, p['ln1_g'], p['ln1_b'])
    moe = _hier_moe(_modulate(h_lat, sh_f, sc_f), p)
    h_lat = _layernorm(ALPHA * h_lat + gt_f * moe, p['ln2_g'], p['ln2_b'])

    if update_ctx:
        mix_c = _merge_branches(o_c, g_c, glu_c, gates_c, p)
        h_ctx = _layernorm(ALPHA * h_ctx + cgt_m * mix_c, p['ln1_g'], p['ln1_b'])
        moe_c = _hier_moe(_modulate(h_ctx, csh_f, csc_f), p)
        h_ctx = _layernorm(ALPHA * h_ctx + cgt_f * moe_c, p['ln2_g'], p['ln2_b'])
    return h_lat, h_ctx


def setup_inputs(seed: int = 0) -> dict:
    key = jax.random.key(seed)
    ks = jax.random.split(key, 32)
    f32 = jnp.float32
    D = D_MODEL

    def nrm(k, shape, scale):
        return jax.random.normal(k, shape, f32) * scale

    gamma0 = 1.0 - 2.0 ** (-5.0 - np.arange(RET_HEADS))
    logit0 = jnp.asarray(np.log(gamma0 / (1.0 - gamma0)), f32)
    return {
        'x': nrm(ks[0], (BATCH, SEQ, D), 1.0),
        'c': nrm(ks[1], (BATCH, D), 1.0),
        'ctx': nrm(ks[2], (BATCH, CTX_LEN, D), 1.0),
        'c_ctx': nrm(ks[3], (D,), 1.0),
        'w_ada': nrm(ks[4], (DEPTH, D, 6 * D), 0.5 * D ** -0.5),
        'b_ada': nrm(ks[5], (DEPTH, 6 * D), 0.01),
        'w_in': nrm(ks[6], (DEPTH, D, IN_COLS), D ** -0.5),
        'b_in': nrm(ks[7], (DEPTH, IN_COLS), 0.01),
        'ret_decay_fwd': logit0 + nrm(ks[8], (DEPTH, RET_HEADS), 0.1),
        'ret_decay_bwd': logit0 + nrm(ks[9], (DEPTH, RET_HEADS), 0.1),
        'w_ret_out': nrm(ks[10], (DEPTH, RET_V, D), RET_V ** -0.5),
        'conv_dw': nrm(ks[11], (DEPTH, CONV_WIDTH, CONV_DIM), CONV_WIDTH ** -0.5),
        'conv_dw_b': nrm(ks[12], (DEPTH, CONV_DIM), 0.01),
        'conv_ln_g': 1.0 + nrm(ks[13], (DEPTH, CONV_DIM), 0.02),
        'conv_ln_b': nrm(ks[14], (DEPTH, CONV_DIM), 0.01),
        'w_conv_out': nrm(ks[15], (DEPTH, CONV_DIM, D), CONV_DIM ** -0.5),
        'b_conv_out': nrm(ks[16], (DEPTH, D), 0.01),
        'w_mix_out': nrm(ks[17], (DEPTH, D, D), BETA * D ** -0.5),
        'b_mix_out': nrm(ks[18], (DEPTH, D), 0.01),
        'ln1_g': 1.0 + nrm(ks[19], (DEPTH, D), 0.02),
        'ln1_b': nrm(ks[20], (DEPTH, D), 0.01),
        'w_router_grp': nrm(ks[21], (DEPTH, D, N_GROUPS), D ** -0.5),
        'b_router_grp': nrm(ks[22], (DEPTH, N_GROUPS), 0.01),
        'w_router_exp': nrm(ks[23], (DEPTH, D, N_EXPERTS), D ** -0.5),
        'b_router_exp': nrm(ks[24], (DEPTH, N_EXPERTS), 0.01),
        'w_exp_gate': nrm(ks[25], (DEPTH, N_EXPERTS, D, EXPERT_FF), D ** -0.5),
        'w_exp_up': nrm(ks[26], (DEPTH, N_EXPERTS, D, EXPERT_FF), D ** -0.5),
        'w_exp_down': nrm(ks[27], (DEPTH, N_EXPERTS, EXPERT_FF, D), BETA * EXPERT_FF ** -0.5),
        'ln2_g': 1.0 + nrm(ks[28], (DEPTH, D), 0.02),
        'ln2_b': nrm(ks[29], (DEPTH, D), 0.01),
    }


def reference(x, c, ctx, c_ctx, w_ada, b_ada, w_in, b_in, ret_decay_fwd, ret_decay_bwd, w_ret_out,
              conv_dw, conv_dw_b, conv_ln_g, conv_ln_b, w_conv_out, b_conv_out, w_mix_out, b_mix_out,
              ln1_g, ln1_b, w_router_grp, b_router_grp, w_router_exp, b_router_exp,
              w_exp_gate, w_exp_up, w_exp_down, ln2_g, ln2_b):
    rows = x.shape[1] // GRID_W
    cos, sin = _rope_2d(rows)
    s_lat = jax.nn.silu(c)
    s_ctx = jax.nn.silu(c_ctx)
    h_lat, h_ctx = x, ctx
    for layer in range(DEPTH):
        p = dict(
            w_ada=w_ada[layer], b_ada=b_ada[layer], w_in=w_in[layer], b_in=b_in[layer],
            ret_decay_fwd=ret_decay_fwd[layer], ret_decay_bwd=ret_decay_bwd[layer],
            w_ret_out=w_ret_out[layer], conv_dw=conv_dw[layer], conv_dw_b=conv_dw_b[layer],
            conv_ln_g=conv_ln_g[layer], conv_ln_b=conv_ln_b[layer], w_conv_out=w_conv_out[layer],
            b_conv_out=b_conv_out[layer], w_mix_out=w_mix_out[layer], b_mix_out=b_mix_out[layer],
            ln1_g=ln1_g[layer], ln1_b=ln1_b[layer], w_router_grp=w_router_grp[layer],
            b_router_grp=b_router_grp[layer], w_router_exp=w_router_exp[layer],
            b_router_exp=b_router_exp[layer], w_exp_gate=w_exp_gate[layer], w_exp_up=w_exp_up[layer],
            w_exp_down=w_exp_down[layer], ln2_g=ln2_g[layer], ln2_b=ln2_b[layer])
        h_lat, h_ctx = _layer(h_lat, h_ctx, s_lat, s_ctx, p, cos, sin, layer + 1 < DEPTH)
    return h_lat
```

```python
import functools

import jax
import jax.numpy as jnp
from jax import lax
from jax.experimental import pallas as pl
from jax.experimental.pallas import tpu as pltpu

GRID_W = 64
RET_HEADS = 8
RET_QK_DIM = 128
RET_V_DIM = 256
RET_CHUNK = 128
ROPE_BASE = 10000.0
CONV_WIDTH = 31
CONV_HALO = 16
N_GROUPS = 4
EXPERTS_PER_GROUP = 8
N_EXPERTS = N_GROUPS * EXPERTS_PER_GROUP
LN_EPS = 1e-5
DEPTH = 1
ALPHA = (2.0 * DEPTH) ** 0.25
ROUTER_LANES = 128

V7X_VMEM_LIMIT = 56 * 1024 * 1024

F32 = jnp.float32
BF16 = jnp.bfloat16
HIGHEST = lax.Precision.HIGHEST


def _params(*sem):
    return pltpu.CompilerParams(dimension_semantics=sem, vmem_limit_bytes=V7X_VMEM_LIMIT)


def _sigmoid(v):
    return 1.0 / (1.0 + jnp.exp(-v))


def _ln_rows(v, g, b):
    mu = jnp.mean(v, axis=-1, keepdims=True)
    d = v - mu
    var = jnp.mean(d * d, axis=-1, keepdims=True)
    return d * lax.rsqrt(var + LN_EPS) * g + b


def _ada_kernel(cs_ref, w_ref, b_ref, out_ref):
    s = cs_ref[...]
    s = s * _sigmoid(s)
    out_ref[...] = jnp.dot(s, w_ref[...], preferred_element_type=F32, precision=HIGHEST) + b_ref[...]


def _ada(cs, w_ada, b_ada):
    rows, d = cs.shape
    cols = w_ada.shape[1]
    tn = 1024
    return pl.pallas_call(
        _ada_kernel,
        grid=(cols // tn,),
        in_specs=[pl.BlockSpec((rows, d), lambda j: (0, 0)),
                  pl.BlockSpec((d, tn), lambda j: (0, j)),
                  pl.BlockSpec((1, tn), lambda j: (0, j))],
        out_specs=pl.BlockSpec((rows, tn), lambda j: (0, j)),
        out_shape=jax.ShapeDtypeStruct((rows, cols), F32),
        compiler_params=_params("arbitrary"),
        name="ada",
    )(cs, w_ada, b_ada)


def _log_sigmoid(v):
    return jnp.minimum(v, 0.0) - jnp.log(1.0 + jnp.exp(-jnp.abs(v)))


def _ctx_kernel(ctx_ref, sh_ref, sc_ref, wk_ref, wv_ref, bk_ref, bv_ref, df_ref, db_ref, sf_ref, sb_ref):
    lc = ctx_ref.shape[1]
    u = (ctx_ref[0] * (1.0 + sc_ref[0]) + sh_ref[0]).astype(BF16)
    k = (jnp.dot(u, wk_ref[...], preferred_element_type=F32) + bk_ref[...]) * RET_QK_DIM ** -0.5
    v = (jnp.dot(u, wv_ref[...], preferred_element_type=F32) + bv_ref[...]).astype(BF16)
    lgf = _log_sigmoid(df_ref[0])
    lgb = _log_sigmoid(db_ref[0])
    pos = lax.broadcasted_iota(jnp.int32, (lc, 1), 0).astype(F32)
    kf = (k * jnp.exp((lc - 1.0 - pos) * lgf)).astype(BF16)
    kb = (k * jnp.exp(pos * lgb)).astype(BF16)
    dn = (((0,), (0,)), ((), ()))
    sf_ref[0, 0] = lax.dot_general(kf, v, dn, preferred_element_type=F32)
    sb_ref[0, 0] = lax.dot_general(kb, v, dn, preferred_element_type=F32)


def _ctx_states(ctx, mod3, w_in_bf, b_in2, dec_f, dec_b, ctx_row):
    b_, lc, d = ctx.shape
    k_blk = d // RET_QK_DIM
    v_blk = (2 * d) // RET_V_DIM
    st = jax.ShapeDtypeStruct((b_, RET_HEADS, RET_QK_DIM, RET_V_DIM), F32)
    st_spec = pl.BlockSpec((1, 1, RET_QK_DIM, RET_V_DIM), lambda b, h: (b, h, 0, 0))
    return pl.pallas_call(
        _ctx_kernel,
        grid=(b_, RET_HEADS),
        in_specs=[pl.BlockSpec((1, lc, d), lambda b, h: (b, 0, 0)),
                  pl.BlockSpec((1, 1, d), lambda b, h: (ctx_row, 0, 0)),
                  pl.BlockSpec((1, 1, d), lambda b, h: (ctx_row, 0, 1)),
                  pl.BlockSpec((d, RET_QK_DIM), lambda b, h: (0, k_blk + h)),
                  pl.BlockSpec((d, RET_V_DIM), lambda b, h: (0, v_blk + h)),
                  pl.BlockSpec((1, RET_QK_DIM), lambda b, h: (0, k_blk + h)),
                  pl.BlockSpec((1, RET_V_DIM), lambda b, h: (0, v_blk + h)),
                  pl.BlockSpec((1, 1, 1), lambda b, h: (h, 0, 0)),
                  pl.BlockSpec((1, 1, 1), lambda b, h: (h, 0, 0))],
        out_specs=[st_spec, st_spec],
        out_shape=[st, st],
        compiler_params=_params("arbitrary", "arbitrary"),
        name="ctx",
    )(ctx, mod3, mod3, w_in_bf, w_in_bf, b_in2, b_in2, dec_f, dec_b)


PROJ_BLOCKS = 9
PB_Q, PB_K, PB_V, PB_G, PB_Z, PB_GATE = 0, 1, 2, 4, 6, 7


def _proj_kernel(x_ref, sh_ref, sc_ref, w_ref, w2_ref, b_ref, b2_ref, cos_ref, sin_ref, out_ref, u_scr):
    j = pl.program_id(1)

    @pl.when(j == 0)
    def _():
        u_scr[...] = (x_ref[...] * (1.0 + sc_ref[0]) + sh_ref[0]).astype(BF16)

    def mm(w, b):
        return jnp.dot(u_scr[...], w[...], preferred_element_type=F32) + b[...]

    def rope(acc):
        cos = cos_ref[...]
        sin = sin_ref[...]
        for h in range(acc.shape[1] // RET_QK_DIM):
            sl = slice(h * RET_QK_DIM, (h + 1) * RET_QK_DIM)
            seg = acc[:, sl]
            out_ref[:, sl] = (seg * cos + pltpu.roll(seg, RET_QK_DIM // 2, 1) * sin).astype(BF16)

    @pl.when(j == PB_Q)
    def _():
        rope(mm(w_ref, b_ref))

    @pl.when(j == PB_K)
    def _():
        rope(mm(w_ref, b_ref) * RET_QK_DIM ** -0.5)

    @pl.when((j >= PB_V) & (j < PB_G))
    def _():
        out_ref[...] = mm(w_ref, b_ref).astype(BF16)

    @pl.when((j >= PB_G) & (j < PB_Z))
    def _():
        acc = mm(w_ref, b_ref)
        out_ref[...] = (acc * _sigmoid(acc)).astype(BF16)

    @pl.when(j == PB_Z)
    def _():
        out_ref[...] = (mm(w_ref, b_ref) * _sigmoid(mm(w2_ref, b2_ref))).astype(BF16)

    @pl.when(j >= PB_GATE)
    def _():
        out_ref[...] = _sigmoid(mm(w_ref, b_ref)).astype(BF16)


def _proj(x2, mod3, w_in_bf, b_in2, cos, sin, seq, tm):
    n, d = x2.shape
    tiles_per_seq = seq // tm

    def wcol(i, j):
        return (0, j + (j > PB_Z).astype(jnp.int32))

    return pl.pallas_call(
        _proj_kernel,
        grid=(n // tm, PROJ_BLOCKS),
        in_specs=[pl.BlockSpec((tm, d), lambda i, j: (i, 0)),
                  pl.BlockSpec((1, 1, d), lambda i, j: (i // tiles_per_seq, 0, 0)),
                  pl.BlockSpec((1, 1, d), lambda i, j: (i // tiles_per_seq, 0, 1)),
                  pl.BlockSpec((d, d), wcol),
                  pl.BlockSpec((d, d), lambda i, j: (0, PB_Z + 1)),
                  pl.BlockSpec((1, d), wcol),
                  pl.BlockSpec((1, d), lambda i, j: (0, PB_Z + 1)),
                  pl.BlockSpec((tm, RET_QK_DIM), lambda i, j: (i % tiles_per_seq, 0)),
                  pl.BlockSpec((tm, RET_QK_DIM), lambda i, j: (i % tiles_per_seq, 0))],
        out_specs=pl.BlockSpec((tm, d), lambda i, j: (i, j)),
        out_shape=jax.ShapeDtypeStruct((n, PROJ_BLOCKS * d), BF16),
        scratch_shapes=[pltpu.VMEM((tm, d), BF16)],
        compiler_params=_params("arbitrary", "arbitrary"),
        name="proj",
    )(x2, mod3, mod3, w_in_bf, w_in_bf, b_in2, b_in2, cos, sin)


def _ret_kernel(q_ref, k_ref, v_ref, g_ref, s0f_ref, s0b_ref, df_ref, db_ref, out_ref, sb_scr, st_scr):
    c_ = RET_CHUNK
    n_chunks = q_ref.shape[1] // c_
    lgf = _log_sigmoid(df_ref[0])
    lgb = _log_sigmoid(db_ref[0])
    ri = lax.broadcasted_iota(jnp.int32, (c_, c_), 0)
    ci = lax.broadcasted_iota(jnp.int32, (c_, c_), 1)
    diff = (ri - ci).astype(F32)
    mask = jnp.where(diff > 0, jnp.exp(diff * lgf), jnp.where(diff < 0, jnp.exp(-diff * lgb), 2.0))
    pos = lax.broadcasted_iota(jnp.int32, (c_, 1), 0).astype(F32)
    qdec_f = jnp.exp((pos + 1.0) * lgf)
    qdec_b = jnp.exp((c_ - pos) * lgb)
    kdec_f = jnp.exp((c_ - 1.0 - pos) * lgf)
    kdec_b = jnp.exp(pos * lgb)
    cdec_f = jnp.exp(c_ * lgf)
    cdec_b = jnp.exp(c_ * lgb)
    dn_t = (((0,), (0,)), ((), ()))

    def chunk(c):
        return pl.ds(pl.multiple_of(c * c_, c_), c_)

    st_scr[...] = s0b_ref[0, 0]

    def bwd(t, carry):
        c = n_chunks - 1 - t
        sb_scr[c] = st_scr[...].astype(BF16)
        kb = (k_ref[0, chunk(c), :].astype(F32) * kdec_b).astype(BF16)
        st_scr[...] = st_scr[...] * cdec_b + lax.dot_general(kb, v_ref[0, chunk(c), :], dn_t,
                                                             preferred_element_type=F32)
        return carry

    lax.fori_loop(0, n_chunks, bwd, 0)

    st_scr[...] = s0f_ref[0, 0]

    def fwd(c, carry):
        q = q_ref[0, chunk(c), :]
        k = k_ref[0, chunk(c), :]
        v = v_ref[0, chunk(c), :]
        s = lax.dot_general(q, k, (((1,), (1,)), ((), ())), preferred_element_type=F32)
        o = jnp.dot((s * mask).astype(BF16), v, preferred_element_type=F32)
        o = o + qdec_f * jnp.dot(q, st_scr[...].astype(BF16), preferred_element_type=F32)
        o = o + qdec_b * jnp.dot(q, sb_scr[c], preferred_element_type=F32)
        mu = jnp.mean(o, axis=-1, keepdims=True)
        d = o - mu
        var = jnp.mean(d * d, axis=-1, keepdims=True)
        on = d * lax.rsqrt(var + LN_EPS)
        out_ref[0, chunk(c), :] = (on * g_ref[0, chunk(c), :].astype(F32)).astype(BF16)
        kf = (k.astype(F32) * kdec_f).astype(BF16)
        st_scr[...] = st_scr[...] * cdec_f + lax.dot_general(kf, v, dn_t, preferred_element_type=F32)
        return carry

    lax.fori_loop(0, n_chunks, fwd, 0)


def _retention(proj3, s0f, s0b, dec_f, dec_b, d):
    b_, seq, _ = proj3.shape
    qb = PB_Q * d // RET_QK_DIM
    kb = PB_K * d // RET_QK_DIM
    vb = PB_V * d // RET_V_DIM
    gb = PB_G * d // RET_V_DIM
    st_spec = pl.BlockSpec((1, 1, RET_QK_DIM, RET_V_DIM), lambda b, h: (b, h, 0, 0))
    return pl.pallas_call(
        _ret_kernel,
        grid=(b_, RET_HEADS),
        in_specs=[pl.BlockSpec((1, seq, RET_QK_DIM), lambda b, h: (b, 0, qb + h)),
                  pl.BlockSpec((1, seq, RET_QK_DIM), lambda b, h: (b, 0, kb + h)),
                  pl.BlockSpec((1, seq, RET_V_DIM), lambda b, h: (b, 0, vb + h)),
                  pl.BlockSpec((1, seq, RET_V_DIM), lambda b, h: (b, 0, gb + h)),
                  st_spec, st_spec,
                  pl.BlockSpec((1, 1, 1), lambda b, h: (h, 0, 0)),
                  pl.BlockSpec((1, 1, 1), lambda b, h: (h, 0, 0))],
        out_specs=pl.BlockSpec((1, seq, RET_V_DIM), lambda b, h: (b, 0, h)),
        out_shape=jax.ShapeDtypeStruct((b_, seq, RET_HEADS * RET_V_DIM), BF16),
        scratch_shapes=[pltpu.VMEM((seq // RET_CHUNK, RET_QK_DIM, RET_V_DIM), BF16),
                        pltpu.VMEM((RET_QK_DIM, RET_V_DIM), F32)],
        compiler_params=_params("arbitrary", "arbitrary"),
        name="ret",
    )(proj3, proj3, proj3, proj3, s0f, s0b, dec_f, dec_b)


VEC_CONV_B, VEC_CLN_G, VEC_CLN_B, VEC_BCONV, VEC_BMIX, VEC_LN1_G, VEC_LN1_B = range(7)
CONV_ROWS = 32


def _merge_kernel(ret_ref, z_ref, zp_ref, zn_ref, ga_ref, gb_ref, x_ref, gt_ref, shf_ref, scf_ref,
                  wret_ref, wconv_ref, wmix_ref, dw_ref, vec_ref, wr_ref, br_ref,
                  h1_ref, t_ref, comb_ref, zext, zc_scr, *, tiles_per_seq):
    tm = x_ref.shape[0]
    ti = pl.program_id(0) % tiles_per_seq
    vec = vec_ref[...]

    def row(r):
        return vec[r:r + 1, :]

    y_a = jnp.dot(ret_ref[...], wret_ref[...], preferred_element_type=F32)

    has_prev = (ti > 0).astype(F32)
    has_next = (ti < tiles_per_seq - 1).astype(F32)
    zext[0:CONV_HALO, :] = zp_ref[...].astype(F32) * has_prev
    zext[CONV_HALO:CONV_HALO + tm, :] = z_ref[...].astype(F32)
    zext[CONV_HALO + tm:, :] = zn_ref[...].astype(F32) * has_next
    base = CONV_HALO - CONV_WIDTH // 2

    def conv_rows(r, carry):
        r0 = pl.multiple_of(r * CONV_ROWS, CONV_ROWS)
        acc = jnp.zeros((CONV_ROWS, zext.shape[1]), F32) + row(VEC_CONV_B)
        win = zext[pl.ds(r0, CONV_ROWS + 2 * CONV_HALO), :]
        for w in range(CONV_WIDTH):
            acc = acc + win[base + w:base + w + CONV_ROWS, :] * dw_ref[w:w + 1, :]
        zc_scr[pl.ds(r0, CONV_ROWS), :] = acc
        return carry

    lax.fori_loop(0, tm // CONV_ROWS, conv_rows, 0)
    zc = _ln_rows(zc_scr[...], row(VEC_CLN_G), row(VEC_CLN_B))
    zc = zc * _sigmoid(zc)
    y_b = jnp.dot(zc.astype(BF16), wconv_ref[...], preferred_element_type=F32) + row(VEC_BCONV)

    mixed = ga_ref[...].astype(F32) * y_a + gb_ref[...].astype(F32) * y_b
    mix = jnp.dot(mixed.astype(BF16), wmix_ref[...], preferred_element_type=F32) + row(VEC_BMIX)
    h1 = _ln_rows(ALPHA * x_ref[...] + gt_ref[0] * mix, row(VEC_LN1_G), row(VEC_LN1_B))
    h1_ref[...] = h1
    t = h1 * (1.0 + scf_ref[0]) + shf_ref[0]
    t_ref[...] = t.astype(BF16)

    logits = jnp.dot(t, wr_ref[...], preferred_element_type=F32, precision=HIGHEST) + br_ref[...]
    lane = lax.broadcasted_iota(jnp.int32, logits.shape, 1)
    neg = -jnp.inf
    is_grp = lane < N_GROUPS
    gl = jnp.where(is_grp, logits, neg)
    gmax = jnp.max(gl, axis=-1, keepdims=True)
    gidx = jnp.min(jnp.where(gl == gmax, lane, ROUTER_LANES), axis=-1, keepdims=True)
    gsum = jnp.sum(jnp.where(is_grp, jnp.exp(gl - gmax), 0.0), axis=-1, keepdims=True)
    grp_w = 1.0 / gsum
    lo = N_GROUPS + gidx * EXPERTS_PER_GROUP
    el = jnp.where((lane >= lo) & (lane < lo + EXPERTS_PER_GROUP), logits, neg)
    m1 = jnp.max(el, axis=-1, keepdims=True)
    i1 = jnp.min(jnp.where(el == m1, lane, ROUTER_LANES), axis=-1, keepdims=True)
    el2 = jnp.where(lane == i1, neg, el)
    m2 = jnp.max(el2, axis=-1, keepdims=True)
    i2 = jnp.min(jnp.where(el2 == m2, lane, ROUTER_LANES), axis=-1, keepdims=True)
    r = jnp.exp(m2 - m1)
    w1 = grp_w / (1.0 + r)
    w2 = grp_w * r / (1.0 + r)
    comb_ref[...] = jnp.where(lane == i1, w1, 0.0) + jnp.where(lane == i2, w2, 0.0)


def _merge(ret2, proj2, x2, mod3, wret, wconv, wmix, dw, vec, wr, br, seq, tm):
    n, d = x2.shape
    tiles_per_seq = seq // tm
    hb = tm // CONV_HALO
    last_hb = n // CONV_HALO - 1
    const = lambda i: (0, 0)
    bat = lambda k: (lambda i: (i // tiles_per_seq, 0, k))
    return pl.pallas_call(
        functools.partial(_merge_kernel, tiles_per_seq=tiles_per_seq),
        grid=(n // tm,),
        in_specs=[pl.BlockSpec((tm, ret2.shape[1]), lambda i: (i, 0)),
                  pl.BlockSpec((tm, d), lambda i: (i, PB_Z)),
                  pl.BlockSpec((CONV_HALO, d), lambda i: (jnp.maximum(i * hb - 1, 0), PB_Z)),
                  pl.BlockSpec((CONV_HALO, d), lambda i: (jnp.minimum((i + 1) * hb, last_hb), PB_Z)),
                  pl.BlockSpec((tm, d), lambda i: (i, PB_GATE)),
                  pl.BlockSpec((tm, d), lambda i: (i, PB_GATE + 1)),
                  pl.BlockSpec((tm, d), lambda i: (i, 0)),
                  pl.BlockSpec((1, 1, d), bat(2)),
                  pl.BlockSpec((1, 1, d), bat(3)),
                  pl.BlockSpec((1, 1, d), bat(4)),
                  pl.BlockSpec(wret.shape, const),
                  pl.BlockSpec(wconv.shape, const),
                  pl.BlockSpec(wmix.shape, const),
                  pl.BlockSpec(dw.shape, const),
                  pl.BlockSpec(vec.shape, const),
                  pl.BlockSpec(wr.shape, const),
                  pl.BlockSpec(br.shape, const)],
        out_specs=[pl.BlockSpec((tm, d), lambda i: (i, 0)),
                   pl.BlockSpec((tm, d), lambda i: (i, 0)),
                   pl.BlockSpec((tm, ROUTER_LANES), lambda i: (i, 0))],
        out_shape=[jax.ShapeDtypeStruct((n, d), F32),
                   jax.ShapeDtypeStruct((n, d), BF16),
                   jax.ShapeDtypeStruct((n, ROUTER_LANES), F32)],
        scratch_shapes=[pltpu.VMEM((tm + 2 * CONV_HALO, d), F32),
                        pltpu.VMEM((tm, d), F32)],
        compiler_params=_params("arbitrary"),
        name="merge",
    )(ret2, proj2, proj2, proj2, proj2, proj2, x2, mod3, mod3, mod3,
      wret, wconv, wmix, dw, vec, wr, br)


def _moe_kernel(t_ref, comb_ref, wg_ref, wu_ref, wd_ref, h1_ref, gt_ref, ln_ref, out_ref, acc):
    e = pl.program_id(1)

    @pl.when(e == 0)
    def _():
        acc[...] = jnp.zeros_like(acc)

    t = t_ref[...]
    g = jnp.dot(t, wg_ref[0, 0].astype(BF16), preferred_element_type=F32)
    u = jnp.dot(t, wu_ref[0, 0].astype(BF16), preferred_element_type=F32)
    hid = (g * _sigmoid(g) * u).astype(BF16)
    y = jnp.dot(hid, wd_ref[0, 0].astype(BF16), preferred_element_type=F32)
    lane = lax.broadcasted_iota(jnp.int32, comb_ref.shape, 1)
    cw = jnp.sum(jnp.where(lane == N_GROUPS + e, comb_ref[...], 0.0), axis=-1, keepdims=True)
    acc[...] += cw * y

    @pl.when(e == N_EXPERTS - 1)
    def _():
        out_ref[...] = _ln_rows(ALPHA * h1_ref[...] + gt_ref[0] * acc[...], ln_ref[0:1, :], ln_ref[1:2, :])


def _moe(t2, comb, w_gate, w_up, w_down, h1, mod3, ln2, seq, tm):
    n, d = t2.shape
    ff = w_gate.shape[-1]
    tiles_per_seq = seq // tm
    return pl.pallas_call(
        _moe_kernel,
        grid=(n // tm, N_EXPERTS),
        in_specs=[pl.BlockSpec((tm, d), lambda i, e: (i, 0)),
                  pl.BlockSpec((tm, ROUTER_LANES), lambda i, e: (i, 0)),
                  pl.BlockSpec((1, 1, d, ff), lambda i, e: (0, e, 0, 0)),
                  pl.BlockSpec((1, 1, d, ff), lambda i, e: (0, e, 0, 0)),
                  pl.BlockSpec((1, 1, ff, d), lambda i, e: (0, e, 0, 0)),
                  pl.BlockSpec((tm, d), lambda i, e: (i, 0)),
                  pl.BlockSpec((1, 1, d), lambda i, e: (i // tiles_per_seq, 0, 5)),
                  pl.BlockSpec(ln2.shape, lambda i, e: (0, 0))],
        out_specs=pl.BlockSpec((tm, d), lambda i, e: (i, 0)),
        out_shape=jax.ShapeDtypeStruct((n, d), F32),
        scratch_shapes=[pltpu.VMEM((tm, d), F32)],
        compiler_params=_params("arbitrary", "arbitrary"),
        name="moe",
    )(t2, comb, w_gate, w_up, w_down, h1, mod3, ln2)


def _rope_tables(seq):
    pos = jnp.arange(seq)
    r = (pos // GRID_W).astype(F32)
    col = (pos % GRID_W).astype(F32)
    n_freq = RET_QK_DIM // 4
    inv = ROPE_BASE ** (-jnp.arange(n_freq, dtype=F32) / n_freq)
    ang = jnp.concatenate([r[:, None] * inv, col[:, None] * inv], axis=-1)
    ang = jnp.concatenate([ang, ang], axis=-1)
    sign = jnp.concatenate([-jnp.ones((RET_QK_DIM // 2,), F32), jnp.ones((RET_QK_DIM // 2,), F32)])
    return jnp.cos(ang), jnp.sin(ang) * sign


def kernel(x, c, ctx, c_ctx, w_ada, b_ada, w_in, b_in, ret_decay_fwd, ret_decay_bwd, w_ret_out, conv_dw, conv_dw_b, conv_ln_g, conv_ln_b, w_conv_out, b_conv_out, w_mix_out, b_mix_out, ln1_g, ln1_b, w_router_grp, b_router_grp, w_router_exp, b_router_exp, w_exp_gate, w_exp_up, w_exp_down, ln2_g, ln2_b):
    b_, seq, d = x.shape
    n = b_ * seq
    assert w_ada.shape[0] == DEPTH
    mod_rows = 8
    assert b_ + 1 <= mod_rows

    cs = jnp.concatenate([c, c_ctx[None, :], jnp.zeros((mod_rows - b_ - 1, d), F32)], axis=0)
    mod = _ada(cs, w_ada[0], b_ada[0][None, :])
    mod3 = mod.reshape(mod_rows, 1, 6 * d)

    w_in_bf = w_in[0].astype(BF16)
    b_in2 = b_in[0][None, :]
    dec_f = ret_decay_fwd[0].reshape(RET_HEADS, 1, 1)
    dec_b = ret_decay_bwd[0].reshape(RET_HEADS, 1, 1)

    s0f, s0b = _ctx_states(ctx, mod3, w_in_bf, b_in2, dec_f, dec_b, b_)

    cos, sin = _rope_tables(seq)
    x2 = x.reshape(n, d)
    proj = _proj(x2, mod3, w_in_bf, b_in2, cos, sin, seq, tm=1024)

    ret = _retention(proj.reshape(b_, seq, PROJ_BLOCKS * d), s0f, s0b, dec_f, dec_b, d)

    vec = jnp.concatenate([conv_dw_b, conv_ln_g, conv_ln_b, b_conv_out, b_mix_out, ln1_g, ln1_b,
                           jnp.zeros((1, d), F32)], axis=0)
    dw = jnp.concatenate([conv_dw[0], jnp.zeros((32 - CONV_WIDTH, d), F32)], axis=0)
    pad = ROUTER_LANES - N_GROUPS - N_EXPERTS
    wr = jnp.concatenate([w_router_grp[0], w_router_exp[0], jnp.zeros((d, pad), F32)], axis=1)
    br = jnp.concatenate([b_router_grp[0], b_router_exp[0], jnp.zeros((pad,), F32)])[None, :]
    h1, t, comb = _merge(ret.reshape(n, RET_HEADS * RET_V_DIM), proj, x2, mod3,
                         w_ret_out[0].astype(BF16), w_conv_out[0].astype(BF16), w_mix_out[0].astype(BF16),
                         dw, vec, wr, br, seq, tm=512)

    ln2 = jnp.concatenate([ln2_g, ln2_b, jnp.zeros((6, d), F32)], axis=0)
    out = _moe(t, comb, w_exp_gate, w_exp_up, w_exp_down, h1, mod3, ln2, seq, tm=1024)
    return out.reshape(b_, seq, d)
```

```python
import functools

import jax
import jax.numpy as jnp
from jax import lax
from jax.experimental import pallas as pl
from jax.experimental.pallas import tpu as pltpu

GRID_W = 64
RET_HEADS = 8
RET_QK_DIM = 128
RET_V_DIM = 256
RET_CHUNK = 128
ROPE_BASE = 10000.0
CONV_WIDTH = 31
CONV_HALO = 16
SUBLANES = 8
N_GROUPS = 4
EXPERTS_PER_GROUP = 8
N_EXPERTS = N_GROUPS * EXPERTS_PER_GROUP
EXPERT_TOP_K = 2
LN_EPS = 1e-5
DEPTH = 1
ALPHA = (2.0 * DEPTH) ** 0.25
ROUTER_LANES = 128

V7X_VMEM_LIMIT = 56 * 1024 * 1024

F32 = jnp.float32
BF16 = jnp.bfloat16
I32 = jnp.int32
HIGHEST = lax.Precision.HIGHEST


def _params(*sem):
    return pltpu.CompilerParams(dimension_semantics=sem, vmem_limit_bytes=V7X_VMEM_LIMIT)


def _sigmoid(v):
    return 1.0 / (1.0 + jnp.exp(-v))


def _ln_rows(v, g, b):
    mu = jnp.mean(v, axis=-1, keepdims=True)
    d = v - mu
    var = jnp.mean(d * d, axis=-1, keepdims=True)
    return d * lax.rsqrt(var + LN_EPS) * g + b


def _ada_kernel(cs_ref, w_ref, b_ref, out_ref):
    s = cs_ref[...]
    s = s * _sigmoid(s)
    out_ref[...] = jnp.dot(s, w_ref[...], preferred_element_type=F32, precision=HIGHEST) + b_ref[...]


def _ada(cs, w_ada, b_ada):
    rows, d = cs.shape
    cols = w_ada.shape[1]
    tn = 1024
    return pl.pallas_call(
        _ada_kernel,
        grid=(cols // tn,),
        in_specs=[pl.BlockSpec((rows, d), lambda j: (0, 0)),
                  pl.BlockSpec((d, tn), lambda j: (0, j)),
                  pl.BlockSpec((1, tn), lambda j: (0, j))],
        out_specs=pl.BlockSpec((rows, tn), lambda j: (0, j)),
        out_shape=jax.ShapeDtypeStruct((rows, cols), F32),
        compiler_params=_params("arbitrary"),
        name="ada",
    )(cs, w_ada, b_ada)


def _log_sigmoid(v):
    return jnp.minimum(v, 0.0) - jnp.log(1.0 + jnp.exp(-jnp.abs(v)))


def _ctx_kernel(ctx_ref, sh_ref, sc_ref, wk_ref, wv_ref, bk_ref, bv_ref, df_ref, db_ref, sf_ref, sb_ref):
    lc = ctx_ref.shape[1]
    u = (ctx_ref[0] * (1.0 + sc_ref[0]) + sh_ref[0]).astype(BF16)
    k = (jnp.dot(u, wk_ref[...], preferred_element_type=F32) + bk_ref[...]) * RET_QK_DIM ** -0.5
    v = (jnp.dot(u, wv_ref[...], preferred_element_type=F32) + bv_ref[...]).astype(BF16)
    lgf = _log_sigmoid(df_ref[0])
    lgb = _log_sigmoid(db_ref[0])
    pos = lax.broadcasted_iota(I32, (lc, 1), 0).astype(F32)
    kf = (k * jnp.exp((lc - 1.0 - pos) * lgf)).astype(BF16)
    kb = (k * jnp.exp(pos * lgb)).astype(BF16)
    dn = (((0,), (0,)), ((), ()))
    sf_ref[0, 0] = lax.dot_general(kf, v, dn, preferred_element_type=F32)
    sb_ref[0, 0] = lax.dot_general(kb, v, dn, preferred_element_type=F32)


def _ctx_states(ctx, mod3, w_in_bf, b_in2, dec_f, dec_b, ctx_row):
    b_, lc, d = ctx.shape
    k_blk = d // RET_QK_DIM
    v_blk = (2 * d) // RET_V_DIM
    st = jax.ShapeDtypeStruct((b_, RET_HEADS, RET_QK_DIM, RET_V_DIM), F32)
    st_spec = pl.BlockSpec((1, 1, RET_QK_DIM, RET_V_DIM), lambda b, h: (b, h, 0, 0))
    return pl.pallas_call(
        _ctx_kernel,
        grid=(b_, RET_HEADS),
        in_specs=[pl.BlockSpec((1, lc, d), lambda b, h: (b, 0, 0)),
                  pl.BlockSpec((1, 1, d), lambda b, h: (ctx_row, 0, 0)),
                  pl.BlockSpec((1, 1, d), lambda b, h: (ctx_row, 0, 1)),
                  pl.BlockSpec((d, RET_QK_DIM), lambda b, h: (0, k_blk + h)),
                  pl.BlockSpec((d, RET_V_DIM), lambda b, h: (0, v_blk + h)),
                  pl.BlockSpec((1, RET_QK_DIM), lambda b, h: (0, k_blk + h)),
                  pl.BlockSpec((1, RET_V_DIM), lambda b, h: (0, v_blk + h)),
                  pl.BlockSpec((1, 1, 1), lambda b, h: (h, 0, 0)),
                  pl.BlockSpec((1, 1, 1), lambda b, h: (h, 0, 0))],
        out_specs=[st_spec, st_spec],
        out_shape=[st, st],
        compiler_params=_params("arbitrary", "arbitrary"),
        name="ctx",
    )(ctx, mod3, mod3, w_in_bf, w_in_bf, b_in2, b_in2, dec_f, dec_b)


PROJ_BLOCKS = 9
PB_Q, PB_K, PB_V, PB_G, PB_Z, PB_GATE = 0, 1, 2, 4, 6, 7


def _proj_kernel(x_ref, sh_ref, sc_ref, w_ref, w2_ref, b_ref, b2_ref, cos_ref, sin_ref, out_ref, u_scr):
    j = pl.program_id(1)

    @pl.when(j == 0)
    def _():
        u_scr[...] = (x_ref[...] * (1.0 + sc_ref[0]) + sh_ref[0]).astype(BF16)

    def mm(w, b):
        return jnp.dot(u_scr[...], w[...], preferred_element_type=F32) + b[...]

    def rope(acc):
        cos = cos_ref[...]
        sin = sin_ref[...]
        for h in range(acc.shape[1] // RET_QK_DIM):
            sl = slice(h * RET_QK_DIM, (h + 1) * RET_QK_DIM)
            seg = acc[:, sl]
            out_ref[:, sl] = (seg * cos + pltpu.roll(seg, RET_QK_DIM // 2, 1) * sin).astype(BF16)

    @pl.when(j == PB_Q)
    def _():
        rope(mm(w_ref, b_ref))

    @pl.when(j == PB_K)
    def _():
        rope(mm(w_ref, b_ref) * RET_QK_DIM ** -0.5)

    @pl.when((j >= PB_V) & (j < PB_G))
    def _():
        out_ref[...] = mm(w_ref, b_ref).astype(BF16)

    @pl.when((j >= PB_G) & (j < PB_Z))
    def _():
        acc = mm(w_ref, b_ref)
        out_ref[...] = (acc * _sigmoid(acc)).astype(BF16)

    @pl.when(j == PB_Z)
    def _():
        out_ref[...] = (mm(w_ref, b_ref) * _sigmoid(mm(w2_ref, b2_ref))).astype(BF16)

    @pl.when(j >= PB_GATE)
    def _():
        out_ref[...] = _sigmoid(mm(w_ref, b_ref)).astype(BF16)


def _proj(x2, mod3, w_in_bf, b_in2, cos, sin, seq, tm):
    n, d = x2.shape
    tiles_per_seq = seq // tm

    def wcol(i, j):
        return (0, j + (j > PB_Z).astype(I32))

    return pl.pallas_call(
        _proj_kernel,
        grid=(n // tm, PROJ_BLOCKS),
        in_specs=[pl.BlockSpec((tm, d), lambda i, j: (i, 0)),
                  pl.BlockSpec((1, 1, d), lambda i, j: (i // tiles_per_seq, 0, 0)),
                  pl.BlockSpec((1, 1, d), lambda i, j: (i // tiles_per_seq, 0, 1)),
                  pl.BlockSpec((d, d), wcol),
                  pl.BlockSpec((d, d), lambda i, j: (0, PB_Z + 1)),
                  pl.BlockSpec((1, d), wcol),
                  pl.BlockSpec((1, d), lambda i, j: (0, PB_Z + 1)),
                  pl.BlockSpec((tm, RET_QK_DIM), lambda i, j: (i % tiles_per_seq, 0)),
                  pl.BlockSpec((tm, RET_QK_DIM), lambda i, j: (i % tiles_per_seq, 0))],
        out_specs=pl.BlockSpec((tm, d), lambda i, j: (i, j)),
        out_shape=jax.ShapeDtypeStruct((n, PROJ_BLOCKS * d), BF16),
        scratch_shapes=[pltpu.VMEM((tm, d), BF16)],
        compiler_params=_params("arbitrary", "arbitrary"),
        name="proj",
    )(x2, mod3, mod3, w_in_bf, w_in_bf, b_in2, b_in2, cos, sin)


RET_HEADS_PER_STEP = 2


def _ret_kernel(q_ref, k_ref, v_ref, g_ref, s0f_ref, s0b_ref, df_ref, db_ref, out_ref,
                sf_all, sb_all, stf_scr, stb_scr):
    c_ = RET_CHUNK
    hs = RET_HEADS_PER_STEP
    n_chunks = q_ref.shape[1] // c_
    dn_t = (((0,), (0,)), ((), ()))
    ri = lax.broadcasted_iota(I32, (c_, c_), 0)
    ci = lax.broadcasted_iota(I32, (c_, c_), 1)
    diff = (ri - ci).astype(F32)
    pos = lax.broadcasted_iota(I32, (c_, 1), 0).astype(F32)

    def chunk(c):
        return pl.ds(pl.multiple_of(c * c_, c_), c_)

    def qk(h):
        return slice(h * RET_QK_DIM, (h + 1) * RET_QK_DIM)

    def vv(h):
        return slice(h * RET_V_DIM, (h + 1) * RET_V_DIM)

    lgf = [_log_sigmoid(df_ref[h]) for h in range(hs)]
    lgb = [_log_sigmoid(db_ref[h]) for h in range(hs)]

    for h in range(hs):
        stf_scr[h] = s0f_ref[0, h]
        stb_scr[h] = s0b_ref[0, h]

    def scan(t, carry):
        cf = t
        cb = n_chunks - 1 - t
        for h in range(hs):
            kdec_f = jnp.exp((c_ - 1.0 - pos) * lgf[h])
            kdec_b = jnp.exp(pos * lgb[h])
            sf = stf_scr[h]
            sb = stb_scr[h]
            sf_all[h, cf] = sf.astype(BF16)
            sb_all[h, cb] = sb.astype(BF16)
            kf = (k_ref[0, chunk(cf), qk(h)].astype(F32) * kdec_f).astype(BF16)
            kb = (k_ref[0, chunk(cb), qk(h)].astype(F32) * kdec_b).astype(BF16)
            stf_scr[h] = sf * jnp.exp(c_ * lgf[h]) + lax.dot_general(
                kf, v_ref[0, chunk(cf), vv(h)], dn_t, preferred_element_type=F32)
            stb_scr[h] = sb * jnp.exp(c_ * lgb[h]) + lax.dot_general(
                kb, v_ref[0, chunk(cb), vv(h)], dn_t, preferred_element_type=F32)
        return carry

    lax.fori_loop(0, n_chunks, scan, 0)

    def outputs(c, carry):
        for h in range(hs):
            mask = jnp.where(diff > 0, jnp.exp(diff * lgf[h]),
                             jnp.where(diff < 0, jnp.exp(-diff * lgb[h]), 2.0))
            qdec_f = jnp.exp((pos + 1.0) * lgf[h])
            qdec_b = jnp.exp((c_ - pos) * lgb[h])
            q = q_ref[0, chunk(c), qk(h)]
            k = k_ref[0, chunk(c), qk(h)]
            v = v_ref[0, chunk(c), vv(h)]
            s = lax.dot_general(q, k, (((1,), (1,)), ((), ())), preferred_element_type=F32)
            o = jnp.dot((s * mask).astype(BF16), v, preferred_element_type=F32)
            o = o + qdec_f * jnp.dot(q, sf_all[h, c], preferred_element_type=F32)
            o = o + qdec_b * jnp.dot(q, sb_all[h, c], preferred_element_type=F32)
            mu = jnp.mean(o, axis=-1, keepdims=True)
            d = o - mu
            var = jnp.mean(d * d, axis=-1, keepdims=True)
            on = d * lax.rsqrt(var + LN_EPS)
            out_ref[0, chunk(c), vv(h)] = (on * g_ref[0, chunk(c), vv(h)].astype(F32)).astype(BF16)
        return carry

    lax.fori_loop(0, n_chunks, outputs, 0, unroll=2)


def _retention(proj3, s0f, s0b, dec_f, dec_b, d):
    b_, seq, _ = proj3.shape
    hs = RET_HEADS_PER_STEP
    qw, vw = hs * RET_QK_DIM, hs * RET_V_DIM
    qb, kb, vb, gb = PB_Q * d // qw, PB_K * d // qw, PB_V * d // vw, PB_G * d // vw
    n_chunks = seq // RET_CHUNK
    st_spec = pl.BlockSpec((1, hs, RET_QK_DIM, RET_V_DIM), lambda b, h: (b, h, 0, 0))
    st_all = pltpu.VMEM((hs, n_chunks, RET_QK_DIM, RET_V_DIM), BF16)
    st_one = pltpu.VMEM((hs, RET_QK_DIM, RET_V_DIM), F32)
    return pl.pallas_call(
        _ret_kernel,
        grid=(b_, RET_HEADS // hs),
        in_specs=[pl.BlockSpec((1, seq, qw), lambda b, h: (b, 0, qb + h)),
                  pl.BlockSpec((1, seq, qw), lambda b, h: (b, 0, kb + h)),
                  pl.BlockSpec((1, seq, vw), lambda b, h: (b, 0, vb + h)),
                  pl.BlockSpec((1, seq, vw), lambda b, h: (b, 0, gb + h)),
                  st_spec, st_spec,
                  pl.BlockSpec((hs, 1, 1), lambda b, h: (h, 0, 0)),
                  pl.BlockSpec((hs, 1, 1), lambda b, h: (h, 0, 0))],
        out_specs=pl.BlockSpec((1, seq, vw), lambda b, h: (b, 0, h)),
        out_shape=jax.ShapeDtypeStruct((b_, seq, RET_HEADS * RET_V_DIM), BF16),
        scratch_shapes=[st_all, st_all, st_one, st_one],
        compiler_params=_params("arbitrary", "arbitrary"),
        name="ret",
    )(proj3, proj3, proj3, proj3, s0f, s0b, dec_f, dec_b)


VEC_CONV_B, VEC_CLN_G, VEC_CLN_B, VEC_BCONV, VEC_BMIX, VEC_LN1_G, VEC_LN1_B = range(7)
CONV_ROWS = 64
LANES = 128
RT_E1, RT_E2, RT_RANK1, RT_RANK2 = 0, 1, 2, 3
RT_W1, RT_W2 = 0, 1


def _split_bf16(v):
    hi = v.astype(BF16)
    return hi, (v - hi.astype(F32)).astype(BF16)


def _merge_kernel(ret_ref, z_ref, zp_ref, zn_ref, ga_ref, gb_ref, x_ref, gt_ref, shf_ref, scf_ref,
                  wret_ref, wconv_ref, wmix_ref, dw_ref, vec_ref, wrh_ref, wrl_ref, br_ref,
                  h1_ref, t_ref, ri_ref, rw_ref, cnt_ref, zext, zc_scr, cnt_scr, *, tiles_per_seq):
    tm, d = x_ref.shape
    step = pl.program_id(0)
    ti = step % tiles_per_seq
    vec = vec_ref[...]

    def row(r):
        return vec[r:r + 1, :]

    y_a = jnp.dot(ret_ref[...], wret_ref[...], preferred_element_type=F32)

    has_prev = (ti > 0).astype(F32)
    has_next = (ti < tiles_per_seq - 1).astype(F32)
    zext[0:CONV_HALO, :] = zp_ref[...].astype(F32) * has_prev
    zext[CONV_HALO:CONV_HALO + tm, :] = z_ref[...].astype(F32)
    zext[CONV_HALO + tm:, :] = zn_ref[...].astype(F32) * has_next
    base = CONV_HALO - CONV_WIDTH // 2
    win_rows = CONV_ROWS + 2 * CONV_HALO

    def conv_rows(r, carry):
        r0 = pl.multiple_of(r * CONV_ROWS, CONV_ROWS)
        for lt in range(d // LANES):
            ls = slice(lt * LANES, (lt + 1) * LANES)
            win = zext[pl.ds(r0, win_rows), ls]
            acc = jnp.zeros((CONV_ROWS, LANES), F32) + vec_ref[VEC_CONV_B:VEC_CONV_B + 1, ls]
            for s in range(SUBLANES):
                taps = [w for w in range(CONV_WIDTH) if (base + w) % SUBLANES == s]
                if not taps:
                    continue
                sh = win if s == 0 else pltpu.roll(win, win_rows - s, 0)
                for w in taps:
                    a = (base + w) - s
                    acc = acc + sh[a:a + CONV_ROWS, :] * dw_ref[w:w + 1, ls]
            zc_scr[pl.ds(r0, CONV_ROWS), ls] = acc
        return carry

    lax.fori_loop(0, tm // CONV_ROWS, conv_rows, 0)
    zc = _ln_rows(zc_scr[...], row(VEC_CLN_G), row(VEC_CLN_B))
    zc = zc * _sigmoid(zc)
    y_b = jnp.dot(zc.astype(BF16), wconv_ref[...], preferred_element_type=F32) + row(VEC_BCONV)

    mixed = ga_ref[...].astype(F32) * y_a + gb_ref[...].astype(F32) * y_b
    mix = jnp.dot(mixed.astype(BF16), wmix_ref[...], preferred_element_type=F32) + row(VEC_BMIX)
    h1 = _ln_rows(ALPHA * x_ref[...] + gt_ref[0] * mix, row(VEC_LN1_G), row(VEC_LN1_B))
    h1_ref[...] = h1
    t = h1 * (1.0 + scf_ref[0]) + shf_ref[0]
    t_ref[...] = t

    t_hi, t_lo = _split_bf16(t)
    logits = (jnp.dot(t_hi, wrh_ref[...], preferred_element_type=F32)
              + jnp.dot(t_lo, wrh_ref[...], preferred_element_type=F32)
              + jnp.dot(t_hi, wrl_ref[...], preferred_element_type=F32)) + br_ref[...]

    lane = lax.broadcasted_iota(I32, logits.shape, 1).astype(F32)
    neg = -jnp.inf
    big = float(ROUTER_LANES)
    is_grp = lane < N_GROUPS
    gl = jnp.where(is_grp, logits, neg)
    gmax = jnp.max(gl, axis=-1, keepdims=True)
    gidx = jnp.min(jnp.where(gl == gmax, lane, big), axis=-1, keepdims=True)
    gsum = jnp.sum(jnp.where(is_grp, jnp.exp(gl - gmax), 0.0), axis=-1, keepdims=True)
    grp_w = 1.0 / gsum
    lo = N_GROUPS + gidx * EXPERTS_PER_GROUP
    el = jnp.where(lane >= lo, jnp.where(lane < lo + EXPERTS_PER_GROUP, logits, neg), neg)
    m1 = jnp.max(el, axis=-1, keepdims=True)
    i1 = jnp.min(jnp.where(el == m1, lane, big), axis=-1, keepdims=True)
    el2 = jnp.where(lane == i1, neg, el)
    m2 = jnp.max(el2, axis=-1, keepdims=True)
    i2 = jnp.min(jnp.where(el2 == m2, lane, big), axis=-1, keepdims=True)
    r = jnp.exp(m2 - m1)
    w1 = grp_w / (1.0 + r)
    w2 = grp_w * r / (1.0 + r)

    @pl.when(step == 0)
    def _():
        cnt_scr[...] = jnp.zeros_like(cnt_scr)

    oh1 = lane == i1
    oh2 = lane == i2
    oh = jnp.where(oh1, 1.0, jnp.where(oh2, 1.0, 0.0))
    tri = jnp.where(lax.broadcasted_iota(I32, (tm, tm), 0) > lax.broadcasted_iota(I32, (tm, tm), 1), 1.0, 0.0)
    before = jnp.dot(tri.astype(BF16), oh.astype(BF16), preferred_element_type=F32) + cnt_scr[0:1, :]
    rank1 = jnp.sum(jnp.where(oh1, before, 0.0), axis=-1, keepdims=True)
    rank2 = jnp.sum(jnp.where(oh2, before, 0.0), axis=-1, keepdims=True)
    cnt = cnt_scr[0:1, :] + jnp.sum(oh, axis=0, keepdims=True)
    cnt_scr[...] = jnp.broadcast_to(cnt, cnt_scr.shape)
    cnt_ref[...] = jnp.broadcast_to(cnt, cnt_ref.shape)

    e1 = i1 - N_GROUPS
    e2 = i2 - N_GROUPS
    ri = jnp.where(lane == RT_E1, e1, jnp.where(lane == RT_E2, e2,
                   jnp.where(lane == RT_RANK1, rank1, jnp.where(lane == RT_RANK2, rank2, 0.0))))
    ri_ref[...] = ri.astype(I32)
    rw_ref[...] = jnp.where(lane == RT_W1, w1, jnp.where(lane == RT_W2, w2, 0.0))


def _merge(ret2, proj2, x2, mod3, wret, wconv, wmix, dw, vec, wrh, wrl, br, seq, tm):
    n, d = x2.shape
    tiles_per_seq = seq // tm
    hb = tm // CONV_HALO
    last_hb = n // CONV_HALO - 1
    const = lambda i: (0, 0)
    bat = lambda k: (lambda i: (i // tiles_per_seq, 0, k))
    return pl.pallas_call(
        functools.partial(_merge_kernel, tiles_per_seq=tiles_per_seq),
        grid=(n // tm,),
        in_specs=[pl.BlockSpec((tm, ret2.shape[1]), lambda i: (i, 0)),
                  pl.BlockSpec((tm, d), lambda i: (i, PB_Z)),
                  pl.BlockSpec((CONV_HALO, d), lambda i: (jnp.maximum(i * hb - 1, 0), PB_Z)),
                  pl.BlockSpec((CONV_HALO, d), lambda i: (jnp.minimum((i + 1) * hb, last_hb), PB_Z)),
                  pl.BlockSpec((tm, d), lambda i: (i, PB_GATE)),
                  pl.BlockSpec((tm, d), lambda i: (i, PB_GATE + 1)),
                  pl.BlockSpec((tm, d), lambda i: (i, 0)),
                  pl.BlockSpec((1, 1, d), bat(2)),
                  pl.BlockSpec((1, 1, d), bat(3)),
                  pl.BlockSpec((1, 1, d), bat(4)),
                  pl.BlockSpec(wret.shape, const),
                  pl.BlockSpec(wconv.shape, const),
                  pl.BlockSpec(wmix.shape, const),
                  pl.BlockSpec(dw.shape, const),
                  pl.BlockSpec(vec.shape, const),
                  pl.BlockSpec(wrh.shape, const),
                  pl.BlockSpec(wrl.shape, const),
                  pl.BlockSpec(br.shape, const)],
        out_specs=[pl.BlockSpec((tm, d), lambda i: (i, 0)),
                   pl.BlockSpec((tm, d), lambda i: (i, 0)),
                   pl.BlockSpec((tm, ROUTER_LANES), lambda i: (i, 0)),
                   pl.BlockSpec((tm, ROUTER_LANES), lambda i: (i, 0)),
                   pl.BlockSpec((SUBLANES, ROUTER_LANES), const)],
        out_shape=[jax.ShapeDtypeStruct((n, d), F32),
                   jax.ShapeDtypeStruct((n, d), F32),
                   jax.ShapeDtypeStruct((n, ROUTER_LANES), I32),
                   jax.ShapeDtypeStruct((n, ROUTER_LANES), F32),
                   jax.ShapeDtypeStruct((SUBLANES, ROUTER_LANES), F32)],
        scratch_shapes=[pltpu.VMEM((tm + 2 * CONV_HALO, d), F32),
                        pltpu.VMEM((tm, d), F32),
                        pltpu.VMEM((SUBLANES, ROUTER_LANES), F32)],
        compiler_params=_params("arbitrary"),
        name="merge",
    )(ret2, proj2, proj2, proj2, proj2, proj2, x2, mod3, mod3, mod3,
      wret, wconv, wmix, dw, vec, wrh, wrl, br)


DISPATCH_CHUNK = 256


def _dispatch_kernel(pos_ref, t_hbm, xs_hbm, sem):
    n = t_hbm.shape[0]
    n_chunks = n // DISPATCH_CHUNK

    def chunk_copy():
        rows = DISPATCH_CHUNK * EXPERT_TOP_K
        return pltpu.make_async_copy(t_hbm.at[pl.ds(0, rows)], xs_hbm.at[pl.ds(0, rows)], sem)

    def chunk(ci, carry):
        def issue(r, carry):
            tok = ci * DISPATCH_CHUNK + r
            for k in range(EXPERT_TOP_K):
                dst = pos_ref[tok * EXPERT_TOP_K + k]
                pltpu.make_async_copy(t_hbm.at[pl.ds(tok, 1)], xs_hbm.at[pl.ds(dst, 1)], sem).start()
            return carry

        lax.fori_loop(0, DISPATCH_CHUNK, issue, 0, unroll=8)

        @pl.when(ci > 0)
        def _():
            chunk_copy().wait()

        return carry

    lax.fori_loop(0, n_chunks, chunk, 0)
    chunk_copy().wait()


def _dispatch(pos_flat, t2):
    n, d = t2.shape
    return pl.pallas_call(
        _dispatch_kernel,
        grid=(1,),
        in_specs=[pl.BlockSpec(memory_space=pltpu.SMEM),
                  pl.BlockSpec(memory_space=pl.ANY)],
        out_specs=pl.BlockSpec(memory_space=pl.ANY),
        out_shape=jax.ShapeDtypeStruct((n * EXPERT_TOP_K, d), F32),
        scratch_shapes=[pltpu.SemaphoreType.DMA],
        compiler_params=_params("arbitrary"),
        name="dispatch",
    )(pos_flat, t2)


EXPERT_TILE = 256


def _experts_kernel(ie_ref, it_ref, lo_ref, hi_ref, nv_ref, xs_ref, wg_ref, wu_ref, wd_ref, ys_ref,
                    wg_bf, wu_bf, wd_bf):
    k = pl.program_id(0)
    prev = jnp.maximum(k - 1, 0)
    new_expert = (k == 0) | (ie_ref[k] != ie_ref[prev])
    new_tile = (k == 0) | (it_ref[k] != it_ref[prev])

    @pl.when(new_expert)
    def _():
        wg_bf[...] = wg_ref[0, 0].astype(BF16)
        wu_bf[...] = wu_ref[0, 0].astype(BF16)
        wd_bf[...] = wd_ref[0, 0].astype(BF16)

    @pl.when(k < nv_ref[0])
    def _():
        x = xs_ref[...].astype(BF16)
        g = jnp.dot(x, wg_bf[...], preferred_element_type=F32)
        u = jnp.dot(x, wu_bf[...], preferred_element_type=F32)
        hid = (g * _sigmoid(g) * u).astype(BF16)
        y = jnp.dot(hid, wd_bf[...], preferred_element_type=F32)
        rows = lax.broadcasted_iota(I32, (xs_ref.shape[0], 1), 0)
        mine = (rows >= lo_ref[k]) & (rows < hi_ref[k])

        @pl.when(new_tile)
        def _():
            ys_ref[...] = jnp.where(mine, y, 0.0)

        @pl.when(jnp.logical_not(new_tile))
        def _():
            ys_ref[...] = jnp.where(mine, y, ys_ref[...])


def _experts(meta, xs, w_gate, w_up, w_down):
    p, d = xs.shape
    ff = w_gate.shape[-1]
    n_items = meta[0].shape[0]
    grid_spec = pltpu.PrefetchScalarGridSpec(
        num_scalar_prefetch=5,
        grid=(n_items,),
        in_specs=[pl.BlockSpec((EXPERT_TILE, d), lambda k, ie, it, lo, hi, nv: (it[k], 0)),
                  pl.BlockSpec((1, 1, d, ff), lambda k, ie, it, lo, hi, nv: (0, ie[k], 0, 0)),
                  pl.BlockSpec((1, 1, d, ff), lambda k, ie, it, lo, hi, nv: (0, ie[k], 0, 0)),
                  pl.BlockSpec((1, 1, ff, d), lambda k, ie, it, lo, hi, nv: (0, ie[k], 0, 0))],
        out_specs=pl.BlockSpec((EXPERT_TILE, d), lambda k, ie, it, lo, hi, nv: (it[k], 0)),
        scratch_shapes=[pltpu.VMEM((d, ff), BF16), pltpu.VMEM((d, ff), BF16), pltpu.VMEM((ff, d), BF16)])
    return pl.pallas_call(
        _experts_kernel,
        grid_spec=grid_spec,
        out_shape=jax.ShapeDtypeStruct((p, d), F32),
        compiler_params=_params("arbitrary"),
        name="experts",
    )(*meta, xs, w_gate, w_up, w_down)


def _expert_work_items(counts, p):
    n_tiles = p // EXPERT_TILE
    n_items = n_tiles + N_EXPERTS - 1
    offs = jnp.concatenate([jnp.zeros((1,), I32), jnp.cumsum(counts)])
    first = offs[:-1] // EXPERT_TILE
    last = (offs[1:] - 1) // EXPERT_TILE
    per = jnp.where(counts > 0, last - first + 1, 0)
    ends = jnp.cumsum(per)
    starts = ends - per
    n_valid = ends[-1]
    k = jnp.arange(n_items, dtype=I32)
    kk = jnp.minimum(k, n_valid - 1)
    ie = jnp.sum((ends[None, :] <= kk[:, None]).astype(I32), axis=1)
    it = first[ie] + (kk - starts[ie])
    lo = jnp.maximum(offs[ie], it * EXPERT_TILE) - it * EXPERT_TILE
    hi = jnp.minimum(offs[ie + 1], (it + 1) * EXPERT_TILE) - it * EXPERT_TILE
    valid = k < n_valid
    lo = jnp.where(valid, lo, 0)
    hi = jnp.where(valid, hi, 0)
    return offs, (ie.astype(I32), it.astype(I32), lo.astype(I32), hi.astype(I32), n_valid.reshape(1).astype(I32))


COMBINE_TILE = 256


def _combine_kernel(pos_ref, ys_hbm, rw_ref, h1_ref, gt_ref, ln_ref, out_ref, buf, sem):
    i = pl.program_id(0)
    n_steps = pl.num_programs(0)
    tm = out_ref.shape[0]
    slot = i % 2

    def slot_copy(s, k):
        return pltpu.make_async_copy(ys_hbm.at[pl.ds(0, tm)], buf.at[s, k], sem.at[s])

    def gather(tile, s):
        def issue(r, carry):
            tok = tile * tm + r
            for k in range(EXPERT_TOP_K):
                src = pos_ref[tok * EXPERT_TOP_K + k]
                pltpu.make_async_copy(ys_hbm.at[pl.ds(src, 1)], buf.at[s, k, pl.ds(r, 1)], sem.at[s]).start()
            return carry

        lax.fori_loop(0, tm, issue, 0, unroll=8)

    @pl.when(i == 0)
    def _():
        gather(0, 0)

    @pl.when(i + 1 < n_steps)
    def _():
        gather(i + 1, 1 - slot)

    for k in range(EXPERT_TOP_K):
        slot_copy(slot, k).wait()

    rw = rw_ref[...]
    y = rw[:, RT_W1:RT_W1 + 1] * buf[slot, 0] + rw[:, RT_W2:RT_W2 + 1] * buf[slot, 1]
    out_ref[...] = _ln_rows(ALPHA * h1_ref[...] + gt_ref[0] * y, ln_ref[0:1, :], ln_ref[1:2, :])


def _combine(pos_flat, ys, rw, h1, mod3, ln2, seq):
    n, d = h1.shape
    tm = COMBINE_TILE
    tiles_per_seq = seq // tm
    return pl.pallas_call(
        _combine_kernel,
        grid=(n // tm,),
        in_specs=[pl.BlockSpec(memory_space=pltpu.SMEM),
                  pl.BlockSpec(memory_space=pl.ANY),
                  pl.BlockSpec((tm, ROUTER_LANES), lambda i: (i, 0)),
                  pl.BlockSpec((tm, d), lambda i: (i, 0)),
                  pl.BlockSpec((1, 1, d), lambda i: (i // tiles_per_seq, 0, 5)),
                  pl.BlockSpec(ln2.shape, lambda i: (0, 0))],
        out_specs=pl.BlockSpec((tm, d), lambda i: (i, 0)),
        out_shape=jax.ShapeDtypeStruct((n, d), F32),
        scratch_shapes=[pltpu.VMEM((2, EXPERT_TOP_K, tm, d), F32),
                        pltpu.SemaphoreType.DMA((2,))],
        compiler_params=_params("arbitrary"),
        name="combine",
    )(pos_flat, ys, rw, h1, mod3, ln2)


def _rope_tables(seq):
    pos = jnp.arange(seq)
    r = (pos // GRID_W).astype(F32)
    col = (pos % GRID_W).astype(F32)
    n_freq = RET_QK_DIM // 4
    inv = ROPE_BASE ** (-jnp.arange(n_freq, dtype=F32) / n_freq)
    ang = jnp.concatenate([r[:, None] * inv, col[:, None] * inv], axis=-1)
    ang = jnp.concatenate([ang, ang], axis=-1)
    sign = jnp.concatenate([-jnp.ones((RET_QK_DIM // 2,), F32), jnp.ones((RET_QK_DIM // 2,), F32)])
    return jnp.cos(ang), jnp.sin(ang) * sign


def kernel(x, c, ctx, c_ctx, w_ada, b_ada, w_in, b_in, ret_decay_fwd, ret_decay_bwd, w_ret_out, conv_dw, conv_dw_b, conv_ln_g, conv_ln_b, w_conv_out, b_conv_out, w_mix_out, b_mix_out, ln1_g, ln1_b, w_router_grp, b_router_grp, w_router_exp, b_router_exp, w_exp_gate, w_exp_up, w_exp_down, ln2_g, ln2_b):
    b_, seq, d = x.shape
    n = b_ * seq
    assert w_ada.shape[0] == DEPTH
    mod_rows = SUBLANES
    assert b_ + 1 <= mod_rows

    cs = jnp.concatenate([c, c_ctx[None, :], jnp.zeros((mod_rows - b_ - 1, d), F32)], axis=0)
    mod = _ada(cs, w_ada[0], b_ada[0][None, :])
    mod3 = mod.reshape(mod_rows, 1, 6 * d)

    w_in_bf = w_in[0].astype(BF16)
    b_in2 = b_in[0][None, :]
    dec_f = ret_decay_fwd[0].reshape(RET_HEADS, 1, 1)
    dec_b = ret_decay_bwd[0].reshape(RET_HEADS, 1, 1)

    s0f, s0b = _ctx_states(ctx, mod3, w_in_bf, b_in2, dec_f, dec_b, b_)

    cos, sin = _rope_tables(seq)
    x2 = x.reshape(n, d)
    proj = _proj(x2, mod3, w_in_bf, b_in2, cos, sin, seq, tm=min(1024, seq))

    ret = _retention(proj.reshape(b_, seq, PROJ_BLOCKS * d), s0f, s0b, dec_f, dec_b, d)

    vec = jnp.concatenate([conv_dw_b, conv_ln_g, conv_ln_b, b_conv_out, b_mix_out, ln1_g, ln1_b,
                           jnp.zeros((1, d), F32)], axis=0)
    dw = jnp.concatenate([conv_dw[0], jnp.zeros((32 - CONV_WIDTH, d), F32)], axis=0)
    pad = ROUTER_LANES - N_GROUPS - N_EXPERTS
    wr = jnp.concatenate([w_router_grp[0], w_router_exp[0], jnp.zeros((d, pad), F32)], axis=1)
    wr_hi = wr.astype(BF16)
    wr_lo = (wr - wr_hi.astype(F32)).astype(BF16)
    br = jnp.concatenate([b_router_grp[0], b_router_exp[0], jnp.zeros((pad,), F32)])[None, :]
    h1, t, route_i, route_w, cnt = _merge(
        ret.reshape(n, RET_HEADS * RET_V_DIM), proj, x2, mod3,
        w_ret_out[0].astype(BF16), w_conv_out[0].astype(BF16), w_mix_out[0].astype(BF16),
        dw, vec, wr_hi, wr_lo, br, seq, tm=min(512, seq))

    counts = cnt[0, N_GROUPS:N_GROUPS + N_EXPERTS].astype(I32)
    offs, meta = _expert_work_items(counts, n * EXPERT_TOP_K)
    pos = offs[route_i[:, RT_E1:RT_E2 + 1]] + route_i[:, RT_RANK1:RT_RANK2 + 1]
    pos_flat = pos.reshape(n * EXPERT_TOP_K)

    xs = _dispatch(pos_flat, t)
    ys = _experts(meta, xs, w_exp_gate, w_exp_up, w_exp_down)
    ln2 = jnp.concatenate([ln2_g, ln2_b, jnp.zeros((SUBLANES - 2, d), F32)], axis=0)
    out = _combine(pos_flat, ys, route_w, h1, mod3, ln2, seq)
    return out.reshape(b_, seq, d)
```

```python
import functools

import jax
import jax.numpy as jnp
from jax import lax
from jax.experimental import pallas as pl
from jax.experimental.pallas import tpu as pltpu

GRID_W = 64
RET_HEADS = 8
RET_QK_DIM = 128
RET_V_DIM = 256
RET_CHUNK = 128
ROPE_BASE = 10000.0
CONV_WIDTH = 31
CONV_HALO = 16
SUBLANES = 8
N_GROUPS = 4
EXPERTS_PER_GROUP = 8
N_EXPERTS = N_GROUPS * EXPERTS_PER_GROUP
EXPERT_TOP_K = 2
LN_EPS = 1e-5
DEPTH = 1
ALPHA = (2.0 * DEPTH) ** 0.25
ROUTER_LANES = 128

V7X_VMEM_LIMIT = 56 * 1024 * 1024

F32 = jnp.float32
BF16 = jnp.bfloat16
I32 = jnp.int32
HIGHEST = lax.Precision.HIGHEST


def _params(*sem):
    return pltpu.CompilerParams(dimension_semantics=sem, vmem_limit_bytes=V7X_VMEM_LIMIT)


def _sigmoid(v):
    return 1.0 / (1.0 + jnp.exp(-v))


def _ln_rows(v, g, b):
    mu = jnp.mean(v, axis=-1, keepdims=True)
    d = v - mu
    var = jnp.mean(d * d, axis=-1, keepdims=True)
    return d * lax.rsqrt(var + LN_EPS) * g + b


def _ada_kernel(cs_ref, w_ref, b_ref, out_ref):
    s = cs_ref[...]
    s = s * _sigmoid(s)
    out_ref[...] = jnp.dot(s, w_ref[...], preferred_element_type=F32, precision=HIGHEST) + b_ref[...]


def _ada(cs, w_ada, b_ada):
    rows, d = cs.shape
    cols = w_ada.shape[1]
    tn = 1024
    return pl.pallas_call(
        _ada_kernel,
        grid=(cols // tn,),
        in_specs=[pl.BlockSpec((rows, d), lambda j: (0, 0)),
                  pl.BlockSpec((d, tn), lambda j: (0, j)),
                  pl.BlockSpec((1, tn), lambda j: (0, j))],
        out_specs=pl.BlockSpec((rows, tn), lambda j: (0, j)),
        out_shape=jax.ShapeDtypeStruct((rows, cols), F32),
        compiler_params=_params("arbitrary"),
        name="ada",
    )(cs, w_ada, b_ada)


def _log_sigmoid(v):
    return jnp.minimum(v, 0.0) - jnp.log(1.0 + jnp.exp(-jnp.abs(v)))


def _ctx_kernel(ctx_ref, sh_ref, sc_ref, wk_ref, wv_ref, bk_ref, bv_ref, df_ref, db_ref, sf_ref, sb_ref):
    lc = ctx_ref.shape[1]
    u = (ctx_ref[0] * (1.0 + sc_ref[0]) + sh_ref[0]).astype(BF16)
    k = (jnp.dot(u, wk_ref[...], preferred_element_type=F32) + bk_ref[...]) * RET_QK_DIM ** -0.5
    v = (jnp.dot(u, wv_ref[...], preferred_element_type=F32) + bv_ref[...]).astype(BF16)
    lgf = _log_sigmoid(df_ref[0])
    lgb = _log_sigmoid(db_ref[0])
    pos = lax.broadcasted_iota(I32, (lc, 1), 0).astype(F32)
    kf = (k * jnp.exp((lc - 1.0 - pos) * lgf)).astype(BF16)
    kb = (k * jnp.exp(pos * lgb)).astype(BF16)
    dn = (((0,), (0,)), ((), ()))
    sf_ref[0, 0] = lax.dot_general(kf, v, dn, preferred_element_type=F32)
    sb_ref[0, 0] = lax.dot_general(kb, v, dn, preferred_element_type=F32)


def _ctx_states(ctx, mod3, w_in_bf, b_in2, dec_f, dec_b, ctx_row):
    b_, lc, d = ctx.shape
    k_blk = d // RET_QK_DIM
    v_blk = (2 * d) // RET_V_DIM
    st = jax.ShapeDtypeStruct((b_, RET_HEADS, RET_QK_DIM, RET_V_DIM), F32)
    st_spec = pl.BlockSpec((1, 1, RET_QK_DIM, RET_V_DIM), lambda b, h: (b, h, 0, 0))
    return pl.pallas_call(
        _ctx_kernel,
        grid=(b_, RET_HEADS),
        in_specs=[pl.BlockSpec((1, lc, d), lambda b, h: (b, 0, 0)),
                  pl.BlockSpec((1, 1, d), lambda b, h: (ctx_row, 0, 0)),
                  pl.BlockSpec((1, 1, d), lambda b, h: (ctx_row, 0, 1)),
                  pl.BlockSpec((d, RET_QK_DIM), lambda b, h: (0, k_blk + h)),
                  pl.BlockSpec((d, RET_V_DIM), lambda b, h: (0, v_blk + h)),
                  pl.BlockSpec((1, RET_QK_DIM), lambda b, h: (0, k_blk + h)),
                  pl.BlockSpec((1, RET_V_DIM), lambda b, h: (0, v_blk + h)),
                  pl.BlockSpec((1, 1, 1), lambda b, h: (h, 0, 0)),
                  pl.BlockSpec((1, 1, 1), lambda b, h: (h, 0, 0))],
        out_specs=[st_spec, st_spec],
        out_shape=[st, st],
        compiler_params=_params("arbitrary", "arbitrary"),
        name="ctx",
    )(ctx, mod3, mod3, w_in_bf, w_in_bf, b_in2, b_in2, dec_f, dec_b)


PROJ_BLOCKS = 9
PB_Q, PB_K, PB_V, PB_G, PB_Z, PB_GATE = 0, 1, 2, 4, 6, 7


def _proj_kernel(x_ref, sh_ref, sc_ref, w_ref, w2_ref, b_ref, b2_ref, cos_ref, sin_ref, out_ref, u_scr):
    j = pl.program_id(1)

    @pl.when(j == 0)
    def _():
        u_scr[...] = (x_ref[...] * (1.0 + sc_ref[0]) + sh_ref[0]).astype(BF16)

    def mm(w, b):
        return jnp.dot(u_scr[...], w[...], preferred_element_type=F32) + b[...]

    def rope(acc):
        cos = cos_ref[...]
        sin = sin_ref[...]
        for h in range(acc.shape[1] // RET_QK_DIM):
            sl = slice(h * RET_QK_DIM, (h + 1) * RET_QK_DIM)
            seg = acc[:, sl]
            out_ref[:, sl] = (seg * cos + pltpu.roll(seg, RET_QK_DIM // 2, 1) * sin).astype(BF16)

    @pl.when(j == PB_Q)
    def _():
        rope(mm(w_ref, b_ref))

    @pl.when(j == PB_K)
    def _():
        rope(mm(w_ref, b_ref) * RET_QK_DIM ** -0.5)

    @pl.when((j >= PB_V) & (j < PB_G))
    def _():
        out_ref[...] = mm(w_ref, b_ref).astype(BF16)

    @pl.when((j >= PB_G) & (j < PB_Z))
    def _():
        acc = mm(w_ref, b_ref)
        out_ref[...] = (acc * _sigmoid(acc)).astype(BF16)

    @pl.when(j == PB_Z)
    def _():
        out_ref[...] = (mm(w_ref, b_ref) * _sigmoid(mm(w2_ref, b2_ref))).astype(BF16)

    @pl.when(j >= PB_GATE)
    def _():
        out_ref[...] = _sigmoid(mm(w_ref, b_ref)).astype(BF16)


def _proj(x2, mod3, w_in_bf, b_in2, cos, sin, seq, tm):
    n, d = x2.shape
    tiles_per_seq = seq // tm

    def wcol(i, j):
        return (0, j + (j > PB_Z).astype(I32))

    return pl.pallas_call(
        _proj_kernel,
        grid=(n // tm, PROJ_BLOCKS),
        in_specs=[pl.BlockSpec((tm, d), lambda i, j: (i, 0)),
                  pl.BlockSpec((1, 1, d), lambda i, j: (i // tiles_per_seq, 0, 0)),
                  pl.BlockSpec((1, 1, d), lambda i, j: (i // tiles_per_seq, 0, 1)),
                  pl.BlockSpec((d, d), wcol),
                  pl.BlockSpec((d, d), lambda i, j: (0, PB_Z + 1)),
                  pl.BlockSpec((1, d), wcol),
                  pl.BlockSpec((1, d), lambda i, j: (0, PB_Z + 1)),
                  pl.BlockSpec((tm, RET_QK_DIM), lambda i, j: (i % tiles_per_seq, 0)),
                  pl.BlockSpec((tm, RET_QK_DIM), lambda i, j: (i % tiles_per_seq, 0))],
        out_specs=pl.BlockSpec((tm, d), lambda i, j: (i, j)),
        out_shape=jax.ShapeDtypeStruct((n, PROJ_BLOCKS * d), BF16),
        scratch_shapes=[pltpu.VMEM((tm, d), BF16)],
        compiler_params=_params("arbitrary", "arbitrary"),
        name="proj",
    )(x2, mod3, mod3, w_in_bf, w_in_bf, b_in2, b_in2, cos, sin)


RET_UNROLL = 4


def _ret_kernel(q_ref, k_ref, v_ref, g_ref, s0f_ref, s0b_ref, df_ref, db_ref, out_ref,
                sf_all, sb_all, p_scr, o_scr):
    c_ = RET_CHUNK
    n_chunks = q_ref.shape[1] // c_
    dn_t = (((0,), (0,)), ((), ()))
    lgf = _log_sigmoid(df_ref[0])
    lgb = _log_sigmoid(db_ref[0])
    ri = lax.broadcasted_iota(I32, (c_, c_), 0)
    ci = lax.broadcasted_iota(I32, (c_, c_), 1)
    diff = (ri - ci).astype(F32)
    mask = jnp.where(diff > 0, jnp.exp(diff * lgf), jnp.where(diff < 0, jnp.exp(-diff * lgb), 2.0))
    pos = lax.broadcasted_iota(I32, (c_, 1), 0).astype(F32)
    qdec_f = jnp.exp((pos + 1.0) * lgf)
    qdec_b = jnp.exp((c_ - pos) * lgb)
    kdec_f = jnp.exp((c_ - 1.0 - pos) * lgf)
    kdec_b = jnp.exp(pos * lgb)
    cdec_f = jnp.exp(c_ * lgf)
    cdec_b = jnp.exp(c_ * lgb)

    def chunk(c):
        return pl.ds(pl.multiple_of(c * c_, c_), c_)

    def products(c, carry):
        k = k_ref[0, chunk(c), :].astype(F32)
        v = v_ref[0, chunk(c), :]
        sf_all[c] = lax.dot_general((k * kdec_f).astype(BF16), v, dn_t, preferred_element_type=F32)
        sb_all[c] = lax.dot_general((k * kdec_b).astype(BF16), v, dn_t, preferred_element_type=F32)
        return carry

    lax.fori_loop(0, n_chunks, products, 0, unroll=RET_UNROLL)

    def scan_f(c, s):
        kv = sf_all[c]
        sf_all[c] = s
        return s * cdec_f + kv

    def scan_b(t, s):
        c = n_chunks - 1 - t
        kv = sb_all[c]
        sb_all[c] = s
        return s * cdec_b + kv

    lax.fori_loop(0, n_chunks, scan_f, s0f_ref[0, 0])
    lax.fori_loop(0, n_chunks, scan_b, s0b_ref[0, 0])

    def scores(c, carry):
        s = lax.dot_general(q_ref[0, chunk(c), :], k_ref[0, chunk(c), :], (((1,), (1,)), ((), ())),
                            preferred_element_type=F32)
        p_scr[c] = (s * mask).astype(BF16)
        return carry

    lax.fori_loop(0, n_chunks, scores, 0, unroll=RET_UNROLL)

    def outputs(c, carry):
        q = q_ref[0, chunk(c), :]
        o = jnp.dot(p_scr[c], v_ref[0, chunk(c), :], preferred_element_type=F32)
        o = o + qdec_f * jnp.dot(q, sf_all[c].astype(BF16), preferred_element_type=F32)
        o = o + qdec_b * jnp.dot(q, sb_all[c].astype(BF16), preferred_element_type=F32)
        o_scr[chunk(c), :] = o
        return carry

    lax.fori_loop(0, n_chunks, outputs, 0, unroll=RET_UNROLL)

    def norm(c, carry):
        o = o_scr[chunk(c), :]
        mu = jnp.mean(o, axis=-1, keepdims=True)
        d = o - mu
        var = jnp.mean(d * d, axis=-1, keepdims=True)
        on = d * lax.rsqrt(var + LN_EPS)
        out_ref[0, chunk(c), :] = (on * g_ref[0, chunk(c), :].astype(F32)).astype(BF16)
        return carry

    lax.fori_loop(0, n_chunks, norm, 0, unroll=RET_UNROLL)


def _retention(proj3, s0f, s0b, dec_f, dec_b, d):
    b_, seq, _ = proj3.shape
    qb, kb = PB_Q * d // RET_QK_DIM, PB_K * d // RET_QK_DIM
    vb, gb = PB_V * d // RET_V_DIM, PB_G * d // RET_V_DIM
    n_chunks = seq // RET_CHUNK
    st_spec = pl.BlockSpec((1, 1, RET_QK_DIM, RET_V_DIM), lambda b, h: (b, h, 0, 0))
    st_all = pltpu.VMEM((n_chunks, RET_QK_DIM, RET_V_DIM), F32)
    return pl.pallas_call(
        _ret_kernel,
        grid=(b_, RET_HEADS),
        in_specs=[pl.BlockSpec((1, seq, RET_QK_DIM), lambda b, h: (b, 0, qb + h)),
                  pl.BlockSpec((1, seq, RET_QK_DIM), lambda b, h: (b, 0, kb + h)),
                  pl.BlockSpec((1, seq, RET_V_DIM), lambda b, h: (b, 0, vb + h)),
                  pl.BlockSpec((1, seq, RET_V_DIM), lambda b, h: (b, 0, gb + h)),
                  st_spec, st_spec,
                  pl.BlockSpec((1, 1, 1), lambda b, h: (h, 0, 0)),
                  pl.BlockSpec((1, 1, 1), lambda b, h: (h, 0, 0))],
        out_specs=pl.BlockSpec((1, seq, RET_V_DIM), lambda b, h: (b, 0, h)),
        out_shape=jax.ShapeDtypeStruct((b_, seq, RET_HEADS * RET_V_DIM), BF16),
        scratch_shapes=[st_all, st_all,
                        pltpu.VMEM((n_chunks, RET_CHUNK, RET_CHUNK), BF16),
                        pltpu.VMEM((seq, RET_V_DIM), F32)],
        compiler_params=_params("arbitrary", "arbitrary"),
        name="ret",
    )(proj3, proj3, proj3, proj3, s0f, s0b, dec_f, dec_b)


VEC_CONV_B, VEC_CLN_G, VEC_CLN_B, VEC_BCONV, VEC_BMIX, VEC_LN1_G, VEC_LN1_B = range(7)
CONV_ROWS = 64
LANES = 128
RT_E1, RT_E2, RT_RANK1, RT_RANK2 = 0, 1, 2, 3
RT_W1, RT_W2 = 0, 1


def _split_bf16(v):
    hi = v.astype(BF16)
    return hi, (v - hi.astype(F32)).astype(BF16)


def _merge_kernel(ret_ref, z_ref, zp_ref, zn_ref, ga_ref, gb_ref, x_ref, gt_ref, shf_ref, scf_ref,
                  wret_ref, wconv_ref, wmix_ref, dw_ref, vec_ref, wrh_ref, wrl_ref, br_ref,
                  h1_ref, t_ref, ri_ref, rw_ref, cnt_ref, zext, zc_scr, cnt_scr, *, tiles_per_seq):
    tm, d = x_ref.shape
    step = pl.program_id(0)
    ti = step % tiles_per_seq
    vec = vec_ref[...]

    def row(r):
        return vec[r:r + 1, :]

    y_a = jnp.dot(ret_ref[...], wret_ref[...], preferred_element_type=F32)

    has_prev = (ti > 0).astype(F32)
    has_next = (ti < tiles_per_seq - 1).astype(F32)
    zext[0:CONV_HALO, :] = zp_ref[...].astype(F32) * has_prev
    zext[CONV_HALO:CONV_HALO + tm, :] = z_ref[...].astype(F32)
    zext[CONV_HALO + tm:, :] = zn_ref[...].astype(F32) * has_next
    base = CONV_HALO - CONV_WIDTH // 2
    win_rows = CONV_ROWS + 2 * CONV_HALO

    def conv_rows(r, carry):
        r0 = pl.multiple_of(r * CONV_ROWS, CONV_ROWS)
        for lt in range(d // LANES):
            ls = slice(lt * LANES, (lt + 1) * LANES)
            win = zext[pl.ds(r0, win_rows), ls]
            acc = jnp.zeros((CONV_ROWS, LANES), F32) + vec_ref[VEC_CONV_B:VEC_CONV_B + 1, ls]
            for s in range(SUBLANES):
                taps = [w for w in range(CONV_WIDTH) if (base + w) % SUBLANES == s]
                if not taps:
                    continue
                sh = win if s == 0 else pltpu.roll(win, win_rows - s, 0)
                for w in taps:
                    a = (base + w) - s
                    acc = acc + sh[a:a + CONV_ROWS, :] * dw_ref[w:w + 1, ls]
            zc_scr[pl.ds(r0, CONV_ROWS), ls] = acc
        return carry

    lax.fori_loop(0, tm // CONV_ROWS, conv_rows, 0)
    zc = _ln_rows(zc_scr[...], row(VEC_CLN_G), row(VEC_CLN_B))
    zc = zc * _sigmoid(zc)
    y_b = jnp.dot(zc.astype(BF16), wconv_ref[...], preferred_element_type=F32) + row(VEC_BCONV)

    mixed = ga_ref[...].astype(F32) * y_a + gb_ref[...].astype(F32) * y_b
    mix = jnp.dot(mixed.astype(BF16), wmix_ref[...], preferred_element_type=F32) + row(VEC_BMIX)
    h1 = _ln_rows(ALPHA * x_ref[...] + gt_ref[0] * mix, row(VEC_LN1_G), row(VEC_LN1_B))
    h1_ref[...] = h1
    t = h1 * (1.0 + scf_ref[0]) + shf_ref[0]
    t_ref[...] = t

    t_hi, t_lo = _split_bf16(t)
    logits = (jnp.dot(t_hi, wrh_ref[...], preferred_element_type=F32)
              + jnp.dot(t_lo, wrh_ref[...], preferred_element_type=F32)
              + jnp.dot(t_hi, wrl_ref[...], preferred_element_type=F32)) + br_ref[...]

    lane = lax.broadcasted_iota(I32, logits.shape, 1).astype(F32)
    neg = -jnp.inf
    big = float(ROUTER_LANES)
    is_grp = lane < N_GROUPS
    gl = jnp.where(is_grp, logits, neg)
    gmax = jnp.max(gl, axis=-1, keepdims=True)
    gidx = jnp.min(jnp.where(gl == gmax, lane, big), axis=-1, keepdims=True)
    gsum = jnp.sum(jnp.where(is_grp, jnp.exp(gl - gmax), 0.0), axis=-1, keepdims=True)
    grp_w = 1.0 / gsum
    lo = N_GROUPS + gidx * EXPERTS_PER_GROUP
    el = jnp.where(lane >= lo, jnp.where(lane < lo + EXPERTS_PER_GROUP, logits, neg), neg)
    m1 = jnp.max(el, axis=-1, keepdims=True)
    i1 = jnp.min(jnp.where(el == m1, lane, big), axis=-1, keepdims=True)
    el2 = jnp.where(lane == i1, neg, el)
    m2 = jnp.max(el2, axis=-1, keepdims=True)
    i2 = jnp.min(jnp.where(el2 == m2, lane, big), axis=-1, keepdims=True)
    r = jnp.exp(m2 - m1)
    w1 = grp_w / (1.0 + r)
    w2 = grp_w * r / (1.0 + r)

    @pl.when(step == 0)
    def _():
        cnt_scr[...] = jnp.zeros_like(cnt_scr)

    oh1 = lane == i1
    oh2 = lane == i2
    oh = jnp.where(oh1, 1.0, jnp.where(oh2, 1.0, 0.0))
    tri = jnp.where(lax.broadcasted_iota(I32, (tm, tm), 0) > lax.broadcasted_iota(I32, (tm, tm), 1), 1.0, 0.0)
    before = jnp.dot(tri.astype(BF16), oh.astype(BF16), preferred_element_type=F32) + cnt_scr[0:1, :]
    rank1 = jnp.sum(jnp.where(oh1, before, 0.0), axis=-1, keepdims=True)
    rank2 = jnp.sum(jnp.where(oh2, before, 0.0), axis=-1, keepdims=True)
    cnt = cnt_scr[0:1, :] + jnp.sum(oh, axis=0, keepdims=True)
    cnt_scr[...] = jnp.broadcast_to(cnt, cnt_scr.shape)
    cnt_ref[...] = jnp.broadcast_to(cnt, cnt_ref.shape)

    e1 = i1 - N_GROUPS
    e2 = i2 - N_GROUPS
    ri = jnp.where(lane == RT_E1, e1, jnp.where(lane == RT_E2, e2,
                   jnp.where(lane == RT_RANK1, rank1, jnp.where(lane == RT_RANK2, rank2, 0.0))))
    ri_ref[...] = ri.astype(I32)
    rw_ref[...] = jnp.where(lane == RT_W1, w1, jnp.where(lane == RT_W2, w2, 0.0))


def _merge(ret2, proj2, x2, mod3, wret, wconv, wmix, dw, vec, wrh, wrl, br, seq, tm):
    n, d = x2.shape
    tiles_per_seq = seq // tm
    hb = tm // CONV_HALO
    last_hb = n // CONV_HALO - 1
    const = lambda i: (0, 0)
    bat = lambda k: (lambda i: (i // tiles_per_seq, 0, k))
    return pl.pallas_call(
        functools.partial(_merge_kernel, tiles_per_seq=tiles_per_seq),
        grid=(n // tm,),
        in_specs=[pl.BlockSpec((tm, ret2.shape[1]), lambda i: (i, 0)),
                  pl.BlockSpec((tm, d), lambda i: (i, PB_Z)),
                  pl.BlockSpec((CONV_HALO, d), lambda i: (jnp.maximum(i * hb - 1, 0), PB_Z)),
                  pl.BlockSpec((CONV_HALO, d), lambda i: (jnp.minimum((i + 1) * hb, last_hb), PB_Z)),
                  pl.BlockSpec((tm, d), lambda i: (i, PB_GATE)),
                  pl.BlockSpec((tm, d), lambda i: (i, PB_GATE + 1)),
                  pl.BlockSpec((tm, d), lambda i: (i, 0)),
                  pl.BlockSpec((1, 1, d), bat(2)),
                  pl.BlockSpec((1, 1, d), bat(3)),
                  pl.BlockSpec((1, 1, d), bat(4)),
                  pl.BlockSpec(wret.shape, const),
                  pl.BlockSpec(wconv.shape, const),
                  pl.BlockSpec(wmix.shape, const),
                  pl.BlockSpec(dw.shape, const),
                  pl.BlockSpec(vec.shape, const),
                  pl.BlockSpec(wrh.shape, const),
                  pl.BlockSpec(wrl.shape, const),
                  pl.BlockSpec(br.shape, const)],
        out_specs=[pl.BlockSpec((tm, d), lambda i: (i, 0)),
                   pl.BlockSpec((tm, d), lambda i: (i, 0)),
                   pl.BlockSpec((tm, ROUTER_LANES), lambda i: (i, 0)),
                   pl.BlockSpec((tm, ROUTER_LANES), lambda i: (i, 0)),
                   pl.BlockSpec((SUBLANES, ROUTER_LANES), const)],
        out_shape=[jax.ShapeDtypeStruct((n, d), F32),
                   jax.ShapeDtypeStruct((n, d), F32),
                   jax.ShapeDtypeStruct((n, ROUTER_LANES), I32),
                   jax.ShapeDtypeStruct((n, ROUTER_LANES), F32),
                   jax.ShapeDtypeStruct((SUBLANES, ROUTER_LANES), F32)],
        scratch_shapes=[pltpu.VMEM((tm + 2 * CONV_HALO, d), F32),
                        pltpu.VMEM((tm, d), F32),
                        pltpu.VMEM((SUBLANES, ROUTER_LANES), F32)],
        compiler_params=_params("arbitrary"),
        name="merge",
    )(ret2, proj2, proj2, proj2, proj2, proj2, x2, mod3, mod3, mod3,
      wret, wconv, wmix, dw, vec, wrh, wrl, br)


DISPATCH_TILE = 1024


def _dispatch_kernel(pos_ref, t_ref, xs_hbm, sem):
    tm = t_ref.shape[0]
    base = pl.program_id(0) * tm

    def issue(r, carry):
        for k in range(EXPERT_TOP_K):
            dst = pos_ref[(base + r) * EXPERT_TOP_K + k]
            pltpu.make_async_copy(t_ref.at[pl.ds(r, 1)], xs_hbm.at[pl.ds(dst, 1)], sem).start()
        return carry

    lax.fori_loop(0, tm, issue, 0, unroll=8)
    for k in range(EXPERT_TOP_K):
        pltpu.make_async_copy(t_ref, xs_hbm.at[pl.ds(0, tm)], sem).wait()


def _dispatch(pos_flat, t2):
    n, d = t2.shape
    tm = min(DISPATCH_TILE, n)
    return pl.pallas_call(
        _dispatch_kernel,
        grid=(n // tm,),
        in_specs=[pl.BlockSpec(memory_space=pltpu.SMEM),
                  pl.BlockSpec((tm, d), lambda i: (i, 0))],
        out_specs=pl.BlockSpec(memory_space=pl.ANY),
        out_shape=jax.ShapeDtypeStruct((n * EXPERT_TOP_K, d), F32),
        scratch_shapes=[pltpu.SemaphoreType.DMA],
        compiler_params=_params("arbitrary"),
        name="dispatch",
    )(pos_flat, t2)


EXPERT_TILE = 256


def _experts_kernel(ie_ref, it_ref, lo_ref, hi_ref, nv_ref, xs_ref, wg_ref, wu_ref, wd_ref, ys_ref,
                    wg_bf, wu_bf, wd_bf):
    k = pl.program_id(0)
    prev = jnp.maximum(k - 1, 0)
    new_expert = (k == 0) | (ie_ref[k] != ie_ref[prev])
    new_tile = (k == 0) | (it_ref[k] != it_ref[prev])

    @pl.when(new_expert)
    def _():
        wg_bf[...] = wg_ref[0, 0].astype(BF16)
        wu_bf[...] = wu_ref[0, 0].astype(BF16)
        wd_bf[...] = wd_ref[0, 0].astype(BF16)

    @pl.when(k < nv_ref[0])
    def _():
        x = xs_ref[...].astype(BF16)
        g = jnp.dot(x, wg_bf[...], preferred_element_type=F32)
        u = jnp.dot(x, wu_bf[...], preferred_element_type=F32)
        hid = (g * _sigmoid(g) * u).astype(BF16)
        y = jnp.dot(hid, wd_bf[...], preferred_element_type=F32)
        rows = lax.broadcasted_iota(I32, (xs_ref.shape[0], 1), 0)
        mine = (rows >= lo_ref[k]) & (rows < hi_ref[k])

        @pl.when(new_tile)
        def _():
            ys_ref[...] = jnp.where(mine, y, 0.0)

        @pl.when(jnp.logical_not(new_tile))
        def _():
            ys_ref[...] = jnp.where(mine, y, ys_ref[...])


def _experts(meta, xs, w_gate, w_up, w_down):
    p, d = xs.shape
    ff = w_gate.shape[-1]
    n_items = meta[0].shape[0]
    grid_spec = pltpu.PrefetchScalarGridSpec(
        num_scalar_prefetch=5,
        grid=(n_items,),
        in_specs=[pl.BlockSpec((EXPERT_TILE, d), lambda k, ie, it, lo, hi, nv: (it[k], 0)),
                  pl.BlockSpec((1, 1, d, ff), lambda k, ie, it, lo, hi, nv: (0, ie[k], 0, 0)),
                  pl.BlockSpec((1, 1, d, ff), lambda k, ie, it, lo, hi, nv: (0, ie[k], 0, 0)),
                  pl.BlockSpec((1, 1, ff, d), lambda k, ie, it, lo, hi, nv: (0, ie[k], 0, 0))],
        out_specs=pl.BlockSpec((EXPERT_TILE, d), lambda k, ie, it, lo, hi, nv: (it[k], 0)),
        scratch_shapes=[pltpu.VMEM((d, ff), BF16), pltpu.VMEM((d, ff), BF16), pltpu.VMEM((ff, d), BF16)])
    return pl.pallas_call(
        _experts_kernel,
        grid_spec=grid_spec,
        out_shape=jax.ShapeDtypeStruct((p, d), F32),
        compiler_params=_params("arbitrary"),
        name="experts",
    )(*meta, xs, w_gate, w_up, w_down)


def _expert_work_items(counts, p):
    n_tiles = p // EXPERT_TILE
    n_items = n_tiles + N_EXPERTS - 1
    offs = jnp.concatenate([jnp.zeros((1,), I32), jnp.cumsum(counts)])
    first = offs[:-1] // EXPERT_TILE
    last = (offs[1:] - 1) // EXPERT_TILE
    per = jnp.where(counts > 0, last - first + 1, 0)
    ends = jnp.cumsum(per)
    starts = ends - per
    n_valid = ends[-1]
    k = jnp.arange(n_items, dtype=I32)
    kk = jnp.minimum(k, n_valid - 1)
    ie = jnp.sum((ends[None, :] <= kk[:, None]).astype(I32), axis=1)
    it = first[ie] + (kk - starts[ie])
    lo = jnp.maximum(offs[ie], it * EXPERT_TILE) - it * EXPERT_TILE
    hi = jnp.minimum(offs[ie + 1], (it + 1) * EXPERT_TILE) - it * EXPERT_TILE
    valid = k < n_valid
    lo = jnp.where(valid, lo, 0)
    hi = jnp.where(valid, hi, 0)
    return offs, (ie.astype(I32), it.astype(I32), lo.astype(I32), hi.astype(I32), n_valid.reshape(1).astype(I32))


COMBINE_TILE = 256


def _combine_kernel(pos_ref, ys_hbm, rw_ref, h1_ref, gt_ref, ln_ref, out_ref, buf, sem):
    i = pl.program_id(0)
    n_steps = pl.num_programs(0)
    tm = out_ref.shape[0]
    slot = i % 2

    def slot_copy(s, k):
        return pltpu.make_async_copy(ys_hbm.at[pl.ds(0, tm)], buf.at[s, k], sem.at[s])

    def gather(tile, s):
        def issue(r, carry):
            tok = tile * tm + r
            for k in range(EXPERT_TOP_K):
                src = pos_ref[tok * EXPERT_TOP_K + k]
                pltpu.make_async_copy(ys_hbm.at[pl.ds(src, 1)], buf.at[s, k, pl.ds(r, 1)], sem.at[s]).start()
            return carry

        lax.fori_loop(0, tm, issue, 0, unroll=8)

    @pl.when(i == 0)
    def _():
        gather(0, 0)

    @pl.when(i + 1 < n_steps)
    def _():
        gather(i + 1, 1 - slot)

    for k in range(EXPERT_TOP_K):
        slot_copy(slot, k).wait()

    rw = rw_ref[...]
    y = rw[:, RT_W1:RT_W1 + 1] * buf[slot, 0] + rw[:, RT_W2:RT_W2 + 1] * buf[slot, 1]
    out_ref[...] = _ln_rows(ALPHA * h1_ref[...] + gt_ref[0] * y, ln_ref[0:1, :], ln_ref[1:2, :])


def _combine(pos_flat, ys, rw, h1, mod3, ln2, seq):
    n, d = h1.shape
    tm = COMBINE_TILE
    tiles_per_seq = seq // tm
    return pl.pallas_call(
        _combine_kernel,
        grid=(n // tm,),
        in_specs=[pl.BlockSpec(memory_space=pltpu.SMEM),
                  pl.BlockSpec(memory_space=pl.ANY),
                  pl.BlockSpec((tm, ROUTER_LANES), lambda i: (i, 0)),
                  pl.BlockSpec((tm, d), lambda i: (i, 0)),
                  pl.BlockSpec((1, 1, d), lambda i: (i // tiles_per_seq, 0, 5)),
                  pl.BlockSpec(ln2.shape, lambda i: (0, 0))],
        out_specs=pl.BlockSpec((tm, d), lambda i: (i, 0)),
        out_shape=jax.ShapeDtypeStruct((n, d), F32),
        scratch_shapes=[pltpu.VMEM((2, EXPERT_TOP_K, tm, d), F32),
                        pltpu.SemaphoreType.DMA((2,))],
        compiler_params=_params("arbitrary"),
        name="combine",
    )(pos_flat, ys, rw, h1, mod3, ln2)


def _rope_tables(seq):
    pos = jnp.arange(seq)
    r = (pos // GRID_W).astype(F32)
    col = (pos % GRID_W).astype(F32)
    n_freq = RET_QK_DIM // 4
    inv = ROPE_BASE ** (-jnp.arange(n_freq, dtype=F32) / n_freq)
    ang = jnp.concatenate([r[:, None] * inv, col[:, None] * inv], axis=-1)
    ang = jnp.concatenate([ang, ang], axis=-1)
    sign = jnp.concatenate([-jnp.ones((RET_QK_DIM // 2,), F32), jnp.ones((RET_QK_DIM // 2,), F32)])
    return jnp.cos(ang), jnp.sin(ang) * sign


def kernel(x, c, ctx, c_ctx, w_ada, b_ada, w_in, b_in, ret_decay_fwd, ret_decay_bwd, w_ret_out, conv_dw, conv_dw_b, conv_ln_g, conv_ln_b, w_conv_out, b_conv_out, w_mix_out, b_mix_out, ln1_g, ln1_b, w_router_grp, b_router_grp, w_router_exp, b_router_exp, w_exp_gate, w_exp_up, w_exp_down, ln2_g, ln2_b):
    b_, seq, d = x.shape
    n = b_ * seq
    assert w_ada.shape[0] == DEPTH
    mod_rows = SUBLANES
    assert b_ + 1 <= mod_rows

    cs = jnp.concatenate([c, c_ctx[None, :], jnp.zeros((mod_rows - b_ - 1, d), F32)], axis=0)
    mod = _ada(cs, w_ada[0], b_ada[0][None, :])
    mod3 = mod.reshape(mod_rows, 1, 6 * d)

    w_in_bf = w_in[0].astype(BF16)
    b_in2 = b_in[0][None, :]
    dec_f = ret_decay_fwd[0].reshape(RET_HEADS, 1, 1)
    dec_b = ret_decay_bwd[0].reshape(RET_HEADS, 1, 1)

    s0f, s0b = _ctx_states(ctx, mod3, w_in_bf, b_in2, dec_f, dec_b, b_)

    cos, sin = _rope_tables(seq)
    x2 = x.reshape(n, d)
    proj = _proj(x2, mod3, w_in_bf, b_in2, cos, sin, seq, tm=min(1024, seq))

    ret = _retention(proj.reshape(b_, seq, PROJ_BLOCKS * d), s0f, s0b, dec_f, dec_b, d)

    vec = jnp.concatenate([conv_dw_b, conv_ln_g, conv_ln_b, b_conv_out, b_mix_out, ln1_g, ln1_b,
                           jnp.zeros((1, d), F32)], axis=0)
    dw = jnp.concatenate([conv_dw[0], jnp.zeros((32 - CONV_WIDTH, d), F32)], axis=0)
    pad = ROUTER_LANES - N_GROUPS - N_EXPERTS
    wr = jnp.concatenate([w_router_grp[0], w_router_exp[0], jnp.zeros((d, pad), F32)], axis=1)
    wr_hi = wr.astype(BF16)
    wr_lo = (wr - wr_hi.astype(F32)).astype(BF16)
    br = jnp.concatenate([b_router_grp[0], b_router_exp[0], jnp.zeros((pad,), F32)])[None, :]
    h1, t, route_i, route_w, cnt = _merge(
        ret.reshape(n, RET_HEADS * RET_V_DIM), proj, x2, mod3,
        w_ret_out[0].astype(BF16), w_conv_out[0].astype(BF16), w_mix_out[0].astype(BF16),
        dw, vec, wr_hi, wr_lo, br, seq, tm=min(512, seq))

    counts = cnt[0, N_GROUPS:N_GROUPS + N_EXPERTS].astype(I32)
    offs, meta = _expert_work_items(counts, n * EXPERT_TOP_K)
    pos = offs[route_i[:, RT_E1:RT_E2 + 1]] + route_i[:, RT_RANK1:RT_RANK2 + 1]
    pos_flat = pos.reshape(n * EXPERT_TOP_K)

    xs = _dispatch(pos_flat, t)
    ys = _experts(meta, xs, w_exp_gate, w_exp_up, w_exp_down)
    ln2 = jnp.concatenate([ln2_g, ln2_b, jnp.zeros((SUBLANES - 2, d), F32)], axis=0)
    out = _combine(pos_flat, ys, route_w, h1, mod3, ln2, seq)
    return out.reshape(b_, seq, d)
```

```python
import functools

import jax
import jax.numpy as jnp
import numpy as np
from jax import lax
from jax.experimental import pallas as pl
from jax.experimental.pallas import tpu as pltpu

GRID_W = 64
RET_HEADS = 8
RET_QK_DIM = 128
RET_V_DIM = 256
RET_CHUNK = 256
ROPE_BASE = 10000.0
CONV_WIDTH = 31
CONV_HALO = 16
SUBLANES = 8
N_GROUPS = 4
EXPERTS_PER_GROUP = 8
N_EXPERTS = N_GROUPS * EXPERTS_PER_GROUP
EXPERT_TOP_K = 2
LN_EPS = 1e-5
DEPTH = 1
ALPHA = (2.0 * DEPTH) ** 0.25
ROUTER_LANES = 128

V7X_VMEM_LIMIT = 56 * 1024 * 1024

F32 = jnp.float32
BF16 = jnp.bfloat16
I32 = jnp.int32
HIGHEST = lax.Precision.HIGHEST


def _params(*sem):
    return pltpu.CompilerParams(dimension_semantics=sem, vmem_limit_bytes=V7X_VMEM_LIMIT)


def _sigmoid(v):
    return 1.0 / (1.0 + jnp.exp(-v))


def _ln_rows(v, g, b):
    mu = jnp.mean(v, axis=-1, keepdims=True)
    d = v - mu
    var = jnp.mean(d * d, axis=-1, keepdims=True)
    return d * lax.rsqrt(var + LN_EPS) * g + b


def _ada_kernel(cs_ref, w_ref, b_ref, out_ref):
    s = cs_ref[...]
    s = s * _sigmoid(s)
    out_ref[...] = jnp.dot(s, w_ref[...], preferred_element_type=F32, precision=HIGHEST) + b_ref[...]


def _ada(cs, w_ada, b_ada):
    rows, d = cs.shape
    cols = w_ada.shape[1]
    tn = 1024
    return pl.pallas_call(
        _ada_kernel,
        grid=(cols // tn,),
        in_specs=[pl.BlockSpec((rows, d), lambda j: (0, 0)),
                  pl.BlockSpec((d, tn), lambda j: (0, j)),
                  pl.BlockSpec((1, tn), lambda j: (0, j))],
        out_specs=pl.BlockSpec((rows, tn), lambda j: (0, j)),
        out_shape=jax.ShapeDtypeStruct((rows, cols), F32),
        compiler_params=_params("arbitrary"),
        name="ada",
    )(cs, w_ada, b_ada)


def _log_sigmoid(v):
    return jnp.minimum(v, 0.0) - jnp.log(1.0 + jnp.exp(-jnp.abs(v)))


def _ctx_kernel(ctx_ref, sh_ref, sc_ref, wk_ref, wv_ref, bk_ref, bv_ref, df_ref, db_ref, sf_ref, sb_ref):
    lc = ctx_ref.shape[1]
    u = (ctx_ref[0] * (1.0 + sc_ref[0]) + sh_ref[0]).astype(BF16)
    k = (jnp.dot(u, wk_ref[0], preferred_element_type=F32) + bk_ref[0]) * RET_QK_DIM ** -0.5
    v = (jnp.dot(u, wv_ref[0], preferred_element_type=F32) + bv_ref[0]).astype(BF16)
    lgf = _log_sigmoid(df_ref[0])
    lgb = _log_sigmoid(db_ref[0])
    pos = lax.broadcasted_iota(I32, (lc, 1), 0).astype(F32)
    kf = (k * jnp.exp((lc - 1.0 - pos) * lgf)).astype(BF16)
    kb = (k * jnp.exp(pos * lgb)).astype(BF16)
    dn = (((0,), (0,)), ((), ()))
    sf_ref[0, 0] = lax.dot_general(kf, v, dn, preferred_element_type=F32)
    sb_ref[0, 0] = lax.dot_general(kb, v, dn, preferred_element_type=F32)


def _ctx_states(ctx, mod3, w_in3, b_in3, dec_f, dec_b, ctx_row):
    b_, lc, d = ctx.shape
    v_per_blk = d // RET_V_DIM

    def k_idx(b, h):
        return (PB_K, 0, h)

    def v_idx(b, h):
        return (PB_V + h // v_per_blk, 0, h % v_per_blk)

    st = jax.ShapeDtypeStruct((b_, RET_HEADS, RET_QK_DIM, RET_V_DIM), F32)
    st_spec = pl.BlockSpec((1, 1, RET_QK_DIM, RET_V_DIM), lambda b, h: (b, h, 0, 0))
    return pl.pallas_call(
        _ctx_kernel,
        grid=(b_, RET_HEADS),
        in_specs=[pl.BlockSpec((1, lc, d), lambda b, h: (b, 0, 0)),
                  pl.BlockSpec((1, 1, d), lambda b, h: (ctx_row, 0, 0)),
                  pl.BlockSpec((1, 1, d), lambda b, h: (ctx_row, 0, 1)),
                  pl.BlockSpec((1, d, RET_QK_DIM), k_idx),
                  pl.BlockSpec((1, d, RET_V_DIM), v_idx),
                  pl.BlockSpec((1, 1, RET_QK_DIM), k_idx),
                  pl.BlockSpec((1, 1, RET_V_DIM), v_idx),
                  pl.BlockSpec((1, 1, 1), lambda b, h: (h, 0, 0)),
                  pl.BlockSpec((1, 1, 1), lambda b, h: (h, 0, 0))],
        out_specs=[st_spec, st_spec],
        out_shape=[st, st],
        compiler_params=_params("arbitrary", "arbitrary"),
        name="ctx",
    )(ctx, mod3, mod3, w_in3, w_in3, b_in3, b_in3, dec_f, dec_b)


PROJ_BLOCKS = 9
PB_Q, PB_K, PB_V, PB_G, PB_Z, PB_GATE = 0, 1, 2, 4, 6, 7


def _proj_kernel(x_ref, sh_ref, sc_ref, w_ref, b_ref, cos_ref, sin_ref, out_ref, u_scr):
    j = pl.program_id(1)

    @pl.when(j == 0)
    def _():
        u_scr[...] = (x_ref[...] * (1.0 + sc_ref[0]) + sh_ref[0]).astype(BF16)

    def mm(blk):
        return jnp.dot(u_scr[...], w_ref[blk], preferred_element_type=F32) + b_ref[blk]

    def rope(acc):
        cos = cos_ref[...]
        sin = sin_ref[...]
        for h in range(acc.shape[1] // RET_QK_DIM):
            sl = slice(h * RET_QK_DIM, (h + 1) * RET_QK_DIM)
            seg = acc[:, sl]
            out_ref[:, sl] = (seg * cos + pltpu.roll(seg, RET_QK_DIM // 2, 1) * sin).astype(BF16)

    @pl.when(j == PB_Q)
    def _():
        rope(mm(PB_Q))

    @pl.when(j == PB_K)
    def _():
        rope(mm(PB_K) * RET_QK_DIM ** -0.5)

    @pl.when((j >= PB_V) & (j < PB_G))
    def _():
        out_ref[...] = mm(j).astype(BF16)

    @pl.when((j >= PB_G) & (j < PB_Z))
    def _():
        acc = mm(j)
        out_ref[...] = (acc * _sigmoid(acc)).astype(BF16)

    @pl.when(j == PB_Z)
    def _():
        out_ref[...] = (mm(PB_Z) * _sigmoid(mm(PB_Z + 1))).astype(BF16)

    @pl.when(j >= PB_GATE)
    def _():
        out_ref[...] = _sigmoid(mm(j + 1)).astype(BF16)


def _proj(x2, mod3, w_in3, b_in3, cos, sin, seq, tm):
    n, d = x2.shape
    tiles_per_seq = seq // tm
    return pl.pallas_call(
        _proj_kernel,
        grid=(n // tm, PROJ_BLOCKS),
        in_specs=[pl.BlockSpec((tm, d), lambda i, j: (i, 0)),
                  pl.BlockSpec((1, 1, d), lambda i, j: (i // tiles_per_seq, 0, 0)),
                  pl.BlockSpec((1, 1, d), lambda i, j: (i // tiles_per_seq, 0, 1)),
                  pl.BlockSpec(memory_space=pltpu.VMEM),
                  pl.BlockSpec(memory_space=pltpu.VMEM),
                  pl.BlockSpec((tm, RET_QK_DIM), lambda i, j: (i % tiles_per_seq, 0)),
                  pl.BlockSpec((tm, RET_QK_DIM), lambda i, j: (i % tiles_per_seq, 0))],
        out_specs=pl.BlockSpec((tm, d), lambda i, j: (i, j)),
        out_shape=jax.ShapeDtypeStruct((n, PROJ_BLOCKS * d), BF16),
        scratch_shapes=[pltpu.VMEM((tm, d), BF16)],
        compiler_params=_params("arbitrary", "arbitrary"),
        name="proj",
    )(x2, mod3, mod3, w_in3, b_in3, cos, sin)


RET_UNROLL = 4


def _ret_kernel(q_ref, k_ref, v_ref, g_ref, s0f_ref, s0b_ref, df_ref, db_ref, out_ref,
                sf_all, sb_all, p_scr, o_scr):
    c_ = RET_CHUNK
    n_chunks = q_ref.shape[1] // c_
    dn_t = (((0,), (0,)), ((), ()))
    lgf = _log_sigmoid(df_ref[0])
    lgb = _log_sigmoid(db_ref[0])
    ri = lax.broadcasted_iota(I32, (c_, c_), 0)
    ci = lax.broadcasted_iota(I32, (c_, c_), 1)
    diff = (ri - ci).astype(F32)
    mask = jnp.where(diff > 0, jnp.exp(diff * lgf), jnp.where(diff < 0, jnp.exp(-diff * lgb), 2.0))
    pos = lax.broadcasted_iota(I32, (c_, 1), 0).astype(F32)
    qdec_f = jnp.exp((pos + 1.0) * lgf)
    qdec_b = jnp.exp((c_ - pos) * lgb)
    kdec_f = jnp.exp((c_ - 1.0 - pos) * lgf)
    kdec_b = jnp.exp(pos * lgb)
    cdec_f = jnp.exp(c_ * lgf)
    cdec_b = jnp.exp(c_ * lgb)

    def chunk(c):
        return pl.ds(pl.multiple_of(c * c_, c_), c_)

    def products(c, carry):
        k = k_ref[0, chunk(c), :].astype(F32)
        v = v_ref[0, chunk(c), :]
        sf_all[c] = lax.dot_general((k * kdec_f).astype(BF16), v, dn_t, preferred_element_type=F32)
        sb_all[c] = lax.dot_general((k * kdec_b).astype(BF16), v, dn_t, preferred_element_type=F32)
        return carry

    lax.fori_loop(0, n_chunks, products, 0, unroll=RET_UNROLL)

    def scan_f(c, s):
        kv = sf_all[c]
        sf_all[c] = s
        return s * cdec_f + kv

    def scan_b(t, s):
        c = n_chunks - 1 - t
        kv = sb_all[c]
        sb_all[c] = s
        return s * cdec_b + kv

    lax.fori_loop(0, n_chunks, scan_f, s0f_ref[0, 0])
    lax.fori_loop(0, n_chunks, scan_b, s0b_ref[0, 0])

    def scores(c, carry):
        s = lax.dot_general(q_ref[0, chunk(c), :], k_ref[0, chunk(c), :], (((1,), (1,)), ((), ())),
                            preferred_element_type=F32)
        p_scr[c] = (s * mask).astype(BF16)
        return carry

    lax.fori_loop(0, n_chunks, scores, 0, unroll=RET_UNROLL)

    def outputs(c, carry):
        q = q_ref[0, chunk(c), :]
        o = jnp.dot(p_scr[c], v_ref[0, chunk(c), :], preferred_element_type=F32)
        o = o + qdec_f * jnp.dot(q, sf_all[c].astype(BF16), preferred_element_type=F32)
        o = o + qdec_b * jnp.dot(q, sb_all[c].astype(BF16), preferred_element_type=F32)
        o_scr[chunk(c), :] = o
        return carry

    lax.fori_loop(0, n_chunks, outputs, 0, unroll=RET_UNROLL)

    def norm(c, carry):
        o = o_scr[chunk(c), :]
        mu = jnp.mean(o, axis=-1, keepdims=True)
        d = o - mu
        var = jnp.mean(d * d, axis=-1, keepdims=True)
        on = d * lax.rsqrt(var + LN_EPS)
        out_ref[0, chunk(c), :] = (on * g_ref[0, chunk(c), :].astype(F32)).astype(BF16)
        return carry

    lax.fori_loop(0, n_chunks, norm, 0, unroll=RET_UNROLL)


def _retention(proj3, s0f, s0b, dec_f, dec_b, d):
    b_, seq, _ = proj3.shape
    qb, kb = PB_Q * d // RET_QK_DIM, PB_K * d // RET_QK_DIM
    vb, gb = PB_V * d // RET_V_DIM, PB_G * d // RET_V_DIM
    n_chunks = seq // RET_CHUNK
    st_spec = pl.BlockSpec((1, 1, RET_QK_DIM, RET_V_DIM), lambda b, h: (b, h, 0, 0))
    st_all = pltpu.VMEM((n_chunks, RET_QK_DIM, RET_V_DIM), F32)
    return pl.pallas_call(
        _ret_kernel,
        grid=(b_, RET_HEADS),
        in_specs=[pl.BlockSpec((1, seq, RET_QK_DIM), lambda b, h: (b, 0, qb + h)),
                  pl.BlockSpec((1, seq, RET_QK_DIM), lambda b, h: (b, 0, kb + h)),
                  pl.BlockSpec((1, seq, RET_V_DIM), lambda b, h: (b, 0, vb + h)),
                  pl.BlockSpec((1, seq, RET_V_DIM), lambda b, h: (b, 0, gb + h)),
                  st_spec, st_spec,
                  pl.BlockSpec((1, 1, 1), lambda b, h: (h, 0, 0)),
                  pl.BlockSpec((1, 1, 1), lambda b, h: (h, 0, 0))],
        out_specs=pl.BlockSpec((1, seq, RET_V_DIM), lambda b, h: (b, 0, h)),
        out_shape=jax.ShapeDtypeStruct((b_, seq, RET_HEADS * RET_V_DIM), BF16),
        scratch_shapes=[st_all, st_all,
                        pltpu.VMEM((n_chunks, RET_CHUNK, RET_CHUNK), BF16),
                        pltpu.VMEM((seq, RET_V_DIM), F32)],
        compiler_params=_params("arbitrary", "arbitrary"),
        name="ret",
    )(proj3, proj3, proj3, proj3, s0f, s0b, dec_f, dec_b)


VEC_CONV_B, VEC_CLN_G, VEC_CLN_B, VEC_BCONV, VEC_BMIX, VEC_LN1_G, VEC_LN1_B = range(7)
CONV_ROWS = 64
LANES = 128
RT_E1, RT_E2, RT_RANK1, RT_RANK2 = 0, 1, 2, 3
RT_W1, RT_W2 = 0, 1


def _split_bf16(v):
    hi = v.astype(BF16)
    return hi, (v - hi.astype(F32)).astype(BF16)


def _merge_kernel(ret_ref, z_ref, zp_ref, zn_ref, ga_ref, gb_ref, x_ref, gt_ref, shf_ref, scf_ref,
                  wret_ref, wconv_ref, wmix_ref, dw_ref, vec_ref, wrh_ref, wrl_ref, br_ref,
                  h1_ref, t_ref, ri_ref, rw_ref, cnt_ref, zext, zc_scr, cnt_scr, *, tiles_per_seq):
    tm, d = x_ref.shape
    step = pl.program_id(0)
    ti = step % tiles_per_seq
    vec = vec_ref[...]

    def row(r):
        return vec[r:r + 1, :]

    y_a = jnp.dot(ret_ref[...], wret_ref[...], preferred_element_type=F32)

    has_prev = (ti > 0).astype(F32)
    has_next = (ti < tiles_per_seq - 1).astype(F32)
    zext[0:CONV_HALO, :] = zp_ref[...].astype(F32) * has_prev
    zext[CONV_HALO:CONV_HALO + tm, :] = z_ref[...].astype(F32)
    zext[CONV_HALO + tm:, :] = zn_ref[...].astype(F32) * has_next
    base = CONV_HALO - CONV_WIDTH // 2
    win_rows = CONV_ROWS + 2 * CONV_HALO

    def conv_rows(r, carry):
        r0 = pl.multiple_of(r * CONV_ROWS, CONV_ROWS)
        for lt in range(d // LANES):
            ls = slice(lt * LANES, (lt + 1) * LANES)
            win = zext[pl.ds(r0, win_rows), ls]
            acc = jnp.zeros((CONV_ROWS, LANES), F32) + vec_ref[VEC_CONV_B:VEC_CONV_B + 1, ls]
            for s in range(SUBLANES):
                taps = [w for w in range(CONV_WIDTH) if (base + w) % SUBLANES == s]
                if not taps:
                    continue
                sh = win if s == 0 else pltpu.roll(win, win_rows - s, 0)
                for w in taps:
                    a = (base + w) - s
                    acc = acc + sh[a:a + CONV_ROWS, :] * dw_ref[w:w + 1, ls]
            zc_scr[pl.ds(r0, CONV_ROWS), ls] = acc
        return carry

    lax.fori_loop(0, tm // CONV_ROWS, conv_rows, 0)
    zc = _ln_rows(zc_scr[...], row(VEC_CLN_G), row(VEC_CLN_B))
    zc = zc * _sigmoid(zc)
    y_b = jnp.dot(zc.astype(BF16), wconv_ref[...], preferred_element_type=F32) + row(VEC_BCONV)

    mixed = ga_ref[...].astype(F32) * y_a + gb_ref[...].astype(F32) * y_b
    mix = jnp.dot(mixed.astype(BF16), wmix_ref[...], preferred_element_type=F32) + row(VEC_BMIX)
    h1 = _ln_rows(ALPHA * x_ref[...] + gt_ref[0] * mix, row(VEC_LN1_G), row(VEC_LN1_B))
    h1_ref[...] = h1
    t = h1 * (1.0 + scf_ref[0]) + shf_ref[0]
    t_ref[...] = t

    t_hi, t_lo = _split_bf16(t)
    logits = (jnp.dot(t_hi, wrh_ref[...], preferred_element_type=F32)
              + jnp.dot(t_lo, wrh_ref[...], preferred_element_type=F32)
              + jnp.dot(t_hi, wrl_ref[...], preferred_element_type=F32)) + br_ref[...]

    lane = lax.broadcasted_iota(I32, logits.shape, 1).astype(F32)
    neg = -jnp.inf
    big = float(ROUTER_LANES)
    is_grp = lane < N_GROUPS
    gl = jnp.where(is_grp, logits, neg)
    gmax = jnp.max(gl, axis=-1, keepdims=True)
    gidx = jnp.min(jnp.where(gl == gmax, lane, big), axis=-1, keepdims=True)
    gsum = jnp.sum(jnp.where(is_grp, jnp.exp(gl - gmax), 0.0), axis=-1, keepdims=True)
    grp_w = 1.0 / gsum
    lo = N_GROUPS + gidx * EXPERTS_PER_GROUP
    el = jnp.where(lane >= lo, jnp.where(lane < lo + EXPERTS_PER_GROUP, logits, neg), neg)
    m1 = jnp.max(el, axis=-1, keepdims=True)
    i1 = jnp.min(jnp.where(el == m1, lane, big), axis=-1, keepdims=True)
    el2 = jnp.where(lane == i1, neg, el)
    m2 = jnp.max(el2, axis=-1, keepdims=True)
    i2 = jnp.min(jnp.where(el2 == m2, lane, big), axis=-1, keepdims=True)
    r = jnp.exp(m2 - m1)
    w1 = grp_w / (1.0 + r)
    w2 = grp_w * r / (1.0 + r)

    @pl.when(step == 0)
    def _():
        cnt_scr[...] = jnp.zeros_like(cnt_scr)

    oh1 = lane == i1
    oh2 = lane == i2
    oh = jnp.where(oh1, 1.0, jnp.where(oh2, 1.0, 0.0))
    tri = jnp.where(lax.broadcasted_iota(I32, (tm, tm), 0) > lax.broadcasted_iota(I32, (tm, tm), 1), 1.0, 0.0)
    before = jnp.dot(tri.astype(BF16), oh.astype(BF16), preferred_element_type=F32) + cnt_scr[0:1, :]
    rank1 = jnp.sum(jnp.where(oh1, before, 0.0), axis=-1, keepdims=True)
    rank2 = jnp.sum(jnp.where(oh2, before, 0.0), axis=-1, keepdims=True)
    cnt = cnt_scr[0:1, :] + jnp.sum(oh, axis=0, keepdims=True)
    cnt_scr[...] = jnp.broadcast_to(cnt, cnt_scr.shape)
    cnt_ref[...] = jnp.broadcast_to(cnt, cnt_ref.shape)

    e1 = i1 - N_GROUPS
    e2 = i2 - N_GROUPS
    ri = jnp.where(lane == RT_E1, e1, jnp.where(lane == RT_E2, e2,
                   jnp.where(lane == RT_RANK1, rank1, jnp.where(lane == RT_RANK2, rank2, 0.0))))
    ri_ref[...] = ri.astype(I32)
    rw_ref[...] = jnp.where(lane == RT_W1, w1, jnp.where(lane == RT_W2, w2, 0.0))


def _merge(ret2, proj2, x2, mod3, wret, wconv, wmix, dw, vec, wrh, wrl, br, seq, tm):
    n, d = x2.shape
    tiles_per_seq = seq // tm
    hb = tm // CONV_HALO
    last_hb = n // CONV_HALO - 1
    const = lambda i: (0, 0)
    bat = lambda k: (lambda i: (i // tiles_per_seq, 0, k))
    return pl.pallas_call(
        functools.partial(_merge_kernel, tiles_per_seq=tiles_per_seq),
        grid=(n // tm,),
        in_specs=[pl.BlockSpec((tm, ret2.shape[1]), lambda i: (i, 0)),
                  pl.BlockSpec((tm, d), lambda i: (i, PB_Z)),
                  pl.BlockSpec((CONV_HALO, d), lambda i: (jnp.maximum(i * hb - 1, 0), PB_Z)),
                  pl.BlockSpec((CONV_HALO, d), lambda i: (jnp.minimum((i + 1) * hb, last_hb), PB_Z)),
                  pl.BlockSpec((tm, d), lambda i: (i, PB_GATE)),
                  pl.BlockSpec((tm, d), lambda i: (i, PB_GATE + 1)),
                  pl.BlockSpec((tm, d), lambda i: (i, 0)),
                  pl.BlockSpec((1, 1, d), bat(2)),
                  pl.BlockSpec((1, 1, d), bat(3)),
                  pl.BlockSpec((1, 1, d), bat(4)),
                  pl.BlockSpec(wret.shape, const),
                  pl.BlockSpec(wconv.shape, const),
                  pl.BlockSpec(wmix.shape, const),
                  pl.BlockSpec(dw.shape, const),
                  pl.BlockSpec(vec.shape, const),
                  pl.BlockSpec(wrh.shape, const),
                  pl.BlockSpec(wrl.shape, const),
                  pl.BlockSpec(br.shape, const)],
        out_specs=[pl.BlockSpec((tm, d), lambda i: (i, 0)),
                   pl.BlockSpec((tm, d), lambda i: (i, 0)),
                   pl.BlockSpec((tm, ROUTER_LANES), lambda i: (i, 0)),
                   pl.BlockSpec((tm, ROUTER_LANES), lambda i: (i, 0)),
                   pl.BlockSpec((SUBLANES, ROUTER_LANES), const)],
        out_shape=[jax.ShapeDtypeStruct((n, d), F32),
                   jax.ShapeDtypeStruct((n, d), F32),
                   jax.ShapeDtypeStruct((n, ROUTER_LANES), I32),
                   jax.ShapeDtypeStruct((n, ROUTER_LANES), F32),
                   jax.ShapeDtypeStruct((SUBLANES, ROUTER_LANES), F32)],
        scratch_shapes=[pltpu.VMEM((tm + 2 * CONV_HALO, d), F32),
                        pltpu.VMEM((tm, d), F32),
                        pltpu.VMEM((SUBLANES, ROUTER_LANES), F32)],
        compiler_params=_params("arbitrary"),
        name="merge",
    )(ret2, proj2, proj2, proj2, proj2, proj2, x2, mod3, mod3, mod3,
      wret, wconv, wmix, dw, vec, wrh, wrl, br)


DISPATCH_TILE = 1024


def _dispatch_kernel(pos_ref, t_ref, xs_hbm, sem):
    tm = t_ref.shape[0]
    base = pl.program_id(0) * tm

    def issue(r, carry):
        for k in range(EXPERT_TOP_K):
            dst = pos_ref[(base + r) * EXPERT_TOP_K + k]
            pltpu.make_async_copy(t_ref.at[pl.ds(r, 1)], xs_hbm.at[pl.ds(dst, 1)], sem).start()
        return carry

    lax.fori_loop(0, tm, issue, 0, unroll=8)
    for k in range(EXPERT_TOP_K):
        pltpu.make_async_copy(t_ref, xs_hbm.at[pl.ds(0, tm)], sem).wait()


def _dispatch(pos_flat, t2):
    n, d = t2.shape
    tm = min(DISPATCH_TILE, n)
    return pl.pallas_call(
        _dispatch_kernel,
        grid=(n // tm,),
        in_specs=[pl.BlockSpec(memory_space=pltpu.SMEM),
                  pl.BlockSpec((tm, d), lambda i: (i, 0))],
        out_specs=pl.BlockSpec(memory_space=pl.ANY),
        out_shape=jax.ShapeDtypeStruct((n * EXPERT_TOP_K, d), F32),
        scratch_shapes=[pltpu.SemaphoreType.DMA],
        compiler_params=_params("arbitrary"),
        name="dispatch",
    )(pos_flat, t2)


EXPERT_TILE = 512


def _experts_kernel(ie_ref, it_ref, lo_ref, hi_ref, nv_ref, xs_ref, wg_ref, wu_ref, wd_ref, ys_ref,
                    wg_bf, wu_bf, wd_bf):
    k = pl.program_id(0)
    prev = jnp.maximum(k - 1, 0)
    new_expert = (k == 0) | (ie_ref[k] != ie_ref[prev])
    new_tile = (k == 0) | (it_ref[k] != it_ref[prev])

    @pl.when(new_expert)
    def _():
        wg_bf[...] = wg_ref[0, 0].astype(BF16)
        wu_bf[...] = wu_ref[0, 0].astype(BF16)
        wd_bf[...] = wd_ref[0, 0].astype(BF16)

    @pl.when(k < nv_ref[0])
    def _():
        x = xs_ref[...].astype(BF16)
        g = jnp.dot(x, wg_bf[...], preferred_element_type=F32)
        u = jnp.dot(x, wu_bf[...], preferred_element_type=F32)
        hid = (g * _sigmoid(g) * u).astype(BF16)
        y = jnp.dot(hid, wd_bf[...], preferred_element_type=F32)
        rows = lax.broadcasted_iota(I32, (xs_ref.shape[0], 1), 0)
        mine = (rows >= lo_ref[k]) & (rows < hi_ref[k])

        @pl.when(new_tile)
        def _():
            ys_ref[...] = jnp.where(mine, y, 0.0)

        @pl.when(jnp.logical_not(new_tile))
        def _():
            ys_ref[...] = jnp.where(mine, y, ys_ref[...])


def _experts(meta, xs, w_gate, w_up, w_down):
    p, d = xs.shape
    ff = w_gate.shape[-1]
    n_items = meta[0].shape[0]
    grid_spec = pltpu.PrefetchScalarGridSpec(
        num_scalar_prefetch=5,
        grid=(n_items,),
        in_specs=[pl.BlockSpec((EXPERT_TILE, d), lambda k, ie, it, lo, hi, nv: (it[k], 0)),
                  pl.BlockSpec((1, 1, d, ff), lambda k, ie, it, lo, hi, nv: (0, ie[k], 0, 0)),
                  pl.BlockSpec((1, 1, d, ff), lambda k, ie, it, lo, hi, nv: (0, ie[k], 0, 0)),
                  pl.BlockSpec((1, 1, ff, d), lambda k, ie, it, lo, hi, nv: (0, ie[k], 0, 0))],
        out_specs=pl.BlockSpec((EXPERT_TILE, d), lambda k, ie, it, lo, hi, nv: (it[k], 0)),
        scratch_shapes=[pltpu.VMEM((d, ff), BF16), pltpu.VMEM((d, ff), BF16), pltpu.VMEM((ff, d), BF16)])
    return pl.pallas_call(
        _experts_kernel,
        grid_spec=grid_spec,
        out_shape=jax.ShapeDtypeStruct((p, d), F32),
        compiler_params=_params("arbitrary"),
        name="experts",
    )(*meta, xs, w_gate, w_up, w_down)


def _expert_work_items(counts, p):
    n_tiles = p // EXPERT_TILE
    n_items = n_tiles + N_EXPERTS - 1
    offs = jnp.concatenate([jnp.zeros((1,), I32), jnp.cumsum(counts)])
    first = offs[:-1] // EXPERT_TILE
    last = (offs[1:] - 1) // EXPERT_TILE
    per = jnp.where(counts > 0, last - first + 1, 0)
    ends = jnp.cumsum(per)
    starts = ends - per
    n_valid = ends[-1]
    k = jnp.arange(n_items, dtype=I32)
    kk = jnp.minimum(k, n_valid - 1)
    ie = jnp.sum((ends[None, :] <= kk[:, None]).astype(I32), axis=1)
    it = first[ie] + (kk - starts[ie])
    lo = jnp.maximum(offs[ie], it * EXPERT_TILE) - it * EXPERT_TILE
    hi = jnp.minimum(offs[ie + 1], (it + 1) * EXPERT_TILE) - it * EXPERT_TILE
    valid = k < n_valid
    lo = jnp.where(valid, lo, 0)
    hi = jnp.where(valid, hi, 0)
    return offs, (ie.astype(I32), it.astype(I32), lo.astype(I32), hi.astype(I32), n_valid.reshape(1).astype(I32))


COMBINE_TILE = 256


def _combine_kernel(pos_ref, ys_hbm, rw_ref, h1_ref, gt_ref, ln_ref, out_ref, buf, sem):
    i = pl.program_id(0)
    n_steps = pl.num_programs(0)
    tm = out_ref.shape[0]
    slot = i % 2

    def slot_copy(s, k):
        return pltpu.make_async_copy(ys_hbm.at[pl.ds(0, tm)], buf.at[s, k], sem.at[s])

    def gather(tile, s):
        def issue(r, carry):
            tok = tile * tm + r
            for k in range(EXPERT_TOP_K):
                src = pos_ref[tok * EXPERT_TOP_K + k]
                pltpu.make_async_copy(ys_hbm.at[pl.ds(src, 1)], buf.at[s, k, pl.ds(r, 1)], sem.at[s]).start()
            return carry

        lax.fori_loop(0, tm, issue, 0, unroll=8)

    @pl.when(i == 0)
    def _():
        gather(0, 0)

    @pl.when(i + 1 < n_steps)
    def _():
        gather(i + 1, 1 - slot)

    for k in range(EXPERT_TOP_K):
        slot_copy(slot, k).wait()

    rw = rw_ref[...]
    y = rw[:, RT_W1:RT_W1 + 1] * buf[slot, 0] + rw[:, RT_W2:RT_W2 + 1] * buf[slot, 1]
    out_ref[...] = _ln_rows(ALPHA * h1_ref[...] + gt_ref[0] * y, ln_ref[0:1, :], ln_ref[1:2, :])


def _combine(pos_flat, ys, rw, h1, mod3, ln2, seq):
    n, d = h1.shape
    tm = COMBINE_TILE
    tiles_per_seq = seq // tm
    return pl.pallas_call(
        _combine_kernel,
        grid=(n // tm,),
        in_specs=[pl.BlockSpec(memory_space=pltpu.SMEM),
                  pl.BlockSpec(memory_space=pl.ANY),
                  pl.BlockSpec((tm, ROUTER_LANES), lambda i: (i, 0)),
                  pl.BlockSpec((tm, d), lambda i: (i, 0)),
                  pl.BlockSpec((1, 1, d), lambda i: (i // tiles_per_seq, 0, 5)),
                  pl.BlockSpec(ln2.shape, lambda i: (0, 0))],
        out_specs=pl.BlockSpec((tm, d), lambda i: (i, 0)),
        out_shape=jax.ShapeDtypeStruct((n, d), F32),
        scratch_shapes=[pltpu.VMEM((2, EXPERT_TOP_K, tm, d), F32),
                        pltpu.SemaphoreType.DMA((2,))],
        compiler_params=_params("arbitrary"),
        name="combine",
    )(pos_flat, ys, rw, h1, mod3, ln2)


def _rope_tables(seq):
    pos = np.arange(seq)
    r = (pos // GRID_W).astype(np.float32)
    col = (pos % GRID_W).astype(np.float32)
    n_freq = RET_QK_DIM // 4
    inv = (np.float32(ROPE_BASE) ** (-np.arange(n_freq, dtype=np.float32) / np.float32(n_freq))).astype(np.float32)
    ang = np.concatenate([r[:, None] * inv, col[:, None] * inv], axis=-1)
    ang = np.concatenate([ang, ang], axis=-1).astype(np.float32)
    sign = np.concatenate([-np.ones((RET_QK_DIM // 2,), np.float32), np.ones((RET_QK_DIM // 2,), np.float32)])
    return jnp.asarray(np.cos(ang), F32), jnp.asarray(np.sin(ang) * sign, F32)


def kernel(x, c, ctx, c_ctx, w_ada, b_ada, w_in, b_in, ret_decay_fwd, ret_decay_bwd, w_ret_out, conv_dw, conv_dw_b, conv_ln_g, conv_ln_b, w_conv_out, b_conv_out, w_mix_out, b_mix_out, ln1_g, ln1_b, w_router_grp, b_router_grp, w_router_exp, b_router_exp, w_exp_gate, w_exp_up, w_exp_down, ln2_g, ln2_b):
    b_, seq, d = x.shape
    n = b_ * seq
    assert w_ada.shape[0] == DEPTH
    mod_rows = SUBLANES
    assert b_ + 1 <= mod_rows

    cs = jnp.concatenate([c, c_ctx[None, :], jnp.zeros((mod_rows - b_ - 1, d), F32)], axis=0)
    mod = _ada(cs, w_ada[0], b_ada[0][None, :])
    mod3 = mod.reshape(mod_rows, 1, 6 * d)

    in_blocks = w_in.shape[-1] // d
    w_in3 = w_in[0].astype(BF16).reshape(d, in_blocks, d).transpose(1, 0, 2)
    b_in3 = b_in[0].reshape(in_blocks, 1, d)
    dec_f = ret_decay_fwd[0].reshape(RET_HEADS, 1, 1)
    dec_b = ret_decay_bwd[0].reshape(RET_HEADS, 1, 1)

    s0f, s0b = _ctx_states(ctx, mod3, w_in3, b_in3, dec_f, dec_b, b_)

    cos, sin = _rope_tables(seq)
    x2 = x.reshape(n, d)
    proj = _proj(x2, mod3, w_in3, b_in3, cos, sin, seq, tm=min(1024, seq))

    ret = _retention(proj.reshape(b_, seq, PROJ_BLOCKS * d), s0f, s0b, dec_f, dec_b, d)

    vec = jnp.concatenate([conv_dw_b, conv_ln_g, conv_ln_b, b_conv_out, b_mix_out, ln1_g, ln1_b,
                           jnp.zeros((1, d), F32)], axis=0)
    dw = jnp.concatenate([conv_dw[0], jnp.zeros((32 - CONV_WIDTH, d), F32)], axis=0)
    pad = ROUTER_LANES - N_GROUPS - N_EXPERTS
    wr = jnp.concatenate([w_router_grp[0], w_router_exp[0], jnp.zeros((d, pad), F32)], axis=1)
    wr_hi = wr.astype(BF16)
    wr_lo = (wr - wr_hi.astype(F32)).astype(BF16)
    br = jnp.concatenate([b_router_grp[0], b_router_exp[0], jnp.zeros((pad,), F32)])[None, :]
    h1, t, route_i, route_w, cnt = _merge(
        ret.reshape(n, RET_HEADS * RET_V_DIM), proj, x2, mod3,
        w_ret_out[0].astype(BF16), w_conv_out[0].astype(BF16), w_mix_out[0].astype(BF16),
        dw, vec, wr_hi, wr_lo, br, seq, tm=min(512, seq))

    counts = cnt[0, N_GROUPS:N_GROUPS + N_EXPERTS].astype(I32)
    offs, meta = _expert_work_items(counts, n * EXPERT_TOP_K)
    eid = route_i[:, RT_E1:RT_E2 + 1]
    seg_start = jnp.sum(jnp.where(eid[:, :, None] == jnp.arange(N_EXPERTS, dtype=I32), offs[:N_EXPERTS], 0), axis=-1)
    pos = seg_start + route_i[:, RT_RANK1:RT_RANK2 + 1]
    pos_flat = pos.reshape(n * EXPERT_TOP_K)

    xs = _dispatch(pos_flat, t)
    ys = _experts(meta, xs, w_exp_gate, w_exp_up, w_exp_down)
    ln2 = jnp.concatenate([ln2_g, ln2_b, jnp.zeros((SUBLANES - 2, d), F32)], axis=0)
    out = _combine(pos_flat, ys, route_w, h1, mod3, ln2, seq)
    return out.reshape(b_, seq, d)
```

```python
import functools

import jax
import jax.numpy as jnp
import numpy as np
from jax import lax
from jax.experimental import pallas as pl
from jax.experimental.pallas import tpu as pltpu

GRID_W = 64
RET_HEADS = 8
RET_QK_DIM = 128
RET_V_DIM = 256
RET_CHUNK = 256
ROPE_BASE = 10000.0
CONV_WIDTH = 31
CONV_HALO = 16
SUBLANES = 8
N_GROUPS = 4
EXPERTS_PER_GROUP = 8
N_EXPERTS = N_GROUPS * EXPERTS_PER_GROUP
EXPERT_TOP_K = 2
LN_EPS = 1e-5
DEPTH = 1
ALPHA = (2.0 * DEPTH) ** 0.25
ROUTER_LANES = 128

V7X_VMEM_LIMIT = 56 * 1024 * 1024

F32 = jnp.float32
BF16 = jnp.bfloat16
I32 = jnp.int32
HIGHEST = lax.Precision.HIGHEST


def _params(*sem):
    return pltpu.CompilerParams(dimension_semantics=sem, vmem_limit_bytes=V7X_VMEM_LIMIT)


def _sigmoid(v):
    return 1.0 / (1.0 + jnp.exp(-v))


def _ln_rows(v, g, b):
    mu = jnp.mean(v, axis=-1, keepdims=True)
    d = v - mu
    var = jnp.mean(d * d, axis=-1, keepdims=True)
    return d * lax.rsqrt(var + LN_EPS) * g + b


def _ada_kernel(cs_ref, w_ref, b_ref, out_ref):
    s = cs_ref[...]
    s = s * _sigmoid(s)
    out_ref[...] = jnp.dot(s, w_ref[...], preferred_element_type=F32, precision=HIGHEST) + b_ref[...]


def _ada(cs, w_ada, b_ada):
    rows, d = cs.shape
    cols = w_ada.shape[1]
    tn = 1024
    return pl.pallas_call(
        _ada_kernel,
        grid=(cols // tn,),
        in_specs=[pl.BlockSpec((rows, d), lambda j: (0, 0)),
                  pl.BlockSpec((d, tn), lambda j: (0, j)),
                  pl.BlockSpec((1, tn), lambda j: (0, j))],
        out_specs=pl.BlockSpec((rows, tn), lambda j: (0, j)),
        out_shape=jax.ShapeDtypeStruct((rows, cols), F32),
        compiler_params=_params("arbitrary"),
        name="ada",
    )(cs, w_ada, b_ada)


def _log_sigmoid(v):
    return jnp.minimum(v, 0.0) - jnp.log(1.0 + jnp.exp(-jnp.abs(v)))


def _ctx_kernel(ctx_ref, sh_ref, sc_ref, wk_ref, wv_ref, bk_ref, bv_ref, df_ref, db_ref, sf_ref, sb_ref):
    lc = ctx_ref.shape[1]
    u = (ctx_ref[0] * (1.0 + sc_ref[0]) + sh_ref[0]).astype(BF16)
    k = (jnp.dot(u, wk_ref[0], preferred_element_type=F32) + bk_ref[0]) * RET_QK_DIM ** -0.5
    v = (jnp.dot(u, wv_ref[0], preferred_element_type=F32) + bv_ref[0]).astype(BF16)
    lgf = _log_sigmoid(df_ref[0])
    lgb = _log_sigmoid(db_ref[0])
    pos = lax.broadcasted_iota(I32, (lc, 1), 0).astype(F32)
    kf = (k * jnp.exp((lc - 1.0 - pos) * lgf)).astype(BF16)
    kb = (k * jnp.exp(pos * lgb)).astype(BF16)
    dn = (((0,), (0,)), ((), ()))
    sf_ref[0, 0] = lax.dot_general(kf, v, dn, preferred_element_type=F32)
    sb_ref[0, 0] = lax.dot_general(kb, v, dn, preferred_element_type=F32)


def _ctx_states(ctx, mod3, w_in3, b_in3, dec_f, dec_b, ctx_row):
    b_, lc, d = ctx.shape
    v_per_blk = d // RET_V_DIM

    def k_idx(b, h):
        return (PB_K, 0, h)

    def v_idx(b, h):
        return (PB_V + h // v_per_blk, 0, h % v_per_blk)

    st = jax.ShapeDtypeStruct((b_, RET_HEADS, RET_QK_DIM, RET_V_DIM), F32)
    st_spec = pl.BlockSpec((1, 1, RET_QK_DIM, RET_V_DIM), lambda b, h: (b, h, 0, 0))
    return pl.pallas_call(
        _ctx_kernel,
        grid=(b_, RET_HEADS),
        in_specs=[pl.BlockSpec((1, lc, d), lambda b, h: (b, 0, 0)),
                  pl.BlockSpec((1, 1, d), lambda b, h: (ctx_row, 0, 0)),
                  pl.BlockSpec((1, 1, d), lambda b, h: (ctx_row, 0, 1)),
                  pl.BlockSpec((1, d, RET_QK_DIM), k_idx),
                  pl.BlockSpec((1, d, RET_V_DIM), v_idx),
                  pl.BlockSpec((1, 1, RET_QK_DIM), k_idx),
                  pl.BlockSpec((1, 1, RET_V_DIM), v_idx),
                  pl.BlockSpec((1, 1, 1), lambda b, h: (h, 0, 0)),
                  pl.BlockSpec((1, 1, 1), lambda b, h: (h, 0, 0))],
        out_specs=[st_spec, st_spec],
        out_shape=[st, st],
        compiler_params=_params("arbitrary", "arbitrary"),
        name="ctx",
    )(ctx, mod3, mod3, w_in3, w_in3, b_in3, b_in3, dec_f, dec_b)


PROJ_BLOCKS = 9
PB_Q, PB_K, PB_V, PB_G, PB_Z, PB_GATE = 0, 1, 2, 4, 6, 7


def _proj_kernel(x_ref, sh_ref, sc_ref, w_ref, b_ref, cos_ref, sin_ref, out_ref, u_scr):
    j = pl.program_id(1)

    @pl.when(j == 0)
    def _():
        u_scr[...] = (x_ref[...] * (1.0 + sc_ref[0]) + sh_ref[0]).astype(BF16)

    def mm(blk):
        return jnp.dot(u_scr[...], w_ref[blk], preferred_element_type=F32) + b_ref[blk]

    def rope(acc):
        cos = cos_ref[...]
        sin = sin_ref[...]
        for h in range(acc.shape[1] // RET_QK_DIM):
            sl = slice(h * RET_QK_DIM, (h + 1) * RET_QK_DIM)
            seg = acc[:, sl]
            out_ref[:, sl] = (seg * cos + pltpu.roll(seg, RET_QK_DIM // 2, 1) * sin).astype(BF16)

    @pl.when(j == PB_Q)
    def _():
        rope(mm(PB_Q))

    @pl.when(j == PB_K)
    def _():
        rope(mm(PB_K) * RET_QK_DIM ** -0.5)

    @pl.when((j >= PB_V) & (j < PB_G))
    def _():
        out_ref[...] = mm(j).astype(BF16)

    @pl.when((j >= PB_G) & (j < PB_Z))
    def _():
        acc = mm(j)
        out_ref[...] = (acc * _sigmoid(acc)).astype(BF16)

    @pl.when(j == PB_Z)
    def _():
        out_ref[...] = (mm(PB_Z) * _sigmoid(mm(PB_Z + 1))).astype(BF16)

    @pl.when(j >= PB_GATE)
    def _():
        out_ref[...] = _sigmoid(mm(j + 1)).astype(BF16)


def _proj(x2, mod3, w_in3, b_in3, cos, sin, seq, tm):
    n, d = x2.shape
    tiles_per_seq = seq // tm
    return pl.pallas_call(
        _proj_kernel,
        grid=(n // tm, PROJ_BLOCKS),
        in_specs=[pl.BlockSpec((tm, d), lambda i, j: (i, 0)),
                  pl.BlockSpec((1, 1, d), lambda i, j: (i // tiles_per_seq, 0, 0)),
                  pl.BlockSpec((1, 1, d), lambda i, j: (i // tiles_per_seq, 0, 1)),
                  pl.BlockSpec(memory_space=pltpu.VMEM),
                  pl.BlockSpec(memory_space=pltpu.VMEM),
                  pl.BlockSpec((tm, RET_QK_DIM), lambda i, j: (i % tiles_per_seq, 0)),
                  pl.BlockSpec((tm, RET_QK_DIM), lambda i, j: (i % tiles_per_seq, 0))],
        out_specs=pl.BlockSpec((tm, d), lambda i, j: (i, j)),
        out_shape=jax.ShapeDtypeStruct((n, PROJ_BLOCKS * d), BF16),
        scratch_shapes=[pltpu.VMEM((tm, d), BF16)],
        compiler_params=_params("arbitrary", "arbitrary"),
        name="proj",
    )(x2, mod3, mod3, w_in3, b_in3, cos, sin)


RET_UNROLL = 4


def _ret_kernel(q_ref, k_ref, v_ref, g_ref, s0f_ref, s0b_ref, df_ref, db_ref, out_ref,
                sf_all, sb_all, p_scr, o_scr):
    c_ = RET_CHUNK
    n_chunks = q_ref.shape[1] // c_
    dn_t = (((0,), (0,)), ((), ()))
    lgf = _log_sigmoid(df_ref[0])
    lgb = _log_sigmoid(db_ref[0])
    ri = lax.broadcasted_iota(I32, (c_, c_), 0)
    ci = lax.broadcasted_iota(I32, (c_, c_), 1)
    diff = (ri - ci).astype(F32)
    mask = jnp.where(diff > 0, jnp.exp(diff * lgf), jnp.where(diff < 0, jnp.exp(-diff * lgb), 2.0))
    pos = lax.broadcasted_iota(I32, (c_, 1), 0).astype(F32)
    qdec_f = jnp.exp((pos + 1.0) * lgf)
    qdec_b = jnp.exp((c_ - pos) * lgb)
    kdec_f = jnp.exp((c_ - 1.0 - pos) * lgf)
    kdec_b = jnp.exp(pos * lgb)
    cdec_f = jnp.exp(c_ * lgf)
    cdec_b = jnp.exp(c_ * lgb)

    def chunk(c):
        return pl.ds(pl.multiple_of(c * c_, c_), c_)

    def products(c, carry):
        k = k_ref[0, chunk(c), :].astype(F32)
        v = v_ref[0, chunk(c), :]
        sf_all[c] = lax.dot_general((k * kdec_f).astype(BF16), v, dn_t, preferred_element_type=F32)
        sb_all[c] = lax.dot_general((k * kdec_b).astype(BF16), v, dn_t, preferred_element_type=F32)
        return carry

    lax.fori_loop(0, n_chunks, products, 0, unroll=RET_UNROLL)

    def scan_f(c, s):
        kv = sf_all[c]
        sf_all[c] = s
        return s * cdec_f + kv

    def scan_b(t, s):
        c = n_chunks - 1 - t
        kv = sb_all[c]
        sb_all[c] = s
        return s * cdec_b + kv

    lax.fori_loop(0, n_chunks, scan_f, s0f_ref[0, 0])
    lax.fori_loop(0, n_chunks, scan_b, s0b_ref[0, 0])

    def scores(c, carry):
        s = lax.dot_general(q_ref[0, chunk(c), :], k_ref[0, chunk(c), :], (((1,), (1,)), ((), ())),
                            preferred_element_type=F32)
        p_scr[c] = (s * mask).astype(BF16)
        return carry

    lax.fori_loop(0, n_chunks, scores, 0, unroll=RET_UNROLL)

    def outputs(c, carry):
        q = q_ref[0, chunk(c), :]
        o = jnp.dot(p_scr[c], v_ref[0, chunk(c), :], preferred_element_type=F32)
        o = o + qdec_f * jnp.dot(q, sf_all[c].astype(BF16), preferred_element_type=F32)
        o = o + qdec_b * jnp.dot(q, sb_all[c].astype(BF16), preferred_element_type=F32)
        o_scr[chunk(c), :] = o
        return carry

    lax.fori_loop(0, n_chunks, outputs, 0, unroll=RET_UNROLL)

    def norm(c, carry):
        o = o_scr[chunk(c), :]
        mu = jnp.mean(o, axis=-1, keepdims=True)
        d = o - mu
        var = jnp.mean(d * d, axis=-1, keepdims=True)
        on = d * lax.rsqrt(var + LN_EPS)
        out_ref[0, chunk(c), :] = (on * g_ref[0, chunk(c), :].astype(F32)).astype(BF16)
        return carry

    lax.fori_loop(0, n_chunks, norm, 0, unroll=RET_UNROLL)


def _retention(proj3, s0f, s0b, dec_f, dec_b, d):
    b_, seq, _ = proj3.shape
    qb, kb = PB_Q * d // RET_QK_DIM, PB_K * d // RET_QK_DIM
    vb, gb = PB_V * d // RET_V_DIM, PB_G * d // RET_V_DIM
    n_chunks = seq // RET_CHUNK
    st_spec = pl.BlockSpec((1, 1, RET_QK_DIM, RET_V_DIM), lambda b, h: (b, h, 0, 0))
    st_all = pltpu.VMEM((n_chunks, RET_QK_DIM, RET_V_DIM), F32)
    return pl.pallas_call(
        _ret_kernel,
        grid=(b_, RET_HEADS),
        in_specs=[pl.BlockSpec((1, seq, RET_QK_DIM), lambda b, h: (b, 0, qb + h)),
                  pl.BlockSpec((1, seq, RET_QK_DIM), lambda b, h: (b, 0, kb + h)),
                  pl.BlockSpec((1, seq, RET_V_DIM), lambda b, h: (b, 0, vb + h)),
                  pl.BlockSpec((1, seq, RET_V_DIM), lambda b, h: (b, 0, gb + h)),
                  st_spec, st_spec,
                  pl.BlockSpec((1, 1, 1), lambda b, h: (h, 0, 0)),
                  pl.BlockSpec((1, 1, 1), lambda b, h: (h, 0, 0))],
        out_specs=pl.BlockSpec((1, seq, RET_V_DIM), lambda b, h: (b, 0, h)),
        out_shape=jax.ShapeDtypeStruct((b_, seq, RET_HEADS * RET_V_DIM), BF16),
        scratch_shapes=[st_all, st_all,
                        pltpu.VMEM((n_chunks, RET_CHUNK, RET_CHUNK), BF16),
                        pltpu.VMEM((seq, RET_V_DIM), F32)],
        compiler_params=_params("arbitrary", "arbitrary"),
        name="ret",
    )(proj3, proj3, proj3, proj3, s0f, s0b, dec_f, dec_b)


VEC_CONV_B, VEC_CLN_G, VEC_CLN_B, VEC_BCONV, VEC_BMIX, VEC_LN1_G, VEC_LN1_B = range(7)
CONV_ROWS = 64
LANES = 128
RT_E1, RT_E2, RT_RANK1, RT_RANK2 = 0, 1, 2, 3
RT_W1, RT_W2 = 0, 1


def _split_bf16(v):
    hi = v.astype(BF16)
    return hi, (v - hi.astype(F32)).astype(BF16)


def _merge_kernel(ret_ref, z_ref, zp_ref, zn_ref, ga_ref, gb_ref, x_ref, gt_ref, shf_ref, scf_ref,
                  wret_ref, wconv_ref, wmix_ref, dw_ref, vec_ref, wrh_ref, wrl_ref, br_ref,
                  h1_ref, t_ref, ri_ref, rw_ref, cnt_ref, zext, zc_scr, cnt_scr, *, tiles_per_seq):
    tm, d = x_ref.shape
    step = pl.program_id(0)
    ti = step % tiles_per_seq
    vec = vec_ref[...]

    def row(r):
        return vec[r:r + 1, :]

    y_a = jnp.dot(ret_ref[...], wret_ref[...], preferred_element_type=F32)

    has_prev = (ti > 0).astype(F32)
    has_next = (ti < tiles_per_seq - 1).astype(F32)
    zext[0:CONV_HALO, :] = zp_ref[...].astype(F32) * has_prev
    zext[CONV_HALO:CONV_HALO + tm, :] = z_ref[...].astype(F32)
    zext[CONV_HALO + tm:, :] = zn_ref[...].astype(F32) * has_next
    base = CONV_HALO - CONV_WIDTH // 2
    win_rows = CONV_ROWS + 2 * CONV_HALO

    def conv_rows(r, carry):
        r0 = pl.multiple_of(r * CONV_ROWS, CONV_ROWS)
        for lt in range(d // LANES):
            ls = slice(lt * LANES, (lt + 1) * LANES)
            win = zext[pl.ds(r0, win_rows), ls]
            acc = jnp.zeros((CONV_ROWS, LANES), F32) + vec_ref[VEC_CONV_B:VEC_CONV_B + 1, ls]
            for s in range(SUBLANES):
                taps = [w for w in range(CONV_WIDTH) if (base + w) % SUBLANES == s]
                if not taps:
                    continue
                sh = win if s == 0 else pltpu.roll(win, win_rows - s, 0)
                for w in taps:
                    a = (base + w) - s
                    acc = acc + sh[a:a + CONV_ROWS, :] * dw_ref[w:w + 1, ls]
            zc_scr[pl.ds(r0, CONV_ROWS), ls] = acc
        return carry

    lax.fori_loop(0, tm // CONV_ROWS, conv_rows, 0)
    zc = _ln_rows(zc_scr[...], row(VEC_CLN_G), row(VEC_CLN_B))
    zc = zc * _sigmoid(zc)
    y_b = jnp.dot(zc.astype(BF16), wconv_ref[...], preferred_element_type=F32) + row(VEC_BCONV)

    mixed = ga_ref[...].astype(F32) * y_a + gb_ref[...].astype(F32) * y_b
    mix = jnp.dot(mixed.astype(BF16), wmix_ref[...], preferred_element_type=F32) + row(VEC_BMIX)
    h1 = _ln_rows(ALPHA * x_ref[...] + gt_ref[0] * mix, row(VEC_LN1_G), row(VEC_LN1_B))
    h1_ref[...] = h1
    t = h1 * (1.0 + scf_ref[0]) + shf_ref[0]
    t_ref[...] = t

    t_hi, t_lo = _split_bf16(t)
    logits = (jnp.dot(t_hi, wrh_ref[...], preferred_element_type=F32)
              + jnp.dot(t_lo, wrh_ref[...], preferred_element_type=F32)
              + jnp.dot(t_hi, wrl_ref[...], preferred_element_type=F32)) + br_ref[...]

    lane = lax.broadcasted_iota(I32, logits.shape, 1).astype(F32)
    neg = -jnp.inf
    big = float(ROUTER_LANES)
    is_grp = lane < N_GROUPS
    gl = jnp.where(is_grp, logits, neg)
    gmax = jnp.max(gl, axis=-1, keepdims=True)
    gidx = jnp.min(jnp.where(gl == gmax, lane, big), axis=-1, keepdims=True)
    gsum = jnp.sum(jnp.where(is_grp, jnp.exp(gl - gmax), 0.0), axis=-1, keepdims=True)
    grp_w = 1.0 / gsum
    lo = N_GROUPS + gidx * EXPERTS_PER_GROUP
    el = jnp.where(lane >= lo, jnp.where(lane < lo + EXPERTS_PER_GROUP, logits, neg), neg)
    m1 = jnp.max(el, axis=-1, keepdims=True)
    i1 = jnp.min(jnp.where(el == m1, lane, big), axis=-1, keepdims=True)
    el2 = jnp.where(lane == i1, neg, el)
    m2 = jnp.max(el2, axis=-1, keepdims=True)
    i2 = jnp.min(jnp.where(el2 == m2, lane, big), axis=-1, keepdims=True)
    r = jnp.exp(m2 - m1)
    w1 = grp_w / (1.0 + r)
    w2 = grp_w * r / (1.0 + r)

    @pl.when(step == 0)
    def _():
        cnt_scr[...] = jnp.zeros_like(cnt_scr)

    oh1 = lane == i1
    oh2 = lane == i2
    oh = jnp.where(oh1, 1.0, jnp.where(oh2, 1.0, 0.0))
    tri = jnp.where(lax.broadcasted_iota(I32, (tm, tm), 0) > lax.broadcasted_iota(I32, (tm, tm), 1), 1.0, 0.0)
    before = jnp.dot(tri.astype(BF16), oh.astype(BF16), preferred_element_type=F32) + cnt_scr[0:1, :]
    rank1 = jnp.sum(jnp.where(oh1, before, 0.0), axis=-1, keepdims=True)
    rank2 = jnp.sum(jnp.where(oh2, before, 0.0), axis=-1, keepdims=True)
    cnt = cnt_scr[0:1, :] + jnp.sum(oh, axis=0, keepdims=True)
    cnt_scr[...] = jnp.broadcast_to(cnt, cnt_scr.shape)
    cnt_ref[...] = jnp.broadcast_to(cnt, cnt_ref.shape)

    e1 = i1 - N_GROUPS
    e2 = i2 - N_GROUPS
    ri = jnp.where(lane == RT_E1, e1, jnp.where(lane == RT_E2, e2,
                   jnp.where(lane == RT_RANK1, rank1, jnp.where(lane == RT_RANK2, rank2, 0.0))))
    ri_ref[...] = ri.astype(I32)
    rw_ref[...] = jnp.where(lane == RT_W1, w1, jnp.where(lane == RT_W2, w2, 0.0))


def _merge(ret2, proj2, x2, mod3, wret, wconv, wmix, dw, vec, wrh, wrl, br, seq, tm):
    n, d = x2.shape
    tiles_per_seq = seq // tm
    hb = tm // CONV_HALO
    last_hb = n // CONV_HALO - 1
    const = lambda i: (0, 0)
    bat = lambda k: (lambda i: (i // tiles_per_seq, 0, k))
    return pl.pallas_call(
        functools.partial(_merge_kernel, tiles_per_seq=tiles_per_seq),
        grid=(n // tm,),
        in_specs=[pl.BlockSpec((tm, ret2.shape[1]), lambda i: (i, 0)),
                  pl.BlockSpec((tm, d), lambda i: (i, PB_Z)),
                  pl.BlockSpec((CONV_HALO, d), lambda i: (jnp.maximum(i * hb - 1, 0), PB_Z)),
                  pl.BlockSpec((CONV_HALO, d), lambda i: (jnp.minimum((i + 1) * hb, last_hb), PB_Z)),
                  pl.BlockSpec((tm, d), lambda i: (i, PB_GATE)),
                  pl.BlockSpec((tm, d), lambda i: (i, PB_GATE + 1)),
                  pl.BlockSpec((tm, d), lambda i: (i, 0)),
                  pl.BlockSpec((1, 1, d), bat(2)),
                  pl.BlockSpec((1, 1, d), bat(3)),
                  pl.BlockSpec((1, 1, d), bat(4)),
                  pl.BlockSpec(wret.shape, const),
                  pl.BlockSpec(wconv.shape, const),
                  pl.BlockSpec(wmix.shape, const),
                  pl.BlockSpec(dw.shape, const),
                  pl.BlockSpec(vec.shape, const),
                  pl.BlockSpec(wrh.shape, const),
                  pl.BlockSpec(wrl.shape, const),
                  pl.BlockSpec(br.shape, const)],
        out_specs=[pl.BlockSpec((tm, d), lambda i: (i, 0)),
                   pl.BlockSpec((tm, d), lambda i: (i, 0)),
                   pl.BlockSpec((tm, ROUTER_LANES), lambda i: (i, 0)),
                   pl.BlockSpec((tm, ROUTER_LANES), lambda i: (i, 0)),
                   pl.BlockSpec((SUBLANES, ROUTER_LANES), const)],
        out_shape=[jax.ShapeDtypeStruct((n, d), F32),
                   jax.ShapeDtypeStruct((n, d), F32),
                   jax.ShapeDtypeStruct((n, ROUTER_LANES), I32),
                   jax.ShapeDtypeStruct((n, ROUTER_LANES), F32),
                   jax.ShapeDtypeStruct((SUBLANES, ROUTER_LANES), F32)],
        scratch_shapes=[pltpu.VMEM((tm + 2 * CONV_HALO, d), F32),
                        pltpu.VMEM((tm, d), F32),
                        pltpu.VMEM((SUBLANES, ROUTER_LANES), F32)],
        compiler_params=_params("arbitrary"),
        name="merge",
    )(ret2, proj2, proj2, proj2, proj2, proj2, x2, mod3, mod3, mod3,
      wret, wconv, wmix, dw, vec, wrh, wrl, br)


DISPATCH_TILE = 512


def _dispatch_kernel(pos_ref, t_ref, xs_hbm, sem):
    tm = t_ref.shape[0]
    base = pl.program_id(0) * tm

    for r in range(tm):
        for k in range(EXPERT_TOP_K):
            dst = pos_ref[(base + r) * EXPERT_TOP_K + k]
            pltpu.make_async_copy(t_ref.at[pl.ds(r, 1)], xs_hbm.at[pl.ds(dst, 1)], sem).start()
    for k in range(EXPERT_TOP_K):
        pltpu.make_async_copy(t_ref, xs_hbm.at[pl.ds(0, tm)], sem).wait()


def _dispatch(pos_flat, t2):
    n, d = t2.shape
    tm = min(DISPATCH_TILE, n)
    return pl.pallas_call(
        _dispatch_kernel,
        grid=(n // tm,),
        in_specs=[pl.BlockSpec(memory_space=pltpu.SMEM),
                  pl.BlockSpec((tm, d), lambda i: (i, 0))],
        out_specs=pl.BlockSpec(memory_space=pl.ANY),
        out_shape=jax.ShapeDtypeStruct((n * EXPERT_TOP_K, d), F32),
        scratch_shapes=[pltpu.SemaphoreType.DMA],
        compiler_params=_params("arbitrary"),
        name="dispatch",
    )(pos_flat, t2)


EXPERT_TILE = 512


def _experts_kernel(ie_ref, it_ref, lo_ref, hi_ref, nv_ref, xs_ref, wg_ref, wu_ref, wd_ref, ys_ref,
                    wg_bf, wu_bf, wd_bf):
    k = pl.program_id(0)
    prev = jnp.maximum(k - 1, 0)
    new_expert = (k == 0) | (ie_ref[k] != ie_ref[prev])
    new_tile = (k == 0) | (it_ref[k] != it_ref[prev])

    @pl.when(new_expert)
    def _():
        wg_bf[...] = wg_ref[0, 0].astype(BF16)
        wu_bf[...] = wu_ref[0, 0].astype(BF16)
        wd_bf[...] = wd_ref[0, 0].astype(BF16)

    @pl.when(k < nv_ref[0])
    def _():
        x = xs_ref[...].astype(BF16)
        g = jnp.dot(x, wg_bf[...], preferred_element_type=F32)
        u = jnp.dot(x, wu_bf[...], preferred_element_type=F32)
        hid = (g * _sigmoid(g) * u).astype(BF16)
        y = jnp.dot(hid, wd_bf[...], preferred_element_type=F32)
        rows = lax.broadcasted_iota(I32, (xs_ref.shape[0], 1), 0)
        mine = (rows >= lo_ref[k]) & (rows < hi_ref[k])

        @pl.when(new_tile)
        def _():
            ys_ref[...] = jnp.where(mine, y, 0.0)

        @pl.when(jnp.logical_not(new_tile))
        def _():
            ys_ref[...] = jnp.where(mine, y, ys_ref[...])


def _experts(meta, xs, w_gate, w_up, w_down):
    p, d = xs.shape
    ff = w_gate.shape[-1]
    n_items = meta[0].shape[0]
    grid_spec = pltpu.PrefetchScalarGridSpec(
        num_scalar_prefetch=5,
        grid=(n_items,),
        in_specs=[pl.BlockSpec((EXPERT_TILE, d), lambda k, ie, it, lo, hi, nv: (it[k], 0)),
                  pl.BlockSpec((1, 1, d, ff), lambda k, ie, it, lo, hi, nv: (0, ie[k], 0, 0)),
                  pl.BlockSpec((1, 1, d, ff), lambda k, ie, it, lo, hi, nv: (0, ie[k], 0, 0)),
                  pl.BlockSpec((1, 1, ff, d), lambda k, ie, it, lo, hi, nv: (0, ie[k], 0, 0))],
        out_specs=pl.BlockSpec((EXPERT_TILE, d), lambda k, ie, it, lo, hi, nv: (it[k], 0)),
        scratch_shapes=[pltpu.VMEM((d, ff), BF16), pltpu.VMEM((d, ff), BF16), pltpu.VMEM((ff, d), BF16)])
    return pl.pallas_call(
        _experts_kernel,
        grid_spec=grid_spec,
        out_shape=jax.ShapeDtypeStruct((p, d), F32),
        compiler_params=_params("arbitrary"),
        name="experts",
    )(*meta, xs, w_gate, w_up, w_down)


def _expert_work_items(counts, p):
    n_tiles = p // EXPERT_TILE
    n_items = n_tiles + N_EXPERTS - 1
    offs = jnp.concatenate([jnp.zeros((1,), I32), jnp.cumsum(counts)])
    first = offs[:-1] // EXPERT_TILE
    last = (offs[1:] - 1) // EXPERT_TILE
    per = jnp.where(counts > 0, last - first + 1, 0)
    ends = jnp.cumsum(per)
    starts = ends - per
    n_valid = ends[-1]
    k = jnp.arange(n_items, dtype=I32)
    kk = jnp.minimum(k, n_valid - 1)
    ie = jnp.sum((ends[None, :] <= kk[:, None]).astype(I32), axis=1)
    it = first[ie] + (kk - starts[ie])
    lo = jnp.maximum(offs[ie], it * EXPERT_TILE) - it * EXPERT_TILE
    hi = jnp.minimum(offs[ie + 1], (it + 1) * EXPERT_TILE) - it * EXPERT_TILE
    valid = k < n_valid
    lo = jnp.where(valid, lo, 0)
    hi = jnp.where(valid, hi, 0)
    return offs, (ie.astype(I32), it.astype(I32), lo.astype(I32), hi.astype(I32), n_valid.reshape(1).astype(I32))


COMBINE_HALF = 256


def _combine_kernel(pos_ref, ys_hbm, rw_ref, h1_ref, gt_ref, ln_ref, out_ref, buf, sem):
    i = pl.program_id(0)
    n_steps = pl.num_programs(0)
    hm = COMBINE_HALF
    base = i * (2 * hm)

    def slot_copy(s, k):
        return pltpu.make_async_copy(ys_hbm.at[pl.ds(0, hm)], buf.at[s, k], sem.at[s])

    def gather(first_tok, s):
        for r in range(hm):
            for k in range(EXPERT_TOP_K):
                src = pos_ref[(first_tok + r) * EXPERT_TOP_K + k]
                pltpu.make_async_copy(ys_hbm.at[pl.ds(src, 1)], buf.at[s, k, pl.ds(r, 1)], sem.at[s]).start()

    def finish(s):
        for k in range(EXPERT_TOP_K):
            slot_copy(s, k).wait()
        rows = slice(s * hm, (s + 1) * hm)
        rw = rw_ref[rows, :]
        y = rw[:, RT_W1:RT_W1 + 1] * buf[s, 0] + rw[:, RT_W2:RT_W2 + 1] * buf[s, 1]
        out_ref[rows, :] = _ln_rows(ALPHA * h1_ref[rows, :] + gt_ref[0] * y, ln_ref[0:1, :], ln_ref[1:2, :])

    @pl.when(i == 0)
    def _():
        gather(0, 0)

    gather(base + hm, 1)
    finish(0)

    @pl.when(i + 1 < n_steps)
    def _():
        gather(base + 2 * hm, 0)

    finish(1)


def _combine(pos_flat, ys, rw, h1, mod3, ln2, seq):
    n, d = h1.shape
    tm = 2 * COMBINE_HALF
    tiles_per_seq = seq // tm
    return pl.pallas_call(
        _combine_kernel,
        grid=(n // tm,),
        in_specs=[pl.BlockSpec(memory_space=pltpu.SMEM),
                  pl.BlockSpec(memory_space=pl.ANY),
                  pl.BlockSpec((tm, ROUTER_LANES), lambda i: (i, 0)),
                  pl.BlockSpec((tm, d), lambda i: (i, 0)),
                  pl.BlockSpec((1, 1, d), lambda i: (i // tiles_per_seq, 0, 5)),
                  pl.BlockSpec(ln2.shape, lambda i: (0, 0))],
        out_specs=pl.BlockSpec((tm, d), lambda i: (i, 0)),
        out_shape=jax.ShapeDtypeStruct((n, d), F32),
        scratch_shapes=[pltpu.VMEM((2, EXPERT_TOP_K, COMBINE_HALF, d), F32),
                        pltpu.SemaphoreType.DMA((2,))],
        compiler_params=_params("arbitrary"),
        name="combine",
    )(pos_flat, ys, rw, h1, mod3, ln2)


def _rope_tables(seq):
    pos = np.arange(seq)
    r = (pos // GRID_W).astype(np.float32)
    col = (pos % GRID_W).astype(np.float32)
    n_freq = RET_QK_DIM // 4
    inv = (np.float32(ROPE_BASE) ** (-np.arange(n_freq, dtype=np.float32) / np.float32(n_freq))).astype(np.float32)
    ang = np.concatenate([r[:, None] * inv, col[:, None] * inv], axis=-1)
    ang = np.concatenate([ang, ang], axis=-1).astype(np.float32)
    sign = np.concatenate([-np.ones((RET_QK_DIM // 2,), np.float32), np.ones((RET_QK_DIM // 2,), np.float32)])
    return jnp.asarray(np.cos(ang), F32), jnp.asarray(np.sin(ang) * sign, F32)


def kernel(x, c, ctx, c_ctx, w_ada, b_ada, w_in, b_in, ret_decay_fwd, ret_decay_bwd, w_ret_out, conv_dw, conv_dw_b, conv_ln_g, conv_ln_b, w_conv_out, b_conv_out, w_mix_out, b_mix_out, ln1_g, ln1_b, w_router_grp, b_router_grp, w_router_exp, b_router_exp, w_exp_gate, w_exp_up, w_exp_down, ln2_g, ln2_b):
    b_, seq, d = x.shape
    n = b_ * seq
    assert w_ada.shape[0] == DEPTH
    mod_rows = SUBLANES
    assert b_ + 1 <= mod_rows

    cs = jnp.concatenate([c, c_ctx[None, :], jnp.zeros((mod_rows - b_ - 1, d), F32)], axis=0)
    mod = _ada(cs, w_ada[0], b_ada[0][None, :])
    mod3 = mod.reshape(mod_rows, 1, 6 * d)

    in_blocks = w_in.shape[-1] // d
    w_in3 = w_in[0].astype(BF16).reshape(d, in_blocks, d).transpose(1, 0, 2)
    b_in3 = b_in[0].reshape(in_blocks, 1, d)
    dec_f = ret_decay_fwd[0].reshape(RET_HEADS, 1, 1)
    dec_b = ret_decay_bwd[0].reshape(RET_HEADS, 1, 1)

    s0f, s0b = _ctx_states(ctx, mod3, w_in3, b_in3, dec_f, dec_b, b_)

    cos, sin = _rope_tables(seq)
    x2 = x.reshape(n, d)
    proj = _proj(x2, mod3, w_in3, b_in3, cos, sin, seq, tm=min(1024, seq))

    ret = _retention(proj.reshape(b_, seq, PROJ_BLOCKS * d), s0f, s0b, dec_f, dec_b, d)

    vec = jnp.concatenate([conv_dw_b, conv_ln_g, conv_ln_b, b_conv_out, b_mix_out, ln1_g, ln1_b,
                           jnp.zeros((1, d), F32)], axis=0)
    dw = jnp.concatenate([conv_dw[0], jnp.zeros((32 - CONV_WIDTH, d), F32)], axis=0)
    pad = ROUTER_LANES - N_GROUPS - N_EXPERTS
    wr = jnp.concatenate([w_router_grp[0], w_router_exp[0], jnp.zeros((d, pad), F32)], axis=1)
    wr_hi = wr.astype(BF16)
    wr_lo = (wr - wr_hi.astype(F32)).astype(BF16)
    br = jnp.concatenate([b_router_grp[0], b_router_exp[0], jnp.zeros((pad,), F32)])[None, :]
    h1, t, route_i, route_w, cnt = _merge(
        ret.reshape(n, RET_HEADS * RET_V_DIM), proj, x2, mod3,
        w_ret_out[0].astype(BF16), w_conv_out[0].astype(BF16), w_mix_out[0].astype(BF16),
        dw, vec, wr_hi, wr_lo, br, seq, tm=min(512, seq))

    counts = cnt[0, N_GROUPS:N_GROUPS + N_EXPERTS].astype(I32)
    offs, meta = _expert_work_items(counts, n * EXPERT_TOP_K)
    eid = route_i[:, RT_E1:RT_E2 + 1]
    seg_start = jnp.sum(jnp.where(eid[:, :, None] == jnp.arange(N_EXPERTS, dtype=I32), offs[:N_EXPERTS], 0), axis=-1)
    pos = seg_start + route_i[:, RT_RANK1:RT_RANK2 + 1]
    pos_flat = pos.reshape(n * EXPERT_TOP_K)

    xs = _dispatch(pos_flat, t)
    ys = _experts(meta, xs, w_exp_gate, w_exp_up, w_exp_down)
    ln2 = jnp.concatenate([ln2_g, ln2_b, jnp.zeros((SUBLANES - 2, d), F32)], axis=0)
    out = _combine(pos_flat, ys, route_w, h1, mod3, ln2, seq)
    return out.reshape(b_, seq, d)
```

```python
import functools

import jax
import jax.numpy as jnp
import numpy as np
from jax import lax
from jax.experimental import pallas as pl
from jax.experimental.pallas import tpu as pltpu

GRID_W = 64
RET_HEADS = 8
RET_QK_DIM = 128
RET_V_DIM = 256
RET_CHUNK = 256
ROPE_BASE = 10000.0
CONV_WIDTH = 31
CONV_HALO = 16
SUBLANES = 8
N_GROUPS = 4
EXPERTS_PER_GROUP = 8
N_EXPERTS = N_GROUPS * EXPERTS_PER_GROUP
EXPERT_TOP_K = 2
LN_EPS = 1e-5
DEPTH = 1
ALPHA = (2.0 * DEPTH) ** 0.25
ROUTER_LANES = 128

V7X_VMEM_LIMIT = 56 * 1024 * 1024

F32 = jnp.float32
BF16 = jnp.bfloat16
I32 = jnp.int32
HIGHEST = lax.Precision.HIGHEST


def _params(*sem):
    return pltpu.CompilerParams(dimension_semantics=sem, vmem_limit_bytes=V7X_VMEM_LIMIT)


def _sigmoid(v):
    return 1.0 / (1.0 + jnp.exp(-v))


def _ln_rows(v, g, b):
    mu = jnp.mean(v, axis=-1, keepdims=True)
    d = v - mu
    var = jnp.mean(d * d, axis=-1, keepdims=True)
    return d * lax.rsqrt(var + LN_EPS) * g + b


def _ada_kernel(cs_ref, w_ref, b_ref, out_ref):
    s = cs_ref[...]
    s = s * _sigmoid(s)
    out_ref[...] = jnp.dot(s, w_ref[...], preferred_element_type=F32, precision=HIGHEST) + b_ref[...]


def _ada(cs, w_ada, b_ada):
    rows, d = cs.shape
    cols = w_ada.shape[1]
    tn = 1024
    return pl.pallas_call(
        _ada_kernel,
        grid=(cols // tn,),
        in_specs=[pl.BlockSpec((rows, d), lambda j: (0, 0)),
                  pl.BlockSpec((d, tn), lambda j: (0, j)),
                  pl.BlockSpec((1, tn), lambda j: (0, j))],
        out_specs=pl.BlockSpec((rows, tn), lambda j: (0, j)),
        out_shape=jax.ShapeDtypeStruct((rows, cols), F32),
        compiler_params=_params("arbitrary"),
        name="ada",
    )(cs, w_ada, b_ada)


def _log_sigmoid(v):
    return jnp.minimum(v, 0.0) - jnp.log(1.0 + jnp.exp(-jnp.abs(v)))


def _ctx_kernel(ctx_ref, sh_ref, sc_ref, wk_ref, wv_ref, bk_ref, bv_ref, df_ref, db_ref, sf_ref, sb_ref):
    lc = ctx_ref.shape[1]
    u = (ctx_ref[0] * (1.0 + sc_ref[0]) + sh_ref[0]).astype(BF16)
    k = (jnp.dot(u, wk_ref[0], preferred_element_type=F32) + bk_ref[0]) * RET_QK_DIM ** -0.5
    v = (jnp.dot(u, wv_ref[0], preferred_element_type=F32) + bv_ref[0]).astype(BF16)
    lgf = _log_sigmoid(df_ref[0])
    lgb = _log_sigmoid(db_ref[0])
    pos = lax.broadcasted_iota(I32, (lc, 1), 0).astype(F32)
    kf = (k * jnp.exp((lc - 1.0 - pos) * lgf)).astype(BF16)
    kb = (k * jnp.exp(pos * lgb)).astype(BF16)
    dn = (((0,), (0,)), ((), ()))
    sf_ref[0, 0] = lax.dot_general(kf, v, dn, preferred_element_type=F32)
    sb_ref[0, 0] = lax.dot_general(kb, v, dn, preferred_element_type=F32)


def _ctx_states(ctx, mod3, w_in3, b_in3, dec_f, dec_b, ctx_row):
    b_, lc, d = ctx.shape
    v_per_blk = d // RET_V_DIM

    def k_idx(b, h):
        return (PB_K, 0, h)

    def v_idx(b, h):
        return (PB_V + h // v_per_blk, 0, h % v_per_blk)

    st = jax.ShapeDtypeStruct((b_, RET_HEADS, RET_QK_DIM, RET_V_DIM), F32)
    st_spec = pl.BlockSpec((1, 1, RET_QK_DIM, RET_V_DIM), lambda b, h: (b, h, 0, 0))
    return pl.pallas_call(
        _ctx_kernel,
        grid=(b_, RET_HEADS),
        in_specs=[pl.BlockSpec((1, lc, d), lambda b, h: (b, 0, 0)),
                  pl.BlockSpec((1, 1, d), lambda b, h: (ctx_row, 0, 0)),
                  pl.BlockSpec((1, 1, d), lambda b, h: (ctx_row, 0, 1)),
                  pl.BlockSpec((1, d, RET_QK_DIM), k_idx),
                  pl.BlockSpec((1, d, RET_V_DIM), v_idx),
                  pl.BlockSpec((1, 1, RET_QK_DIM), k_idx),
                  pl.BlockSpec((1, 1, RET_V_DIM), v_idx),
                  pl.BlockSpec((1, 1, 1), lambda b, h: (h, 0, 0)),
                  pl.BlockSpec((1, 1, 1), lambda b, h: (h, 0, 0))],
        out_specs=[st_spec, st_spec],
        out_shape=[st, st],
        compiler_params=_params("arbitrary", "arbitrary"),
        name="ctx",
    )(ctx, mod3, mod3, w_in3, w_in3, b_in3, b_in3, dec_f, dec_b)


PROJ_BLOCKS = 9
PB_Q, PB_K, PB_V, PB_G, PB_Z, PB_GATE = 0, 1, 2, 4, 6, 7


def _proj_kernel(x_ref, sh_ref, sc_ref, w_ref, b_ref, cos_ref, sin_ref, out_ref, u_scr):
    j = pl.program_id(1)

    @pl.when(j == 0)
    def _():
        u_scr[...] = (x_ref[...] * (1.0 + sc_ref[0]) + sh_ref[0]).astype(BF16)

    def mm(blk):
        return jnp.dot(u_scr[...], w_ref[blk], preferred_element_type=F32) + b_ref[blk]

    def rope(acc):
        cos = cos_ref[...]
        sin = sin_ref[...]
        for h in range(acc.shape[1] // RET_QK_DIM):
            sl = slice(h * RET_QK_DIM, (h + 1) * RET_QK_DIM)
            seg = acc[:, sl]
            out_ref[:, sl] = (seg * cos + pltpu.roll(seg, RET_QK_DIM // 2, 1) * sin).astype(BF16)

    @pl.when(j == PB_Q)
    def _():
        rope(mm(PB_Q))

    @pl.when(j == PB_K)
    def _():
        rope(mm(PB_K) * RET_QK_DIM ** -0.5)

    @pl.when((j >= PB_V) & (j < PB_G))
    def _():
        out_ref[...] = mm(j).astype(BF16)

    @pl.when((j >= PB_G) & (j < PB_Z))
    def _():
        acc = mm(j)
        out_ref[...] = (acc * _sigmoid(acc)).astype(BF16)

    @pl.when(j == PB_Z)
    def _():
        out_ref[...] = (mm(PB_Z) * _sigmoid(mm(PB_Z + 1))).astype(BF16)

    @pl.when(j >= PB_GATE)
    def _():
        out_ref[...] = _sigmoid(mm(j + 1)).astype(BF16)


def _proj(x2, mod3, w_in3, b_in3, cos, sin, seq, tm):
    n, d = x2.shape
    tiles_per_seq = seq // tm
    return pl.pallas_call(
        _proj_kernel,
        grid=(n // tm, PROJ_BLOCKS),
        in_specs=[pl.BlockSpec((tm, d), lambda i, j: (i, 0)),
                  pl.BlockSpec((1, 1, d), lambda i, j: (i // tiles_per_seq, 0, 0)),
                  pl.BlockSpec((1, 1, d), lambda i, j: (i // tiles_per_seq, 0, 1)),
                  pl.BlockSpec(memory_space=pltpu.VMEM),
                  pl.BlockSpec(memory_space=pltpu.VMEM),
                  pl.BlockSpec((tm, RET_QK_DIM), lambda i, j: (i % tiles_per_seq, 0)),
                  pl.BlockSpec((tm, RET_QK_DIM), lambda i, j: (i % tiles_per_seq, 0))],
        out_specs=pl.BlockSpec((tm, d), lambda i, j: (i, j)),
        out_shape=jax.ShapeDtypeStruct((n, PROJ_BLOCKS * d), BF16),
        scratch_shapes=[pltpu.VMEM((tm, d), BF16)],
        compiler_params=_params("arbitrary", "arbitrary"),
        name="proj",
    )(x2, mod3, mod3, w_in3, b_in3, cos, sin)


def _ret_kernel(q_ref, k_ref, v_ref, g_ref, s0f_ref, s0b_ref, df_ref, db_ref, out_ref,
                sf_all, sb_all, p_scr, o_scr):
    c_ = RET_CHUNK
    n_chunks = q_ref.shape[1] // c_
    dn_t = (((0,), (0,)), ((), ()))
    lgf = _log_sigmoid(df_ref[0])
    lgb = _log_sigmoid(db_ref[0])
    ri = lax.broadcasted_iota(I32, (c_, c_), 0)
    ci = lax.broadcasted_iota(I32, (c_, c_), 1)
    diff = (ri - ci).astype(F32)
    mask = jnp.where(diff > 0, jnp.exp(diff * lgf), jnp.where(diff < 0, jnp.exp(-diff * lgb), 2.0))
    pos = lax.broadcasted_iota(I32, (c_, 1), 0).astype(F32)
    qdec_f = jnp.exp((pos + 1.0) * lgf)
    qdec_b = jnp.exp((c_ - pos) * lgb)
    kdec_f = jnp.exp((c_ - 1.0 - pos) * lgf)
    kdec_b = jnp.exp(pos * lgb)
    cdec_f = jnp.exp(c_ * lgf)
    cdec_b = jnp.exp(c_ * lgb)

    def chunk(c):
        return slice(c * c_, (c + 1) * c_)

    for c in range(n_chunks):
        k = k_ref[0, chunk(c), :].astype(F32)
        v = v_ref[0, chunk(c), :]
        sf_all[c] = lax.dot_general((k * kdec_f).astype(BF16), v, dn_t, preferred_element_type=F32)
        sb_all[c] = lax.dot_general((k * kdec_b).astype(BF16), v, dn_t, preferred_element_type=F32)

    sf = s0f_ref[0, 0]
    sb = s0b_ref[0, 0]
    for t in range(n_chunks):
        kv = sf_all[t]
        sf_all[t] = sf
        sf = sf * cdec_f + kv
        c = n_chunks - 1 - t
        kv = sb_all[c]
        sb_all[c] = sb
        sb = sb * cdec_b + kv

    for c in range(n_chunks):
        s = lax.dot_general(q_ref[0, chunk(c), :], k_ref[0, chunk(c), :], (((1,), (1,)), ((), ())),
                            preferred_element_type=F32)
        p_scr[c] = (s * mask).astype(BF16)

    for c in range(n_chunks):
        q = q_ref[0, chunk(c), :]
        o = jnp.dot(p_scr[c], v_ref[0, chunk(c), :], preferred_element_type=F32)
        o = o + qdec_f * jnp.dot(q, sf_all[c].astype(BF16), preferred_element_type=F32)
        o = o + qdec_b * jnp.dot(q, sb_all[c].astype(BF16), preferred_element_type=F32)
        o_scr[chunk(c), :] = o

    for c in range(n_chunks):
        o = o_scr[chunk(c), :]
        mu = jnp.mean(o, axis=-1, keepdims=True)
        d = o - mu
        var = jnp.mean(d * d, axis=-1, keepdims=True)
        on = d * lax.rsqrt(var + LN_EPS)
        out_ref[0, chunk(c), :] = (on * g_ref[0, chunk(c), :].astype(F32)).astype(BF16)


def _retention(proj3, s0f, s0b, dec_f, dec_b, d):
    b_, seq, _ = proj3.shape
    qb, kb = PB_Q * d // RET_QK_DIM, PB_K * d // RET_QK_DIM
    vb, gb = PB_V * d // RET_V_DIM, PB_G * d // RET_V_DIM
    n_chunks = seq // RET_CHUNK
    st_spec = pl.BlockSpec((1, 1, RET_QK_DIM, RET_V_DIM), lambda b, h: (b, h, 0, 0))
    st_all = pltpu.VMEM((n_chunks, RET_QK_DIM, RET_V_DIM), F32)
    return pl.pallas_call(
        _ret_kernel,
        grid=(b_, RET_HEADS),
        in_specs=[pl.BlockSpec((1, seq, RET_QK_DIM), lambda b, h: (b, 0, qb + h)),
                  pl.BlockSpec((1, seq, RET_QK_DIM), lambda b, h: (b, 0, kb + h)),
                  pl.BlockSpec((1, seq, RET_V_DIM), lambda b, h: (b, 0, vb + h)),
                  pl.BlockSpec((1, seq, RET_V_DIM), lambda b, h: (b, 0, gb + h)),
                  st_spec, st_spec,
                  pl.BlockSpec((1, 1, 1), lambda b, h: (h, 0, 0)),
                  pl.BlockSpec((1, 1, 1), lambda b, h: (h, 0, 0))],
        out_specs=pl.BlockSpec((1, seq, RET_V_DIM), lambda b, h: (b, 0, h)),
        out_shape=jax.ShapeDtypeStruct((b_, seq, RET_HEADS * RET_V_DIM), BF16),
        scratch_shapes=[st_all, st_all,
                        pltpu.VMEM((n_chunks, RET_CHUNK, RET_CHUNK), BF16),
                        pltpu.VMEM((seq, RET_V_DIM), F32)],
        compiler_params=_params("arbitrary", "arbitrary"),
        name="ret",
    )(proj3, proj3, proj3, proj3, s0f, s0b, dec_f, dec_b)


VEC_CONV_B, VEC_CLN_G, VEC_CLN_B, VEC_BCONV, VEC_BMIX, VEC_LN1_G, VEC_LN1_B = range(7)
CONV_ROWS = 64
YA_BLOCKS = 4
LANES = 128
RT_E1, RT_E2, RT_RANK1, RT_RANK2 = 0, 1, 2, 3
RT_W1, RT_W2 = 0, 1


def _split_bf16(v):
    hi = v.astype(BF16)
    return hi, (v - hi.astype(F32)).astype(BF16)


def _merge_kernel(ret_ref, z_ref, zp_ref, zn_ref, ga_ref, gb_ref, x_ref, gt_ref, shf_ref, scf_ref,
                  wret_ref, wconv_ref, wmix_ref, dw_ref, vec_ref, wrh_ref, wrl_ref, br_ref,
                  h1_ref, t_ref, ri_ref, rw_ref, cnt_ref, zext, zc_scr, cnt_scr, ya_scr, *, tiles_per_seq):
    tm, d = x_ref.shape
    step = pl.program_id(0)
    ti = step % tiles_per_seq
    vec = vec_ref[...]

    def row(r):
        return vec[r:r + 1, :]


    has_prev = (ti > 0).astype(F32)
    has_next = (ti < tiles_per_seq - 1).astype(F32)
    zext[0:CONV_HALO, :] = zp_ref[...].astype(F32) * has_prev
    zext[CONV_HALO:CONV_HALO + tm, :] = z_ref[...].astype(F32)
    zext[CONV_HALO + tm:, :] = zn_ref[...].astype(F32) * has_next
    base = CONV_HALO - CONV_WIDTH // 2
    win_rows = CONV_ROWS + 2 * CONV_HALO

    n_lt = d // LANES
    kc = ret_ref.shape[1] // n_lt

    def conv_piece(r0, lt):
        ls = slice(lt * LANES, (lt + 1) * LANES)
        win = zext[pl.ds(r0, win_rows), ls]
        acc = jnp.zeros((CONV_ROWS, LANES), F32) + vec_ref[VEC_CONV_B:VEC_CONV_B + 1, ls]
        for s in range(SUBLANES):
            taps = [w for w in range(CONV_WIDTH) if (base + w) % SUBLANES == s]
            if not taps:
                continue
            sh = win if s == 0 else pltpu.roll(win, win_rows - s, 0)
            for w in taps:
                a = (base + w) - s
                acc = acc + sh[a:a + CONV_ROWS, :] * dw_ref[w:w + 1, ls]
        return acc

    def conv_rows(r, carry):
        r0 = pl.multiple_of(r * (2 * CONV_ROWS), 2 * CONV_ROWS)
        ya = None
        accs = []
        for lt in range(n_lt):
            part = jnp.dot(ret_ref[:, lt * kc:(lt + 1) * kc], wret_ref[r, lt * kc:(lt + 1) * kc, :],
                           preferred_element_type=F32)
            ya = part if ya is None else ya + part
            accs.append((conv_piece(r0, lt), conv_piece(r0 + CONV_ROWS, lt)))
        ya_scr[r] = ya
        for lt in range(n_lt):
            ls = slice(lt * LANES, (lt + 1) * LANES)
            zc_scr[pl.ds(r0, CONV_ROWS), ls] = accs[lt][0]
            zc_scr[pl.ds(r0 + CONV_ROWS, CONV_ROWS), ls] = accs[lt][1]
        return carry

    lax.fori_loop(0, YA_BLOCKS, conv_rows, 0)
    y_a = jnp.concatenate([ya_scr[q] for q in range(YA_BLOCKS)], axis=1)
    zc = _ln_rows(zc_scr[...], row(VEC_CLN_G), row(VEC_CLN_B))
    zc = zc * _sigmoid(zc)
    y_b = jnp.dot(zc.astype(BF16), wconv_ref[...], preferred_element_type=F32) + row(VEC_BCONV)

    mixed = ga_ref[...].astype(F32) * y_a + gb_ref[...].astype(F32) * y_b
    mix = jnp.dot(mixed.astype(BF16), wmix_ref[...], preferred_element_type=F32) + row(VEC_BMIX)
    h1 = _ln_rows(ALPHA * x_ref[...] + gt_ref[0] * mix, row(VEC_LN1_G), row(VEC_LN1_B))
    h1_ref[...] = h1
    t = h1 * (1.0 + scf_ref[0]) + shf_ref[0]
    t_ref[...] = t

    t_hi, t_lo = _split_bf16(t)
    logits = (jnp.dot(t_hi, wrh_ref[...], preferred_element_type=F32)
              + jnp.dot(t_lo, wrh_ref[...], preferred_element_type=F32)
              + jnp.dot(t_hi, wrl_ref[...], preferred_element_type=F32)) + br_ref[...]

    lane = lax.broadcasted_iota(I32, logits.shape, 1).astype(F32)
    neg = -jnp.inf
    big = float(ROUTER_LANES)
    is_grp = lane < N_GROUPS
    gl = jnp.where(is_grp, logits, neg)
    gmax = jnp.max(gl, axis=-1, keepdims=True)
    gidx = jnp.min(jnp.where(gl == gmax, lane, big), axis=-1, keepdims=True)
    gsum = jnp.sum(jnp.where(is_grp, jnp.exp(gl - gmax), 0.0), axis=-1, keepdims=True)
    grp_w = 1.0 / gsum
    lo = N_GROUPS + gidx * EXPERTS_PER_GROUP
    el = jnp.where(lane >= lo, jnp.where(lane < lo + EXPERTS_PER_GROUP, logits, neg), neg)
    m1 = jnp.max(el, axis=-1, keepdims=True)
    i1 = jnp.min(jnp.where(el == m1, lane, big), axis=-1, keepdims=True)
    el2 = jnp.where(lane == i1, neg, el)
    m2 = jnp.max(el2, axis=-1, keepdims=True)
    i2 = jnp.min(jnp.where(el2 == m2, lane, big), axis=-1, keepdims=True)
    r = jnp.exp(m2 - m1)
    w1 = grp_w / (1.0 + r)
    w2 = grp_w * r / (1.0 + r)

    @pl.when(step == 0)
    def _():
        cnt_scr[...] = jnp.zeros_like(cnt_scr)

    oh1 = lane == i1
    oh2 = lane == i2
    oh = jnp.where(oh1, 1.0, jnp.where(oh2, 1.0, 0.0))
    tri = jnp.where(lax.broadcasted_iota(I32, (tm, tm), 0) > lax.broadcasted_iota(I32, (tm, tm), 1), 1.0, 0.0)
    before = jnp.dot(tri.astype(BF16), oh.astype(BF16), preferred_element_type=F32) + cnt_scr[0:1, :]
    rank1 = jnp.sum(jnp.where(oh1, before, 0.0), axis=-1, keepdims=True)
    rank2 = jnp.sum(jnp.where(oh2, before, 0.0), axis=-1, keepdims=True)
    cnt = cnt_scr[0:1, :] + jnp.sum(oh, axis=0, keepdims=True)
    cnt_scr[...] = jnp.broadcast_to(cnt, cnt_scr.shape)
    cnt_ref[...] = jnp.broadcast_to(cnt, cnt_ref.shape)

    e1 = i1 - N_GROUPS
    e2 = i2 - N_GROUPS
    ri = jnp.where(lane == RT_E1, e1, jnp.where(lane == RT_E2, e2,
                   jnp.where(lane == RT_RANK1, rank1, jnp.where(lane == RT_RANK2, rank2, 0.0))))
    ri_ref[...] = ri.astype(I32)
    rw_ref[...] = jnp.where(lane == RT_W1, w1, jnp.where(lane == RT_W2, w2, 0.0))


def _merge(ret2, proj2, x2, mod3, wret, wconv, wmix, dw, vec, wrh, wrl, br, seq, tm):
    n, d = x2.shape
    assert tm == YA_BLOCKS * 2 * CONV_ROWS
    tiles_per_seq = seq // tm
    hb = tm // CONV_HALO
    last_hb = n // CONV_HALO - 1
    const = lambda i: (0, 0)
    bat = lambda k: (lambda i: (i // tiles_per_seq, 0, k))
    return pl.pallas_call(
        functools.partial(_merge_kernel, tiles_per_seq=tiles_per_seq),
        grid=(n // tm,),
        in_specs=[pl.BlockSpec((tm, ret2.shape[1]), lambda i: (i, 0)),
                  pl.BlockSpec((tm, d), lambda i: (i, PB_Z)),
                  pl.BlockSpec((CONV_HALO, d), lambda i: (jnp.maximum(i * hb - 1, 0), PB_Z)),
                  pl.BlockSpec((CONV_HALO, d), lambda i: (jnp.minimum((i + 1) * hb, last_hb), PB_Z)),
                  pl.BlockSpec((tm, d), lambda i: (i, PB_GATE)),
                  pl.BlockSpec((tm, d), lambda i: (i, PB_GATE + 1)),
                  pl.BlockSpec((tm, d), lambda i: (i, 0)),
                  pl.BlockSpec((1, 1, d), bat(2)),
                  pl.BlockSpec((1, 1, d), bat(3)),
                  pl.BlockSpec((1, 1, d), bat(4)),
                  pl.BlockSpec(wret.shape, lambda i: (0, 0, 0)),
                  pl.BlockSpec(wconv.shape, const),
                  pl.BlockSpec(wmix.shape, const),
                  pl.BlockSpec(dw.shape, const),
                  pl.BlockSpec(vec.shape, const),
                  pl.BlockSpec(wrh.shape, const),
                  pl.BlockSpec(wrl.shape, const),
                  pl.BlockSpec(br.shape, const)],
        out_specs=[pl.BlockSpec((tm, d), lambda i: (i, 0)),
                   pl.BlockSpec((tm, d), lambda i: (i, 0)),
                   pl.BlockSpec((tm, ROUTER_LANES), lambda i: (i, 0)),
                   pl.BlockSpec((tm, ROUTER_LANES), lambda i: (i, 0)),
                   pl.BlockSpec((SUBLANES, ROUTER_LANES), const)],
        out_shape=[jax.ShapeDtypeStruct((n, d), F32),
                   jax.ShapeDtypeStruct((n, d), F32),
                   jax.ShapeDtypeStruct((n, ROUTER_LANES), I32),
                   jax.ShapeDtypeStruct((n, ROUTER_LANES), F32),
                   jax.ShapeDtypeStruct((SUBLANES, ROUTER_LANES), F32)],
        scratch_shapes=[pltpu.VMEM((tm + 2 * CONV_HALO, d), F32),
                        pltpu.VMEM((tm, d), F32),
                        pltpu.VMEM((SUBLANES, ROUTER_LANES), F32),
                        pltpu.VMEM((YA_BLOCKS, tm, d // YA_BLOCKS), F32)],
        compiler_params=_params("arbitrary"),
        name="merge",
    )(ret2, proj2, proj2, proj2, proj2, proj2, x2, mod3, mod3, mod3,
      wret, wconv, wmix, dw, vec, wrh, wrl, br)


DISPATCH_TILE = 512


def _dispatch_kernel(pos_ref, t_ref, xs_hbm, sem):
    tm = t_ref.shape[0]
    base = pl.program_id(0) * tm

    for r in range(tm):
        for k in range(EXPERT_TOP_K):
            dst = pos_ref[(base + r) * EXPERT_TOP_K + k]
            pltpu.make_async_copy(t_ref.at[pl.ds(r, 1)], xs_hbm.at[pl.ds(dst, 1)], sem).start()
    for k in range(EXPERT_TOP_K):
        pltpu.make_async_copy(t_ref, xs_hbm.at[pl.ds(0, tm)], sem).wait()


def _dispatch(pos_flat, t2):
    n, d = t2.shape
    tm = min(DISPATCH_TILE, n)
    return pl.pallas_call(
        _dispatch_kernel,
        grid=(n // tm,),
        in_specs=[pl.BlockSpec(memory_space=pltpu.SMEM),
                  pl.BlockSpec((tm, d), lambda i: (i, 0))],
        out_specs=pl.BlockSpec(memory_space=pl.ANY),
        out_shape=jax.ShapeDtypeStruct((n * EXPERT_TOP_K, d), F32),
        scratch_shapes=[pltpu.SemaphoreType.DMA],
        compiler_params=_params("arbitrary"),
        name="dispatch",
    )(pos_flat, t2)


EXPERT_TILE = 512


def _experts_kernel(ie_ref, it_ref, lo_ref, hi_ref, nv_ref, xs_ref, wg_ref, wu_ref, wd_ref, ys_ref,
                    wg_bf, wu_bf, wd_bf):
    k = pl.program_id(0)
    prev = jnp.maximum(k - 1, 0)
    new_expert = (k == 0) | (ie_ref[k] != ie_ref[prev])
    new_tile = (k == 0) | (it_ref[k] != it_ref[prev])

    @pl.when(new_expert)
    def _():
        wg_bf[...] = wg_ref[0, 0].astype(BF16)
        wu_bf[...] = wu_ref[0, 0].astype(BF16)
        wd_bf[...] = wd_ref[0, 0].astype(BF16)

    @pl.when(k < nv_ref[0])
    def _():
        x = xs_ref[...].astype(BF16)
        g = jnp.dot(x, wg_bf[...], preferred_element_type=F32)
        u = jnp.dot(x, wu_bf[...], preferred_element_type=F32)
        hid = (g * _sigmoid(g) * u).astype(BF16)
        y = jnp.dot(hid, wd_bf[...], preferred_element_type=F32)
        rows = lax.broadcasted_iota(I32, (xs_ref.shape[0], 1), 0)
        mine = (rows >= lo_ref[k]) & (rows < hi_ref[k])

        @pl.when(new_tile)
        def _():
            ys_ref[...] = jnp.where(mine, y, 0.0)

        @pl.when(jnp.logical_not(new_tile))
        def _():
            ys_ref[...] = jnp.where(mine, y, ys_ref[...])


def _experts(meta, xs, w_gate, w_up, w_down):
    p, d = xs.shape
    ff = w_gate.shape[-1]
    n_items = meta[0].shape[0]
    grid_spec = pltpu.PrefetchScalarGridSpec(
        num_scalar_prefetch=5,
        grid=(n_items,),
        in_specs=[pl.BlockSpec((EXPERT_TILE, d), lambda k, ie, it, lo, hi, nv: (it[k], 0)),
                  pl.BlockSpec((1, 1, d, ff), lambda k, ie, it, lo, hi, nv: (0, ie[k], 0, 0)),
                  pl.BlockSpec((1, 1, d, ff), lambda k, ie, it, lo, hi, nv: (0, ie[k], 0, 0)),
                  pl.BlockSpec((1, 1, ff, d), lambda k, ie, it, lo, hi, nv: (0, ie[k], 0, 0))],
        out_specs=pl.BlockSpec((EXPERT_TILE, d), lambda k, ie, it, lo, hi, nv: (it[k], 0)),
        scratch_shapes=[pltpu.VMEM((d, ff), BF16), pltpu.VMEM((d, ff), BF16), pltpu.VMEM((ff, d), BF16)])
    return pl.pallas_call(
        _experts_kernel,
        grid_spec=grid_spec,
        out_shape=jax.ShapeDtypeStruct((p, d), F32),
        compiler_params=_params("arbitrary"),
        name="experts",
    )(*meta, xs, w_gate, w_up, w_down)


def _expert_work_items(counts, p):
    n_tiles = p // EXPERT_TILE
    n_items = n_tiles + N_EXPERTS - 1
    offs = jnp.concatenate([jnp.zeros((1,), I32), jnp.cumsum(counts)])
    first = offs[:-1] // EXPERT_TILE
    last = (offs[1:] - 1) // EXPERT_TILE
    per = jnp.where(counts > 0, last - first + 1, 0)
    ends = jnp.cumsum(per)
    starts = ends - per
    n_valid = ends[-1]
    k = jnp.arange(n_items, dtype=I32)
    kk = jnp.minimum(k, n_valid - 1)
    ie = jnp.sum((ends[None, :] <= kk[:, None]).astype(I32), axis=1)
    it = first[ie] + (kk - starts[ie])
    lo = jnp.maximum(offs[ie], it * EXPERT_TILE) - it * EXPERT_TILE
    hi = jnp.minimum(offs[ie + 1], (it + 1) * EXPERT_TILE) - it * EXPERT_TILE
    valid = k < n_valid
    lo = jnp.where(valid, lo, 0)
    hi = jnp.where(valid, hi, 0)
    return offs, (ie.astype(I32), it.astype(I32), lo.astype(I32), hi.astype(I32), n_valid.reshape(1).astype(I32))


COMBINE_HALF = 256


def _combine_kernel(pos_ref, ys_hbm, rw_ref, h1_ref, gt_ref, ln_ref, out_ref, buf, sem):
    i = pl.program_id(0)
    n_steps = pl.num_programs(0)
    hm = COMBINE_HALF
    base = i * (2 * hm)

    def slot_copy(s, k):
        return pltpu.make_async_copy(ys_hbm.at[pl.ds(0, hm)], buf.at[s, k], sem.at[s])

    def gather(first_tok, s):
        for r in range(hm):
            for k in range(EXPERT_TOP_K):
                src = pos_ref[(first_tok + r) * EXPERT_TOP_K + k]
                pltpu.make_async_copy(ys_hbm.at[pl.ds(src, 1)], buf.at[s, k, pl.ds(r, 1)], sem.at[s]).start()

    def finish(s):
        for k in range(EXPERT_TOP_K):
            slot_copy(s, k).wait()
        rows = slice(s * hm, (s + 1) * hm)
        rw = rw_ref[rows, :]
        y = rw[:, RT_W1:RT_W1 + 1] * buf[s, 0] + rw[:, RT_W2:RT_W2 + 1] * buf[s, 1]
        out_ref[rows, :] = _ln_rows(ALPHA * h1_ref[rows, :] + gt_ref[0] * y, ln_ref[0:1, :], ln_ref[1:2, :])

    @pl.when(i == 0)
    def _():
        gather(0, 0)

    gather(base + hm, 1)
    finish(0)

    @pl.when(i + 1 < n_steps)
    def _():
        gather(base + 2 * hm, 0)

    finish(1)


def _combine(pos_flat, ys, rw, h1, mod3, ln2, seq):
    n, d = h1.shape
    tm = 2 * COMBINE_HALF
    tiles_per_seq = seq // tm
    return pl.pallas_call(
        _combine_kernel,
        grid=(n // tm,),
        in_specs=[pl.BlockSpec(memory_space=pltpu.SMEM),
                  pl.BlockSpec(memory_space=pl.ANY),
                  pl.BlockSpec((tm, ROUTER_LANES), lambda i: (i, 0)),
                  pl.BlockSpec((tm, d), lambda i: (i, 0)),
                  pl.BlockSpec((1, 1, d), lambda i: (i // tiles_per_seq, 0, 5)),
                  pl.BlockSpec(ln2.shape, lambda i: (0, 0))],
        out_specs=pl.BlockSpec((tm, d), lambda i: (i, 0)),
        out_shape=jax.ShapeDtypeStruct((n, d), F32),
        scratch_shapes=[pltpu.VMEM((2, EXPERT_TOP_K, COMBINE_HALF, d), F32),
                        pltpu.SemaphoreType.DMA((2,))],
        compiler_params=_params("arbitrary"),
        name="combine",
    )(pos_flat, ys, rw, h1, mod3, ln2)


def _rope_tables(seq):
    pos = np.arange(seq)
    r = (pos // GRID_W).astype(np.float32)
    col = (pos % GRID_W).astype(np.float32)
    n_freq = RET_QK_DIM // 4
    inv = (np.float32(ROPE_BASE) ** (-np.arange(n_freq, dtype=np.float32) / np.float32(n_freq))).astype(np.float32)
    ang = np.concatenate([r[:, None] * inv, col[:, None] * inv], axis=-1)
    ang = np.concatenate([ang, ang], axis=-1).astype(np.float32)
    sign = np.concatenate([-np.ones((RET_QK_DIM // 2,), np.float32), np.ones((RET_QK_DIM // 2,), np.float32)])
    return jnp.asarray(np.cos(ang), F32), jnp.asarray(np.sin(ang) * sign, F32)


def kernel(x, c, ctx, c_ctx, w_ada, b_ada, w_in, b_in, ret_decay_fwd, ret_decay_bwd, w_ret_out, conv_dw, conv_dw_b, conv_ln_g, conv_ln_b, w_conv_out, b_conv_out, w_mix_out, b_mix_out, ln1_g, ln1_b, w_router_grp, b_router_grp, w_router_exp, b_router_exp, w_exp_gate, w_exp_up, w_exp_down, ln2_g, ln2_b):
    b_, seq, d = x.shape
    n = b_ * seq
    assert w_ada.shape[0] == DEPTH
    mod_rows = SUBLANES
    assert b_ + 1 <= mod_rows

    cs = jnp.concatenate([c, c_ctx[None, :], jnp.zeros((mod_rows - b_ - 1, d), F32)], axis=0)
    mod = _ada(cs, w_ada[0], b_ada[0][None, :])
    mod3 = mod.reshape(mod_rows, 1, 6 * d)

    in_blocks = w_in.shape[-1] // d
    w_in3 = w_in[0].astype(BF16).reshape(d, in_blocks, d).transpose(1, 0, 2)
    b_in3 = b_in[0].reshape(in_blocks, 1, d)
    dec_f = ret_decay_fwd[0].reshape(RET_HEADS, 1, 1)
    dec_b = ret_decay_bwd[0].reshape(RET_HEADS, 1, 1)

    s0f, s0b = _ctx_states(ctx, mod3, w_in3, b_in3, dec_f, dec_b, b_)

    cos, sin = _rope_tables(seq)
    x2 = x.reshape(n, d)
    proj = _proj(x2, mod3, w_in3, b_in3, cos, sin, seq, tm=min(1024, seq))

    ret = _retention(proj.reshape(b_, seq, PROJ_BLOCKS * d), s0f, s0b, dec_f, dec_b, d)

    vec = jnp.concatenate([conv_dw_b, conv_ln_g, conv_ln_b, b_conv_out, b_mix_out, ln1_g, ln1_b,
                           jnp.zeros((1, d), F32)], axis=0)
    dw = jnp.concatenate([conv_dw[0], jnp.zeros((32 - CONV_WIDTH, d), F32)], axis=0)
    pad = ROUTER_LANES - N_GROUPS - N_EXPERTS
    wr = jnp.concatenate([w_router_grp[0], w_router_exp[0], jnp.zeros((d, pad), F32)], axis=1)
    wr_hi = wr.astype(BF16)
    wr_lo = (wr - wr_hi.astype(F32)).astype(BF16)
    br = jnp.concatenate([b_router_grp[0], b_router_exp[0], jnp.zeros((pad,), F32)])[None, :]
    h1, t, route_i, route_w, cnt = _merge(
        ret.reshape(n, RET_HEADS * RET_V_DIM), proj, x2, mod3,
        w_ret_out[0].astype(BF16).reshape(-1, YA_BLOCKS, d // YA_BLOCKS).transpose(1, 0, 2),
        w_conv_out[0].astype(BF16), w_mix_out[0].astype(BF16),
        dw, vec, wr_hi, wr_lo, br, seq, tm=min(512, seq))

    counts = cnt[0, N_GROUPS:N_GROUPS + N_EXPERTS].astype(I32)
    offs, meta = _expert_work_items(counts, n * EXPERT_TOP_K)
    eid = route_i[:, RT_E1:RT_E2 + 1]
    seg_start = jnp.sum(jnp.where(eid[:, :, None] == jnp.arange(N_EXPERTS, dtype=I32), offs[:N_EXPERTS], 0), axis=-1)
    pos = seg_start + route_i[:, RT_RANK1:RT_RANK2 + 1]
    pos_flat = pos.reshape(n * EXPERT_TOP_K)

    xs = _dispatch(pos_flat, t)
    ys = _experts(meta, xs, w_exp_gate, w_exp_up, w_exp_down)
    ln2 = jnp.concatenate([ln2_g, ln2_b, jnp.zeros((SUBLANES - 2, d), F32)], axis=0)
    out = _combine(pos_flat, ys, route_w, h1, mod3, ln2, seq)
    return out.reshape(b_, seq, d)
```

```python
import functools

import jax
import jax.numpy as jnp
import numpy as np
from jax import lax
from jax.experimental import pallas as pl
from jax.experimental.pallas import tpu as pltpu

GRID_W = 64
RET_HEADS = 8
RET_QK_DIM = 128
RET_V_DIM = 256
RET_CHUNK = 256
ROPE_BASE = 10000.0
CONV_WIDTH = 31
CONV_HALO = 16
SUBLANES = 8
N_GROUPS = 4
EXPERTS_PER_GROUP = 8
N_EXPERTS = N_GROUPS * EXPERTS_PER_GROUP
EXPERT_TOP_K = 2
LN_EPS = 1e-5
DEPTH = 1
ALPHA = (2.0 * DEPTH) ** 0.25
ROUTER_LANES = 128

V7X_VMEM_LIMIT = 56 * 1024 * 1024

F32 = jnp.float32
BF16 = jnp.bfloat16
I32 = jnp.int32


def _params(*sem):
    return pltpu.CompilerParams(dimension_semantics=sem, vmem_limit_bytes=V7X_VMEM_LIMIT)


def _sigmoid(v):
    return 0.5 * jnp.tanh(0.5 * v) + 0.5


def _split_bf16(v):
    hi = v.astype(BF16)
    return hi, (v - hi.astype(F32)).astype(BF16)


def _dot3(a, b_hi, b_lo):
    a_hi, a_lo = _split_bf16(a)
    return (jnp.dot(a_hi, b_hi, preferred_element_type=F32)
            + jnp.dot(a_lo, b_hi, preferred_element_type=F32)
            + jnp.dot(a_hi, b_lo, preferred_element_type=F32))


def _ln_rows(v, g, b):
    mu = jnp.mean(v, axis=-1, keepdims=True)
    d = v - mu
    var = jnp.mean(d * d, axis=-1, keepdims=True)
    return d * lax.rsqrt(var + LN_EPS) * g + b


def _ada_kernel(cs_ref, w_ref, b_ref, out_ref):
    s = cs_ref[...]
    s = s * _sigmoid(s)
    w_hi, w_lo = _split_bf16(w_ref[...])
    out_ref[...] = _dot3(s, w_hi, w_lo) + b_ref[...]


def _ada(cs, w_ada, b_ada):
    rows, d = cs.shape
    cols = w_ada.shape[1]
    tn = 1024
    return pl.pallas_call(
        _ada_kernel,
        grid=(cols // tn,),
        in_specs=[pl.BlockSpec((rows, d), lambda j: (0, 0)),
                  pl.BlockSpec((d, tn), lambda j: (0, j)),
                  pl.BlockSpec((1, tn), lambda j: (0, j))],
        out_specs=pl.BlockSpec((rows, tn), lambda j: (0, j)),
        out_shape=jax.ShapeDtypeStruct((rows, cols), F32),
        compiler_params=_params("arbitrary"),
        name="ada",
    )(cs, w_ada, b_ada)


def _log_sigmoid(v):
    return jnp.minimum(v, 0.0) - jnp.log(1.0 + jnp.exp(-jnp.abs(v)))


def _ctx_kernel(ctx_ref, sh_ref, sc_ref, wk_ref, wv0_ref, wv1_ref, bk_ref, bv0_ref, bv1_ref, df_ref, db_ref,
                sf_ref, sb_ref):
    lc = ctx_ref.shape[1]
    u = (ctx_ref[0] * (1.0 + sc_ref[0]) + sh_ref[0]).astype(BF16)
    k = (jnp.dot(u, wk_ref[0], preferred_element_type=F32) + bk_ref[0]) * RET_QK_DIM ** -0.5
    v = jnp.concatenate(
        [(jnp.dot(u, w[0], preferred_element_type=F32) + b[0]).astype(BF16)
         for w, b in ((wv0_ref, bv0_ref), (wv1_ref, bv1_ref))], axis=1)
    pos = lax.broadcasted_iota(I32, (lc, 1), 0).astype(F32)
    dn = (((0,), (0,)), ((), ()))
    for h in range(RET_HEADS):
        lgf = _log_sigmoid(df_ref[h])
        lgb = _log_sigmoid(db_ref[h])
        kh = k[:, h * RET_QK_DIM:(h + 1) * RET_QK_DIM]
        vh = v[:, h * RET_V_DIM:(h + 1) * RET_V_DIM]
        kf = (kh * jnp.exp((lc - 1.0 - pos) * lgf)).astype(BF16)
        kb = (kh * jnp.exp(pos * lgb)).astype(BF16)
        sf_ref[0, h] = lax.dot_general(kf, vh, dn, preferred_element_type=F32)
        sb_ref[0, h] = lax.dot_general(kb, vh, dn, preferred_element_type=F32)


def _ctx_states(ctx, mod3, w_in3, b_in3, dec_f, dec_b, ctx_row):
    b_, lc, d = ctx.shape
    assert RET_HEADS * RET_QK_DIM == d and RET_HEADS * RET_V_DIM == 2 * d
    st = jax.ShapeDtypeStruct((b_, RET_HEADS, RET_QK_DIM, RET_V_DIM), F32)
    st_spec = pl.BlockSpec((1, RET_HEADS, RET_QK_DIM, RET_V_DIM), lambda b: (b, 0, 0, 0))
    wblk = lambda blk: pl.BlockSpec((1, d, d), lambda b: (blk, 0, 0))
    bblk = lambda blk: pl.BlockSpec((1, 1, d), lambda b: (blk, 0, 0))
    dec_spec = pl.BlockSpec((RET_HEADS, 1, 1), lambda b: (0, 0, 0))
    return pl.pallas_call(
        _ctx_kernel,
        grid=(b_,),
        in_specs=[pl.BlockSpec((1, lc, d), lambda b: (b, 0, 0)),
                  pl.BlockSpec((1, 1, d), lambda b: (ctx_row, 0, 0)),
                  pl.BlockSpec((1, 1, d), lambda b: (ctx_row, 0, 1)),
                  wblk(PB_K), wblk(PB_V), wblk(PB_V + 1),
                  bblk(PB_K), bblk(PB_V), bblk(PB_V + 1),
                  dec_spec, dec_spec],
        out_specs=[st_spec, st_spec],
        out_shape=[st, st],
        compiler_params=_params("arbitrary"),
        name="ctx",
    )(ctx, mod3, mod3, w_in3, w_in3, w_in3, b_in3, b_in3, b_in3, dec_f, dec_b)


PROJ_BLOCKS = 9
PB_Q, PB_K, PB_V, PB_G, PB_Z, PB_GATE = 0, 1, 2, 4, 6, 7


def _proj_kernel(x_ref, sh_ref, sc_ref, w_ref, b_ref, cos_ref, sin_ref, out_ref, u_scr):
    j = pl.program_id(1)

    @pl.when(j == 0)
    def _():
        u_scr[...] = (x_ref[...] * (1.0 + sc_ref[0]) + sh_ref[0]).astype(BF16)

    def mm(blk):
        return jnp.dot(u_scr[...], w_ref[blk], preferred_element_type=F32) + b_ref[blk]

    def rope(acc):
        cos = cos_ref[...]
        sin = sin_ref[...]
        for h in range(acc.shape[1] // RET_QK_DIM):
            sl = slice(h * RET_QK_DIM, (h + 1) * RET_QK_DIM)
            seg = acc[:, sl]
            out_ref[:, sl] = (seg * cos + pltpu.roll(seg, RET_QK_DIM // 2, 1) * sin).astype(BF16)

    @pl.when(j == PB_Q)
    def _():
        rope(mm(PB_Q))

    @pl.when(j == PB_K)
    def _():
        rope(mm(PB_K) * RET_QK_DIM ** -0.5)

    @pl.when((j >= PB_V) & (j < PB_G))
    def _():
        out_ref[...] = mm(j).astype(BF16)

    @pl.when((j >= PB_G) & (j < PB_Z))
    def _():
        acc = mm(j)
        out_ref[...] = (acc * _sigmoid(acc)).astype(BF16)

    @pl.when(j == PB_Z)
    def _():
        out_ref[...] = (mm(PB_Z) * _sigmoid(mm(PB_Z + 1))).astype(BF16)

    @pl.when(j >= PB_GATE)
    def _():
        out_ref[...] = _sigmoid(mm(j + 1)).astype(BF16)


def _proj(x2, mod3, w_in3, b_in3, cos, sin, seq, tm):
    n, d = x2.shape
    tiles_per_seq = seq // tm
    return pl.pallas_call(
        _proj_kernel,
        grid=(n // tm, PROJ_BLOCKS),
        in_specs=[pl.BlockSpec((tm, d), lambda i, j: (i, 0)),
                  pl.BlockSpec((1, 1, d), lambda i, j: (i // tiles_per_seq, 0, 0)),
                  pl.BlockSpec((1, 1, d), lambda i, j: (i // tiles_per_seq, 0, 1)),
                  pl.BlockSpec(memory_space=pltpu.VMEM),
                  pl.BlockSpec(memory_space=pltpu.VMEM),
                  pl.BlockSpec((tm, RET_QK_DIM), lambda i, j: (i % tiles_per_seq, 0)),
                  pl.BlockSpec((tm, RET_QK_DIM), lambda i, j: (i % tiles_per_seq, 0))],
        out_specs=pl.BlockSpec((tm, d), lambda i, j: (i, j)),
        out_shape=jax.ShapeDtypeStruct((n, PROJ_BLOCKS * d), BF16),
        scratch_shapes=[pltpu.VMEM((tm, d), BF16)],
        compiler_params=_params("arbitrary", "arbitrary"),
        name="proj",
    )(x2, mod3, mod3, w_in3, b_in3, cos, sin)


def _ret_kernel(q_ref, k_ref, v_ref, g_ref, s0f_ref, s0b_ref, df_ref, db_ref, out_ref,
                sf_all, sb_all, p_scr, o_scr):
    c_ = RET_CHUNK
    n_chunks = q_ref.shape[1] // c_
    dn_t = (((0,), (0,)), ((), ()))
    lgf = _log_sigmoid(df_ref[0])
    lgb = _log_sigmoid(db_ref[0])
    ri = lax.broadcasted_iota(I32, (c_, c_), 0)
    ci = lax.broadcasted_iota(I32, (c_, c_), 1)
    diff = (ri - ci).astype(F32)
    mask = jnp.where(diff > 0, jnp.exp(diff * lgf), jnp.where(diff < 0, jnp.exp(-diff * lgb), 2.0))
    pos = lax.broadcasted_iota(I32, (c_, 1), 0).astype(F32)
    qdec_f = jnp.exp((pos + 1.0) * lgf)
    qdec_b = jnp.exp((c_ - pos) * lgb)
    kdec_f = jnp.exp((c_ - 1.0 - pos) * lgf)
    kdec_b = jnp.exp(pos * lgb)
    cdec_f = jnp.exp(c_ * lgf)
    cdec_b = jnp.exp(c_ * lgb)

    def chunk(c):
        return slice(c * c_, (c + 1) * c_)

    for c in range(n_chunks):
        k = k_ref[0, chunk(c), :].astype(F32)
        v = v_ref[0, chunk(c), :]
        sf_all[c] = lax.dot_general((k * kdec_f).astype(BF16), v, dn_t, preferred_element_type=F32)
        sb_all[c] = lax.dot_general((k * kdec_b).astype(BF16), v, dn_t, preferred_element_type=F32)

    sf = s0f_ref[0, 0]
    sb = s0b_ref[0, 0]
    for t in range(n_chunks):
        kv = sf_all[t]
        sf_all[t] = sf
        sf = sf * cdec_f + kv
        c = n_chunks - 1 - t
        kv = sb_all[c]
        sb_all[c] = sb
        sb = sb * cdec_b + kv

    for c in range(n_chunks):
        s = lax.dot_general(q_ref[0, chunk(c), :], k_ref[0, chunk(c), :], (((1,), (1,)), ((), ())),
                            preferred_element_type=F32)
        p_scr[c] = (s * mask).astype(BF16)

    for c in range(n_chunks):
        q = q_ref[0, chunk(c), :]
        o = jnp.dot(p_scr[c], v_ref[0, chunk(c), :], preferred_element_type=F32)
        o = o + qdec_f * jnp.dot(q, sf_all[c].astype(BF16), preferred_element_type=F32)
        o = o + qdec_b * jnp.dot(q, sb_all[c].astype(BF16), preferred_element_type=F32)
        o_scr[chunk(c), :] = o

    for c in range(n_chunks):
        o = o_scr[chunk(c), :]
        mu = jnp.mean(o, axis=-1, keepdims=True)
        d = o - mu
        var = jnp.mean(d * d, axis=-1, keepdims=True)
        on = d * lax.rsqrt(var + LN_EPS)
        out_ref[0, chunk(c), :] = (on * g_ref[0, chunk(c), :].astype(F32)).astype(BF16)


def _retention(proj3, s0f, s0b, dec_f, dec_b, d):
    b_, seq, _ = proj3.shape
    qb, kb = PB_Q * d // RET_QK_DIM, PB_K * d // RET_QK_DIM
    vb, gb = PB_V * d // RET_V_DIM, PB_G * d // RET_V_DIM
    n_chunks = seq // RET_CHUNK
    st_spec = pl.BlockSpec((1, 1, RET_QK_DIM, RET_V_DIM), lambda b, h: (b, h, 0, 0))
    st_all = pltpu.VMEM((n_chunks, RET_QK_DIM, RET_V_DIM), F32)
    return pl.pallas_call(
        _ret_kernel,
        grid=(b_, RET_HEADS),
        in_specs=[pl.BlockSpec((1, seq, RET_QK_DIM), lambda b, h: (b, 0, qb + h)),
                  pl.BlockSpec((1, seq, RET_QK_DIM), lambda b, h: (b, 0, kb + h)),
                  pl.BlockSpec((1, seq, RET_V_DIM), lambda b, h: (b, 0, vb + h)),
                  pl.BlockSpec((1, seq, RET_V_DIM), lambda b, h: (b, 0, gb + h)),
                  st_spec, st_spec,
                  pl.BlockSpec((1, 1, 1), lambda b, h: (h, 0, 0)),
                  pl.BlockSpec((1, 1, 1), lambda b, h: (h, 0, 0))],
        out_specs=pl.BlockSpec((1, seq, RET_V_DIM), lambda b, h: (b, 0, h)),
        out_shape=jax.ShapeDtypeStruct((b_, seq, RET_HEADS * RET_V_DIM), BF16),
        scratch_shapes=[st_all, st_all,
                        pltpu.VMEM((n_chunks, RET_CHUNK, RET_CHUNK), BF16),
                        pltpu.VMEM((seq, RET_V_DIM), F32)],
        compiler_params=_params("arbitrary", "arbitrary"),
        name="ret",
    )(proj3, proj3, proj3, proj3, s0f, s0b, dec_f, dec_b)


VEC_CONV_B, VEC_CLN_G, VEC_CLN_B, VEC_BCONV, VEC_BMIX, VEC_LN1_G, VEC_LN1_B = range(7)
CONV_ROWS = 64
YA_BLOCKS = 4
LANES = 128
RT_E1, RT_E2, RT_RANK1, RT_RANK2 = 0, 1, 2, 3
RT_W1, RT_W2 = 0, 1


def _merge_kernel(ret_ref, z_ref, zp_ref, zn_ref, ga_ref, gb_ref, x_ref, gt_ref, shf_ref, scf_ref,
                  wret_ref, wconv_ref, wmix_ref, dw_ref, vec_ref, wrh_ref, wrl_ref, br_ref,
                  h1_ref, t_ref, ri_ref, rw_ref, cnt_ref, zext, zc_scr, cnt_scr, ya_scr, *, tiles_per_seq):
    tm, d = x_ref.shape
    step = pl.program_id(0)
    ti = step % tiles_per_seq
    vec = vec_ref[...]

    def row(r):
        return vec[r:r + 1, :]


    has_prev = (ti > 0).astype(F32)
    has_next = (ti < tiles_per_seq - 1).astype(F32)
    zext[0:CONV_HALO, :] = zp_ref[...].astype(F32) * has_prev
    zext[CONV_HALO:CONV_HALO + tm, :] = z_ref[...].astype(F32)
    zext[CONV_HALO + tm:, :] = zn_ref[...].astype(F32) * has_next
    base = CONV_HALO - CONV_WIDTH // 2
    win_rows = CONV_ROWS + 2 * CONV_HALO

    n_lt = d // LANES
    kc = ret_ref.shape[1] // n_lt

    def conv_piece(r0, lt):
        ls = slice(lt * LANES, (lt + 1) * LANES)
        win = zext[pl.ds(r0, win_rows), ls]
        acc = jnp.zeros((CONV_ROWS, LANES), F32) + vec_ref[VEC_CONV_B:VEC_CONV_B + 1, ls]
        for s in range(SUBLANES):
            taps = [w for w in range(CONV_WIDTH) if (base + w) % SUBLANES == s]
            if not taps:
                continue
            sh = win if s == 0 else pltpu.roll(win, win_rows - s, 0)
            for w in taps:
                a = (base + w) - s
                acc = acc + sh[a:a + CONV_ROWS, :] * dw_ref[w:w + 1, ls]
        return acc

    def conv_rows(r, carry):
        r0 = pl.multiple_of(r * (2 * CONV_ROWS), 2 * CONV_ROWS)
        ya = None
        accs = []
        for lt in range(n_lt):
            part = jnp.dot(ret_ref[:, lt * kc:(lt + 1) * kc], wret_ref[r, lt * kc:(lt + 1) * kc, :],
                           preferred_element_type=F32)
            ya = part if ya is None else ya + part
            accs.append((conv_piece(r0, lt), conv_piece(r0 + CONV_ROWS, lt)))
        ya_scr[r] = ya
        for lt in range(n_lt):
            ls = slice(lt * LANES, (lt + 1) * LANES)
            zc_scr[pl.ds(r0, CONV_ROWS), ls] = accs[lt][0]
            zc_scr[pl.ds(r0 + CONV_ROWS, CONV_ROWS), ls] = accs[lt][1]
        return carry

    lax.fori_loop(0, YA_BLOCKS, conv_rows, 0)
    y_a = jnp.concatenate([ya_scr[q] for q in range(YA_BLOCKS)], axis=1)
    zc = _ln_rows(zc_scr[...], row(VEC_CLN_G), row(VEC_CLN_B))
    zc = zc * _sigmoid(zc)
    y_b = jnp.dot(zc.astype(BF16), wconv_ref[...], preferred_element_type=F32) + row(VEC_BCONV)

    mixed = ga_ref[...].astype(F32) * y_a + gb_ref[...].astype(F32) * y_b
    mix = jnp.dot(mixed.astype(BF16), wmix_ref[...], preferred_element_type=F32) + row(VEC_BMIX)
    h1 = _ln_rows(ALPHA * x_ref[...] + gt_ref[0] * mix, row(VEC_LN1_G), row(VEC_LN1_B))
    h1_ref[...] = h1
    t = h1 * (1.0 + scf_ref[0]) + shf_ref[0]
    t_ref[...] = t

    logits = _dot3(t, wrh_ref[...], wrl_ref[...]) + br_ref[...]

    lane = lax.broadcasted_iota(I32, logits.shape, 1).astype(F32)
    neg = -jnp.inf
    big = float(ROUTER_LANES)
    is_grp = lane < N_GROUPS
    gl = jnp.where(is_grp, logits, neg)
    gmax = jnp.max(gl, axis=-1, keepdims=True)
    gidx = jnp.min(jnp.where(gl == gmax, lane, big), axis=-1, keepdims=True)
    gsum = jnp.sum(jnp.where(is_grp, jnp.exp(gl - gmax), 0.0), axis=-1, keepdims=True)
    grp_w = 1.0 / gsum
    lo = N_GROUPS + gidx * EXPERTS_PER_GROUP
    el = jnp.where(lane >= lo, jnp.where(lane < lo + EXPERTS_PER_GROUP, logits, neg), neg)
    m1 = jnp.max(el, axis=-1, keepdims=True)
    i1 = jnp.min(jnp.where(el == m1, lane, big), axis=-1, keepdims=True)
    el2 = jnp.where(lane == i1, neg, el)
    m2 = jnp.max(el2, axis=-1, keepdims=True)
    i2 = jnp.min(jnp.where(el2 == m2, lane, big), axis=-1, keepdims=True)
    r = jnp.exp(m2 - m1)
    w1 = grp_w / (1.0 + r)
    w2 = grp_w * r / (1.0 + r)

    @pl.when(step == 0)
    def _():
        cnt_scr[...] = jnp.zeros_like(cnt_scr)

    oh1 = lane == i1
    oh2 = lane == i2
    oh = jnp.where(oh1, 1.0, jnp.where(oh2, 1.0, 0.0))
    tri = jnp.where(lax.broadcasted_iota(I32, (tm, tm), 0) > lax.broadcasted_iota(I32, (tm, tm), 1), 1.0, 0.0)
    before = jnp.dot(tri.astype(BF16), oh.astype(BF16), preferred_element_type=F32) + cnt_scr[0:1, :]
    rank1 = jnp.sum(jnp.where(oh1, before, 0.0), axis=-1, keepdims=True)
    rank2 = jnp.sum(jnp.where(oh2, before, 0.0), axis=-1, keepdims=True)
    cnt = cnt_scr[0:1, :] + jnp.sum(oh, axis=0, keepdims=True)
    cnt_scr[...] = jnp.broadcast_to(cnt, cnt_scr.shape)
    cnt_ref[...] = jnp.broadcast_to(cnt, cnt_ref.shape)

    e1 = i1 - N_GROUPS
    e2 = i2 - N_GROUPS
    ri = jnp.where(lane == RT_E1, e1, jnp.where(lane == RT_E2, e2,
                   jnp.where(lane == RT_RANK1, rank1, jnp.where(lane == RT_RANK2, rank2, 0.0))))
    ri_ref[...] = ri.astype(I32)
    rw_ref[...] = jnp.where(lane == RT_W1, w1, jnp.where(lane == RT_W2, w2, 0.0))


def _merge(ret2, proj2, x2, mod3, wret, wconv, wmix, dw, vec, wrh, wrl, br, seq, tm):
    n, d = x2.shape
    assert tm == YA_BLOCKS * 2 * CONV_ROWS
    tiles_per_seq = seq // tm
    hb = tm // CONV_HALO
    last_hb = n // CONV_HALO - 1
    const = lambda i: (0, 0)
    bat = lambda k: (lambda i: (i // tiles_per_seq, 0, k))
    return pl.pallas_call(
        functools.partial(_merge_kernel, tiles_per_seq=tiles_per_seq),
        grid=(n // tm,),
        in_specs=[pl.BlockSpec((tm, ret2.shape[1]), lambda i: (i, 0)),
                  pl.BlockSpec((tm, d), lambda i: (i, PB_Z)),
                  pl.BlockSpec((CONV_HALO, d), lambda i: (jnp.maximum(i * hb - 1, 0), PB_Z)),
                  pl.BlockSpec((CONV_HALO, d), lambda i: (jnp.minimum((i + 1) * hb, last_hb), PB_Z)),
                  pl.BlockSpec((tm, d), lambda i: (i, PB_GATE)),
                  pl.BlockSpec((tm, d), lambda i: (i, PB_GATE + 1)),
                  pl.BlockSpec((tm, d), lambda i: (i, 0)),
                  pl.BlockSpec((1, 1, d), bat(2)),
                  pl.BlockSpec((1, 1, d), bat(3)),
                  pl.BlockSpec((1, 1, d), bat(4)),
                  pl.BlockSpec(wret.shape, lambda i: (0, 0, 0)),
                  pl.BlockSpec(wconv.shape, const),
                  pl.BlockSpec(wmix.shape, const),
                  pl.BlockSpec(dw.shape, const),
                  pl.BlockSpec(vec.shape, const),
                  pl.BlockSpec(wrh.shape, const),
                  pl.BlockSpec(wrl.shape, const),
                  pl.BlockSpec(br.shape, const)],
        out_specs=[pl.BlockSpec((tm, d), lambda i: (i, 0)),
                   pl.BlockSpec((tm, d), lambda i: (i, 0)),
                   pl.BlockSpec((tm, ROUTER_LANES), lambda i: (i, 0)),
                   pl.BlockSpec((tm, ROUTER_LANES), lambda i: (i, 0)),
                   pl.BlockSpec((SUBLANES, ROUTER_LANES), const)],
        out_shape=[jax.ShapeDtypeStruct((n, d), F32),
                   jax.ShapeDtypeStruct((n, d), F32),
                   jax.ShapeDtypeStruct((n, ROUTER_LANES), I32),
                   jax.ShapeDtypeStruct((n, ROUTER_LANES), F32),
                   jax.ShapeDtypeStruct((SUBLANES, ROUTER_LANES), F32)],
        scratch_shapes=[pltpu.VMEM((tm + 2 * CONV_HALO, d), F32),
                        pltpu.VMEM((tm, d), F32),
                        pltpu.VMEM((SUBLANES, ROUTER_LANES), F32),
                        pltpu.VMEM((YA_BLOCKS, tm, d // YA_BLOCKS), F32)],
        compiler_params=_params("arbitrary"),
        name="merge",
    )(ret2, proj2, proj2, proj2, proj2, proj2, x2, mod3, mod3, mod3,
      wret, wconv, wmix, dw, vec, wrh, wrl, br)


DISPATCH_TILE = 512


def _dispatch_kernel(pos_ref, t_ref, xs_hbm, sem):
    tm = t_ref.shape[0]
    base = pl.program_id(0) * tm

    for r in range(tm):
        for k in range(EXPERT_TOP_K):
            dst = pos_ref[(base + r) * EXPERT_TOP_K + k]
            pltpu.make_async_copy(t_ref.at[pl.ds(r, 1)], xs_hbm.at[pl.ds(dst, 1)], sem).start()
    for k in range(EXPERT_TOP_K):
        pltpu.make_async_copy(t_ref, xs_hbm.at[pl.ds(0, tm)], sem).wait()


def _dispatch(pos_flat, t2):
    n, d = t2.shape
    tm = min(DISPATCH_TILE, n)
    return pl.pallas_call(
        _dispatch_kernel,
        grid=(n // tm,),
        in_specs=[pl.BlockSpec(memory_space=pltpu.SMEM),
                  pl.BlockSpec((tm, d), lambda i: (i, 0))],
        out_specs=pl.BlockSpec(memory_space=pl.ANY),
        out_shape=jax.ShapeDtypeStruct((n * EXPERT_TOP_K, d), F32),
        scratch_shapes=[pltpu.SemaphoreType.DMA],
        compiler_params=_params("arbitrary"),
        name="dispatch",
    )(pos_flat, t2)


EXPERT_TILE = 512


def _experts_kernel(ie_ref, it_ref, lo_ref, hi_ref, nv_ref, xs_ref, wg_ref, wu_ref, wd_ref, ys_ref,
                    wg_bf, wu_bf, wd_bf):
    k = pl.program_id(0)
    prev = jnp.maximum(k - 1, 0)
    new_expert = (k == 0) | (ie_ref[k] != ie_ref[prev])
    new_tile = (k == 0) | (it_ref[k] != it_ref[prev])

    @pl.when(new_expert)
    def _():
        wg_bf[...] = wg_ref[0, 0].astype(BF16)
        wu_bf[...] = wu_ref[0, 0].astype(BF16)
        wd_bf[...] = wd_ref[0, 0].astype(BF16)

    @pl.when(k < nv_ref[0])
    def _():
        x = xs_ref[...].astype(BF16)
        g = jnp.dot(x, wg_bf[...], preferred_element_type=F32)
        u = jnp.dot(x, wu_bf[...], preferred_element_type=F32)
        hid = (g * _sigmoid(g) * u).astype(BF16)
        y = jnp.dot(hid, wd_bf[...], preferred_element_type=F32)
        rows = lax.broadcasted_iota(I32, (xs_ref.shape[0], 1), 0)
        mine = (rows >= lo_ref[k]) & (rows < hi_ref[k])

        @pl.when(new_tile)
        def _():
            ys_ref[...] = jnp.where(mine, y, 0.0)

        @pl.when(jnp.logical_not(new_tile))
        def _():
            ys_ref[...] = jnp.where(mine, y, ys_ref[...])


def _experts(meta, xs, w_gate, w_up, w_down):
    p, d = xs.shape
    ff = w_gate.shape[-1]
    n_items = meta[0].shape[0]
    grid_spec = pltpu.PrefetchScalarGridSpec(
        num_scalar_prefetch=5,
        grid=(n_items,),
        in_specs=[pl.BlockSpec((EXPERT_TILE, d), lambda k, ie, it, lo, hi, nv: (it[k], 0)),
                  pl.BlockSpec((1, 1, d, ff), lambda k, ie, it, lo, hi, nv: (0, ie[k], 0, 0)),
                  pl.BlockSpec((1, 1, d, ff), lambda k, ie, it, lo, hi, nv: (0, ie[k], 0, 0)),
                  pl.BlockSpec((1, 1, ff, d), lambda k, ie, it, lo, hi, nv: (0, ie[k], 0, 0))],
        out_specs=pl.BlockSpec((EXPERT_TILE, d), lambda k, ie, it, lo, hi, nv: (it[k], 0)),
        scratch_shapes=[pltpu.VMEM((d, ff), BF16), pltpu.VMEM((d, ff), BF16), pltpu.VMEM((ff, d), BF16)])
    return pl.pallas_call(
        _experts_kernel,
        grid_spec=grid_spec,
        out_shape=jax.ShapeDtypeStruct((p, d), F32),
        compiler_params=_params("arbitrary"),
        name="experts",
    )(*meta, xs, w_gate, w_up, w_down)


def _expert_work_items(counts, p):
    n_tiles = p // EXPERT_TILE
    n_items = n_tiles + N_EXPERTS - 1
    offs = jnp.concatenate([jnp.zeros((1,), I32), jnp.cumsum(counts)])
    first = offs[:-1] // EXPERT_TILE
    last = (offs[1:] - 1) // EXPERT_TILE
    per = jnp.where(counts > 0, last - first + 1, 0)
    ends = jnp.cumsum(per)
    starts = ends - per
    n_valid = ends[-1]
    k = jnp.arange(n_items, dtype=I32)
    kk = jnp.minimum(k, n_valid - 1)
    ie = jnp.sum((ends[None, :] <= kk[:, None]).astype(I32), axis=1)
    it = first[ie] + (kk - starts[ie])
    lo = jnp.maximum(offs[ie], it * EXPERT_TILE) - it * EXPERT_TILE
    hi = jnp.minimum(offs[ie + 1], (it + 1) * EXPERT_TILE) - it * EXPERT_TILE
    valid = k < n_valid
    lo = jnp.where(valid, lo, 0)
    hi = jnp.where(valid, hi, 0)
    return offs, (ie.astype(I32), it.astype(I32), lo.astype(I32), hi.astype(I32), n_valid.reshape(1).astype(I32))


COMBINE_HALF = 256


def _combine_kernel(pos_ref, ys_hbm, rw_ref, h1_ref, gt_ref, ln_ref, out_ref, buf, sem):
    i = pl.program_id(0)
    n_steps = pl.num_programs(0)
    hm = COMBINE_HALF
    base = i * (2 * hm)

    def slot_copy(s, k):
        return pltpu.make_async_copy(ys_hbm.at[pl.ds(0, hm)], buf.at[s, k], sem.at[s])

    def gather(first_tok, s):
        for r in range(hm):
            for k in range(EXPERT_TOP_K):
                src = pos_ref[(first_tok + r) * EXPERT_TOP_K + k]
                pltpu.make_async_copy(ys_hbm.at[pl.ds(src, 1)], buf.at[s, k, pl.ds(r, 1)], sem.at[s]).start()

    def finish(s):
        for k in range(EXPERT_TOP_K):
            slot_copy(s, k).wait()
        rows = slice(s * hm, (s + 1) * hm)
        rw = rw_ref[rows, :]
        y = rw[:, RT_W1:RT_W1 + 1] * buf[s, 0] + rw[:, RT_W2:RT_W2 + 1] * buf[s, 1]
        out_ref[rows, :] = _ln_rows(ALPHA * h1_ref[rows, :] + gt_ref[0] * y, ln_ref[0:1, :], ln_ref[1:2, :])

    @pl.when(i == 0)
    def _():
        gather(0, 0)

    gather(base + hm, 1)
    finish(0)

    @pl.when(i + 1 < n_steps)
    def _():
        gather(base + 2 * hm, 0)

    finish(1)


def _combine(pos_flat, ys, rw, h1, mod3, ln2, seq):
    n, d = h1.shape
    tm = 2 * COMBINE_HALF
    tiles_per_seq = seq // tm
    return pl.pallas_call(
        _combine_kernel,
        grid=(n // tm,),
        in_specs=[pl.BlockSpec(memory_space=pltpu.SMEM),
                  pl.BlockSpec(memory_space=pl.ANY),
                  pl.BlockSpec((tm, ROUTER_LANES), lambda i: (i, 0)),
                  pl.BlockSpec((tm, d), lambda i: (i, 0)),
                  pl.BlockSpec((1, 1, d), lambda i: (i // tiles_per_seq, 0, 5)),
                  pl.BlockSpec(ln2.shape, lambda i: (0, 0))],
        out_specs=pl.BlockSpec((tm, d), lambda i: (i, 0)),
        out_shape=jax.ShapeDtypeStruct((n, d), F32),
        scratch_shapes=[pltpu.VMEM((2, EXPERT_TOP_K, COMBINE_HALF, d), F32),
                        pltpu.SemaphoreType.DMA((2,))],
        compiler_params=_params("arbitrary"),
        name="combine",
    )(pos_flat, ys, rw, h1, mod3, ln2)


def _rope_tables(seq):
    pos = np.arange(seq)
    r = (pos // GRID_W).astype(np.float32)
    col = (pos % GRID_W).astype(np.float32)
    n_freq = RET_QK_DIM // 4
    inv = (np.float32(ROPE_BASE) ** (-np.arange(n_freq, dtype=np.float32) / np.float32(n_freq))).astype(np.float32)
    ang = np.concatenate([r[:, None] * inv, col[:, None] * inv], axis=-1)
    ang = np.concatenate([ang, ang], axis=-1).astype(np.float32)
    sign = np.concatenate([-np.ones((RET_QK_DIM // 2,), np.float32), np.ones((RET_QK_DIM // 2,), np.float32)])
    return jnp.asarray(np.cos(ang), F32), jnp.asarray(np.sin(ang) * sign, F32)


def kernel(x, c, ctx, c_ctx, w_ada, b_ada, w_in, b_in, ret_decay_fwd, ret_decay_bwd, w_ret_out, conv_dw, conv_dw_b, conv_ln_g, conv_ln_b, w_conv_out, b_conv_out, w_mix_out, b_mix_out, ln1_g, ln1_b, w_router_grp, b_router_grp, w_router_exp, b_router_exp, w_exp_gate, w_exp_up, w_exp_down, ln2_g, ln2_b):
    b_, seq, d = x.shape
    n = b_ * seq
    assert w_ada.shape[0] == DEPTH
    mod_rows = SUBLANES
    assert b_ + 1 <= mod_rows

    cs = jnp.concatenate([c, c_ctx[None, :], jnp.zeros((mod_rows - b_ - 1, d), F32)], axis=0)
    mod = _ada(cs, w_ada[0], b_ada[0][None, :])
    mod3 = mod.reshape(mod_rows, 1, 6 * d)

    in_blocks = w_in.shape[-1] // d
    w_in3 = w_in[0].astype(BF16).reshape(d, in_blocks, d).transpose(1, 0, 2)
    b_in3 = b_in[0].reshape(in_blocks, 1, d)
    dec_f = ret_decay_fwd[0].reshape(RET_HEADS, 1, 1)
    dec_b = ret_decay_bwd[0].reshape(RET_HEADS, 1, 1)

    s0f, s0b = _ctx_states(ctx, mod3, w_in3, b_in3, dec_f, dec_b, b_)

    cos, sin = _rope_tables(seq)
    x2 = x.reshape(n, d)
    proj = _proj(x2, mod3, w_in3, b_in3, cos, sin, seq, tm=min(1024, seq))

    ret = _retention(proj.reshape(b_, seq, PROJ_BLOCKS * d), s0f, s0b, dec_f, dec_b, d)

    vec = jnp.concatenate([conv_dw_b, conv_ln_g, conv_ln_b, b_conv_out, b_mix_out, ln1_g, ln1_b,
                           jnp.zeros((1, d), F32)], axis=0)
    dw = jnp.concatenate([conv_dw[0], jnp.zeros((32 - CONV_WIDTH, d), F32)], axis=0)
    pad = ROUTER_LANES - N_GROUPS - N_EXPERTS
    wr = jnp.concatenate([w_router_grp[0], w_router_exp[0], jnp.zeros((d, pad), F32)], axis=1)
    wr_hi = wr.astype(BF16)
    wr_lo = (wr - wr_hi.astype(F32)).astype(BF16)
    br = jnp.concatenate([b_router_grp[0], b_router_exp[0], jnp.zeros((pad,), F32)])[None, :]
    h1, t, route_i, route_w, cnt = _merge(
        ret.reshape(n, RET_HEADS * RET_V_DIM), proj, x2, mod3,
        w_ret_out[0].astype(BF16).reshape(-1, YA_BLOCKS, d // YA_BLOCKS).transpose(1, 0, 2),
        w_conv_out[0].astype(BF16), w_mix_out[0].astype(BF16),
        dw, vec, wr_hi, wr_lo, br, seq, tm=min(512, seq))

    counts = cnt[0, N_GROUPS:N_GROUPS + N_EXPERTS].astype(I32)
    offs, meta = _expert_work_items(counts, n * EXPERT_TOP_K)
    eid = route_i[:, RT_E1:RT_E2 + 1]
    seg_start = jnp.sum(jnp.where(eid[:, :, None] == jnp.arange(N_EXPERTS, dtype=I32), offs[:N_EXPERTS], 0), axis=-1)
    pos = seg_start + route_i[:, RT_RANK1:RT_RANK2 + 1]
    pos_flat = pos.reshape(n * EXPERT_TOP_K)

    xs = _dispatch(pos_flat, t)
    ys = _experts(meta, xs, w_exp_gate, w_exp_up, w_exp_down)
    ln2 = jnp.concatenate([ln2_g, ln2_b, jnp.zeros((SUBLANES - 2, d), F32)], axis=0)
    out = _combine(pos_flat, ys, route_w, h1, mod3, ln2, seq)
    return out.reshape(b_, seq, d)
```

```python
import functools

import jax
import jax.numpy as jnp
import numpy as np
from jax import lax
from jax.experimental import pallas as pl
from jax.experimental.pallas import tpu as pltpu

GRID_W = 64
RET_HEADS = 8
RET_QK_DIM = 128
RET_V_DIM = 256
RET_CHUNK = 256
ROPE_BASE = 10000.0
CONV_WIDTH = 31
CONV_HALO = 16
SUBLANES = 8
N_GROUPS = 4
EXPERTS_PER_GROUP = 8
N_EXPERTS = N_GROUPS * EXPERTS_PER_GROUP
EXPERT_TOP_K = 2
LN_EPS = 1e-5
DEPTH = 1
ALPHA = (2.0 * DEPTH) ** 0.25
ROUTER_LANES = 128

V7X_VMEM_LIMIT = 56 * 1024 * 1024

F32 = jnp.float32
BF16 = jnp.bfloat16
I32 = jnp.int32


def _params(*sem):
    return pltpu.CompilerParams(dimension_semantics=sem, vmem_limit_bytes=V7X_VMEM_LIMIT)


def _sigmoid(v):
    return 0.5 * jnp.tanh(0.5 * v) + 0.5


def _split_bf16(v):
    hi = v.astype(BF16)
    return hi, (v - hi.astype(F32)).astype(BF16)


def _dot3(a, b_hi, b_lo):
    a_hi, a_lo = _split_bf16(a)
    return (jnp.dot(a_hi, b_hi, preferred_element_type=F32)
            + jnp.dot(a_lo, b_hi, preferred_element_type=F32)
            + jnp.dot(a_hi, b_lo, preferred_element_type=F32))


def _ln_rows(v, g, b):
    mu = jnp.mean(v, axis=-1, keepdims=True)
    d = v - mu
    var = jnp.mean(d * d, axis=-1, keepdims=True)
    return d * lax.rsqrt(var + LN_EPS) * g + b


ADA_K_TILE = 256


def _ada_kernel(cs_ref, w_ref, b_ref, out_ref):
    k = pl.program_id(0)
    s = cs_ref[...]
    s = s * _sigmoid(s)
    w_hi, w_lo = _split_bf16(w_ref[...])
    part = _dot3(s, w_hi, w_lo)

    @pl.when(k == 0)
    def _():
        out_ref[...] = part + b_ref[...]

    @pl.when(k > 0)
    def _():
        out_ref[...] += part


def _ada(cs, w_ada, b_ada):
    rows, d = cs.shape
    cols = w_ada.shape[1]
    tk = ADA_K_TILE
    return pl.pallas_call(
        _ada_kernel,
        grid=(d // tk,),
        in_specs=[pl.BlockSpec((rows, tk), lambda k: (0, k)),
                  pl.BlockSpec((tk, cols), lambda k: (k, 0)),
                  pl.BlockSpec((1, cols), lambda k: (0, 0))],
        out_specs=pl.BlockSpec((rows, cols), lambda k: (0, 0)),
        out_shape=jax.ShapeDtypeStruct((rows, cols), F32),
        compiler_params=_params("arbitrary"),
        name="ada",
    )(cs, w_ada, b_ada)


def _log_sigmoid(v):
    return jnp.minimum(v, 0.0) - jnp.log(1.0 + jnp.exp(-jnp.abs(v)))


def _ctx_kernel(ctx_ref, sh_ref, sc_ref, wk_ref, wv0_ref, wv1_ref, bk_ref, bv0_ref, bv1_ref, df_ref, db_ref,
                sf_ref, sb_ref):
    lc = ctx_ref.shape[1]
    u = (ctx_ref[0] * (1.0 + sc_ref[0]) + sh_ref[0]).astype(BF16)
    k = (jnp.dot(u, wk_ref[...], preferred_element_type=F32) + bk_ref[...]) * RET_QK_DIM ** -0.5
    v = jnp.concatenate(
        [(jnp.dot(u, w[...], preferred_element_type=F32) + b[...]).astype(BF16)
         for w, b in ((wv0_ref, bv0_ref), (wv1_ref, bv1_ref))], axis=1)
    pos = lax.broadcasted_iota(I32, (lc, 1), 0).astype(F32)
    dn = (((0,), (0,)), ((), ()))
    for h in range(RET_HEADS):
        lgf = _log_sigmoid(df_ref[h])
        lgb = _log_sigmoid(db_ref[h])
        kh = k[:, h * RET_QK_DIM:(h + 1) * RET_QK_DIM]
        vh = v[:, h * RET_V_DIM:(h + 1) * RET_V_DIM]
        kf = (kh * jnp.exp((lc - 1.0 - pos) * lgf)).astype(BF16)
        kb = (kh * jnp.exp(pos * lgb)).astype(BF16)
        sf_ref[0, h] = lax.dot_general(kf, vh, dn, preferred_element_type=F32)
        sb_ref[0, h] = lax.dot_general(kb, vh, dn, preferred_element_type=F32)


def _ctx_states(ctx, mod3, w_in3, b_in3, dec_f, dec_b, ctx_row):
    b_, lc, d = ctx.shape
    assert RET_HEADS * RET_QK_DIM == d and RET_HEADS * RET_V_DIM == 2 * d
    st = jax.ShapeDtypeStruct((b_, RET_HEADS, RET_QK_DIM, RET_V_DIM), F32)
    st_spec = pl.BlockSpec((1, RET_HEADS, RET_QK_DIM, RET_V_DIM), lambda b: (b, 0, 0, 0))
    wblk = lambda blk: pl.BlockSpec((d, d), lambda b: (0, blk))
    bblk = lambda blk: pl.BlockSpec((1, d), lambda b: (0, blk))
    dec_spec = pl.BlockSpec((RET_HEADS, 1, 1), lambda b: (0, 0, 0))
    return pl.pallas_call(
        _ctx_kernel,
        grid=(b_,),
        in_specs=[pl.BlockSpec((1, lc, d), lambda b: (b, 0, 0)),
                  pl.BlockSpec((1, 1, d), lambda b: (ctx_row, 0, 0)),
                  pl.BlockSpec((1, 1, d), lambda b: (ctx_row, 0, 1)),
                  wblk(PB_K), wblk(PB_V), wblk(PB_V + 1),
                  bblk(PB_K), bblk(PB_V), bblk(PB_V + 1),
                  dec_spec, dec_spec],
        out_specs=[st_spec, st_spec],
        out_shape=[st, st],
        compiler_params=_params("arbitrary"),
        name="ctx",
    )(ctx, mod3, mod3, w_in3, w_in3, w_in3, b_in3, b_in3, b_in3, dec_f, dec_b)


PROJ_BLOCKS = 9
PB_Q, PB_K, PB_V, PB_G, PB_Z, PB_GATE = 0, 1, 2, 4, 6, 7


def _proj_kernel(x_ref, sh_ref, sc_ref, w_ref, b_ref, cos_ref, sin_ref, out_ref, u_scr):
    j = pl.program_id(1)
    d = x_ref.shape[1]

    @pl.when(j == 0)
    def _():
        u_scr[...] = (x_ref[...] * (1.0 + sc_ref[0]) + sh_ref[0]).astype(BF16)

    def mm(blk):
        cols = slice(blk * d, (blk + 1) * d)
        return jnp.dot(u_scr[...], w_ref[:, cols], preferred_element_type=F32) + b_ref[:, cols]

    def rope(acc):
        cos = cos_ref[...]
        sin = sin_ref[...]
        for h in range(acc.shape[1] // RET_QK_DIM):
            sl = slice(h * RET_QK_DIM, (h + 1) * RET_QK_DIM)
            seg = acc[:, sl]
            out_ref[:, sl] = (seg * cos + pltpu.roll(seg, RET_QK_DIM // 2, 1) * sin).astype(BF16)

    def block(jj):
        if jj == PB_Q:
            rope(mm(PB_Q))
        elif jj == PB_K:
            rope(mm(PB_K) * RET_QK_DIM ** -0.5)
        elif jj < PB_G:
            out_ref[...] = mm(jj).astype(BF16)
        elif jj < PB_Z:
            acc = mm(jj)
            out_ref[...] = (acc * _sigmoid(acc)).astype(BF16)
        elif jj == PB_Z:
            out_ref[...] = (mm(PB_Z) * _sigmoid(mm(PB_Z + 1))).astype(BF16)
        else:
            out_ref[...] = _sigmoid(mm(jj + 1)).astype(BF16)

    for jj in range(PROJ_BLOCKS):
        pl.when(j == jj)(functools.partial(block, jj))


def _proj(x2, mod3, w_in3, b_in3, cos, sin, seq, tm):
    n, d = x2.shape
    tiles_per_seq = seq // tm
    return pl.pallas_call(
        _proj_kernel,
        grid=(n // tm, PROJ_BLOCKS),
        in_specs=[pl.BlockSpec((tm, d), lambda i, j: (i, 0)),
                  pl.BlockSpec((1, 1, d), lambda i, j: (i // tiles_per_seq, 0, 0)),
                  pl.BlockSpec((1, 1, d), lambda i, j: (i // tiles_per_seq, 0, 1)),
                  pl.BlockSpec(memory_space=pltpu.VMEM),
                  pl.BlockSpec(memory_space=pltpu.VMEM),
                  pl.BlockSpec((tm, RET_QK_DIM), lambda i, j: (i % tiles_per_seq, 0)),
                  pl.BlockSpec((tm, RET_QK_DIM), lambda i, j: (i % tiles_per_seq, 0))],
        out_specs=pl.BlockSpec((tm, d), lambda i, j: (i, j)),
        out_shape=jax.ShapeDtypeStruct((n, PROJ_BLOCKS * d), BF16),
        scratch_shapes=[pltpu.VMEM((tm, d), BF16)],
        compiler_params=_params("arbitrary", "arbitrary"),
        name="proj",
    )(x2, mod3, mod3, w_in3, b_in3, cos, sin)


def _ret_kernel(q_ref, k_ref, v_ref, g_ref, s0f_ref, s0b_ref, df_ref, db_ref, out_ref,
                sf_all, sb_all, p_scr, o_scr):
    c_ = RET_CHUNK
    n_chunks = q_ref.shape[1] // c_
    dn_t = (((0,), (0,)), ((), ()))
    lgf = _log_sigmoid(df_ref[0])
    lgb = _log_sigmoid(db_ref[0])
    ri = lax.broadcasted_iota(I32, (c_, c_), 0)
    ci = lax.broadcasted_iota(I32, (c_, c_), 1)
    diff = (ri - ci).astype(F32)
    mask = jnp.where(diff > 0, jnp.exp(diff * lgf), jnp.where(diff < 0, jnp.exp(-diff * lgb), 2.0))
    pos = lax.broadcasted_iota(I32, (c_, 1), 0).astype(F32)
    qdec_f = jnp.exp((pos + 1.0) * lgf)
    qdec_b = jnp.exp((c_ - pos) * lgb)
    kdec_f = jnp.exp((c_ - 1.0 - pos) * lgf)
    kdec_b = jnp.exp(pos * lgb)
    cdec_f = jnp.exp(c_ * lgf)
    cdec_b = jnp.exp(c_ * lgb)

    def chunk(c):
        return slice(c * c_, (c + 1) * c_)

    for c in range(n_chunks):
        k = k_ref[0, chunk(c), :].astype(F32)
        v = v_ref[0, chunk(c), :]
        sf_all[c] = lax.dot_general((k * kdec_f).astype(BF16), v, dn_t, preferred_element_type=F32)
        sb_all[c] = lax.dot_general((k * kdec_b).astype(BF16), v, dn_t, preferred_element_type=F32)

    sf = s0f_ref[0, 0]
    sb = s0b_ref[0, 0]
    for t in range(n_chunks):
        kv = sf_all[t]
        sf_all[t] = sf
        sf = sf * cdec_f + kv
        c = n_chunks - 1 - t
        kv = sb_all[c]
        sb_all[c] = sb
        sb = sb * cdec_b + kv

    for c in range(n_chunks):
        s = lax.dot_general(q_ref[0, chunk(c), :], k_ref[0, chunk(c), :], (((1,), (1,)), ((), ())),
                            preferred_element_type=F32)
        p_scr[c] = (s * mask).astype(BF16)

    for c in range(n_chunks):
        q = q_ref[0, chunk(c), :]
        o = jnp.dot(p_scr[c], v_ref[0, chunk(c), :], preferred_element_type=F32)
        o = o + qdec_f * jnp.dot(q, sf_all[c].astype(BF16), preferred_element_type=F32)
        o = o + qdec_b * jnp.dot(q, sb_all[c].astype(BF16), preferred_element_type=F32)
        o_scr[chunk(c), :] = o

    for c in range(n_chunks):
        o = o_scr[chunk(c), :]
        mu = jnp.mean(o, axis=-1, keepdims=True)
        d = o - mu
        var = jnp.mean(d * d, axis=-1, keepdims=True)
        on = d * lax.rsqrt(var + LN_EPS)
        out_ref[0, chunk(c), :] = (on * g_ref[0, chunk(c), :].astype(F32)).astype(BF16)


def _retention(proj3, s0f, s0b, dec_f, dec_b, d):
    b_, seq, _ = proj3.shape
    qb, kb = PB_Q * d // RET_QK_DIM, PB_K * d // RET_QK_DIM
    vb, gb = PB_V * d // RET_V_DIM, PB_G * d // RET_V_DIM
    n_chunks = seq // RET_CHUNK
    st_spec = pl.BlockSpec((1, 1, RET_QK_DIM, RET_V_DIM), lambda b, h: (b, h, 0, 0))
    st_all = pltpu.VMEM((n_chunks, RET_QK_DIM, RET_V_DIM), F32)
    return pl.pallas_call(
        _ret_kernel,
        grid=(b_, RET_HEADS),
        in_specs=[pl.BlockSpec((1, seq, RET_QK_DIM), lambda b, h: (b, 0, qb + h)),
                  pl.BlockSpec((1, seq, RET_QK_DIM), lambda b, h: (b, 0, kb + h)),
                  pl.BlockSpec((1, seq, RET_V_DIM), lambda b, h: (b, 0, vb + h)),
                  pl.BlockSpec((1, seq, RET_V_DIM), lambda b, h: (b, 0, gb + h)),
                  st_spec, st_spec,
                  pl.BlockSpec((1, 1, 1), lambda b, h: (h, 0, 0)),
                  pl.BlockSpec((1, 1, 1), lambda b, h: (h, 0, 0))],
        out_specs=pl.BlockSpec((1, seq, RET_V_DIM), lambda b, h: (b, 0, h)),
        out_shape=jax.ShapeDtypeStruct((b_, seq, RET_HEADS * RET_V_DIM), BF16),
        scratch_shapes=[st_all, st_all,
                        pltpu.VMEM((n_chunks, RET_CHUNK, RET_CHUNK), BF16),
                        pltpu.VMEM((seq, RET_V_DIM), F32)],
        compiler_params=_params("arbitrary", "arbitrary"),
        name="ret",
    )(proj3, proj3, proj3, proj3, s0f, s0b, dec_f, dec_b)


VEC_CONV_B, VEC_CLN_G, VEC_CLN_B, VEC_BCONV, VEC_BMIX, VEC_LN1_G, VEC_LN1_B = range(7)
CONV_ROWS = 64
YA_BLOCKS = 4
LANES = 128
RT_E1, RT_E2, RT_RANK1, RT_RANK2 = 0, 1, 2, 3
RT_W1, RT_W2 = 0, 1


def _merge_kernel(ret_ref, z_ref, zp_ref, zn_ref, ga_ref, gb_ref, x_ref, gt_ref, shf_ref, scf_ref,
                  wret_ref, wconv_ref, wmix_ref, dw_ref, vec_ref, wrh_ref, wrl_ref, br_ref,
                  h1_ref, t_ref, ri_ref, rw_ref, cnt_ref, zext, zc_scr, cnt_scr, ya_scr, *, tiles_per_seq):
    tm, d = x_ref.shape
    step = pl.program_id(0)
    ti = step % tiles_per_seq
    vec = vec_ref[...]

    def row(r):
        return vec[r:r + 1, :]


    has_prev = (ti > 0).astype(F32)
    has_next = (ti < tiles_per_seq - 1).astype(F32)
    zext[0:CONV_HALO, :] = zp_ref[...].astype(F32) * has_prev
    zext[CONV_HALO:CONV_HALO + tm, :] = z_ref[...].astype(F32)
    zext[CONV_HALO + tm:, :] = zn_ref[...].astype(F32) * has_next
    base = CONV_HALO - CONV_WIDTH // 2
    win_rows = CONV_ROWS + 2 * CONV_HALO

    n_lt = d // LANES
    kc = ret_ref.shape[1] // n_lt

    def conv_piece(r0, lt):
        ls = slice(lt * LANES, (lt + 1) * LANES)
        win = zext[pl.ds(r0, win_rows), ls]
        acc = jnp.zeros((CONV_ROWS, LANES), F32) + vec_ref[VEC_CONV_B:VEC_CONV_B + 1, ls]
        for s in range(SUBLANES):
            taps = [w for w in range(CONV_WIDTH) if (base + w) % SUBLANES == s]
            if not taps:
                continue
            sh = win if s == 0 else pltpu.roll(win, win_rows - s, 0)
            for w in taps:
                a = (base + w) - s
                acc = acc + sh[a:a + CONV_ROWS, :] * dw_ref[w:w + 1, ls]
        return acc

    def conv_rows(r, carry):
        r0 = pl.multiple_of(r * (2 * CONV_ROWS), 2 * CONV_ROWS)
        ya = None
        accs = []
        for lt in range(n_lt):
            part = jnp.dot(ret_ref[:, lt * kc:(lt + 1) * kc], wret_ref[r, lt * kc:(lt + 1) * kc, :],
                           preferred_element_type=F32)
            ya = part if ya is None else ya + part
            accs.append((conv_piece(r0, lt), conv_piece(r0 + CONV_ROWS, lt)))
        ya_scr[r] = ya
        for lt in range(n_lt):
            ls = slice(lt * LANES, (lt + 1) * LANES)
            zc_scr[pl.ds(r0, CONV_ROWS), ls] = accs[lt][0]
            zc_scr[pl.ds(r0 + CONV_ROWS, CONV_ROWS), ls] = accs[lt][1]
        return carry

    lax.fori_loop(0, YA_BLOCKS, conv_rows, 0)
    y_a = jnp.concatenate([ya_scr[q] for q in range(YA_BLOCKS)], axis=1)
    zc = _ln_rows(zc_scr[...], row(VEC_CLN_G), row(VEC_CLN_B))
    zc = zc * _sigmoid(zc)
    y_b = jnp.dot(zc.astype(BF16), wconv_ref[...], preferred_element_type=F32) + row(VEC_BCONV)

    mixed = ga_ref[...].astype(F32) * y_a + gb_ref[...].astype(F32) * y_b
    mix = jnp.dot(mixed.astype(BF16), wmix_ref[...], preferred_element_type=F32) + row(VEC_BMIX)
    h1 = _ln_rows(ALPHA * x_ref[...] + gt_ref[0] * mix, row(VEC_LN1_G), row(VEC_LN1_B))
    h1_ref[...] = h1
    t = h1 * (1.0 + scf_ref[0]) + shf_ref[0]
    t_ref[...] = t

    logits = _dot3(t, wrh_ref[...], wrl_ref[...]) + br_ref[...]

    lane = lax.broadcasted_iota(I32, logits.shape, 1).astype(F32)
    neg = -jnp.inf
    big = float(ROUTER_LANES)
    is_grp = lane < N_GROUPS
    gl = jnp.where(is_grp, logits, neg)
    gmax = jnp.max(gl, axis=-1, keepdims=True)
    gidx = jnp.min(jnp.where(gl == gmax, lane, big), axis=-1, keepdims=True)
    gsum = jnp.sum(jnp.where(is_grp, jnp.exp(gl - gmax), 0.0), axis=-1, keepdims=True)
    grp_w = 1.0 / gsum
    lo = N_GROUPS + gidx * EXPERTS_PER_GROUP
    el = jnp.where(lane >= lo, jnp.where(lane < lo + EXPERTS_PER_GROUP, logits, neg), neg)
    m1 = jnp.max(el, axis=-1, keepdims=True)
    i1 = jnp.min(jnp.where(el == m1, lane, big), axis=-1, keepdims=True)
    el2 = jnp.where(lane == i1, neg, el)
    m2 = jnp.max(el2, axis=-1, keepdims=True)
    i2 = jnp.min(jnp.where(el2 == m2, lane, big), axis=-1, keepdims=True)
    r = jnp.exp(m2 - m1)
    w1 = grp_w / (1.0 + r)
    w2 = grp_w * r / (1.0 + r)

    @pl.when(step == 0)
    def _():
        cnt_scr[...] = jnp.zeros_like(cnt_scr)

    oh1 = lane == i1
    oh2 = lane == i2
    oh = jnp.where(oh1, 1.0, jnp.where(oh2, 1.0, 0.0))
    tri = jnp.where(lax.broadcasted_iota(I32, (tm, tm), 0) > lax.broadcasted_iota(I32, (tm, tm), 1), 1.0, 0.0)
    before = jnp.dot(tri.astype(BF16), oh.astype(BF16), preferred_element_type=F32) + cnt_scr[0:1, :]
    rank1 = jnp.sum(jnp.where(oh1, before, 0.0), axis=-1, keepdims=True)
    rank2 = jnp.sum(jnp.where(oh2, before, 0.0), axis=-1, keepdims=True)
    cnt = cnt_scr[0:1, :] + jnp.sum(oh, axis=0, keepdims=True)
    cnt_scr[...] = jnp.broadcast_to(cnt, cnt_scr.shape)
    cnt_ref[...] = jnp.broadcast_to(cnt, cnt_ref.shape)

    e1 = i1 - N_GROUPS
    e2 = i2 - N_GROUPS
    ri = jnp.where(lane == RT_E1, e1, jnp.where(lane == RT_E2, e2,
                   jnp.where(lane == RT_RANK1, rank1, jnp.where(lane == RT_RANK2, rank2, 0.0))))
    ri_ref[...] = ri.astype(I32)
    rw_ref[...] = jnp.where(lane == RT_W1, w1, jnp.where(lane == RT_W2, w2, 0.0))


def _merge(ret2, proj2, x2, mod3, wret, wconv, wmix, dw, vec, wrh, wrl, br, seq, tm):
    n, d = x2.shape
    assert tm == YA_BLOCKS * 2 * CONV_ROWS
    tiles_per_seq = seq // tm
    hb = tm // CONV_HALO
    last_hb = n // CONV_HALO - 1
    const = lambda i: (0, 0)
    bat = lambda k: (lambda i: (i // tiles_per_seq, 0, k))
    return pl.pallas_call(
        functools.partial(_merge_kernel, tiles_per_seq=tiles_per_seq),
        grid=(n // tm,),
        in_specs=[pl.BlockSpec((tm, ret2.shape[1]), lambda i: (i, 0)),
                  pl.BlockSpec((tm, d), lambda i: (i, PB_Z)),
                  pl.BlockSpec((CONV_HALO, d), lambda i: (jnp.maximum(i * hb - 1, 0), PB_Z)),
                  pl.BlockSpec((CONV_HALO, d), lambda i: (jnp.minimum((i + 1) * hb, last_hb), PB_Z)),
                  pl.BlockSpec((tm, d), lambda i: (i, PB_GATE)),
                  pl.BlockSpec((tm, d), lambda i: (i, PB_GATE + 1)),
                  pl.BlockSpec((tm, d), lambda i: (i, 0)),
                  pl.BlockSpec((1, 1, d), bat(2)),
                  pl.BlockSpec((1, 1, d), bat(3)),
                  pl.BlockSpec((1, 1, d), bat(4)),
                  pl.BlockSpec(wret.shape, lambda i: (0, 0, 0)),
                  pl.BlockSpec(wconv.shape, const),
                  pl.BlockSpec(wmix.shape, const),
                  pl.BlockSpec(dw.shape, const),
                  pl.BlockSpec(vec.shape, const),
                  pl.BlockSpec(wrh.shape, const),
                  pl.BlockSpec(wrl.shape, const),
                  pl.BlockSpec(br.shape, const)],
        out_specs=[pl.BlockSpec((tm, d), lambda i: (i, 0)),
                   pl.BlockSpec((tm, d), lambda i: (i, 0)),
                   pl.BlockSpec((tm, ROUTER_LANES), lambda i: (i, 0)),
                   pl.BlockSpec((tm, ROUTER_LANES), lambda i: (i, 0)),
                   pl.BlockSpec((SUBLANES, ROUTER_LANES), const)],
        out_shape=[jax.ShapeDtypeStruct((n, d), F32),
                   jax.ShapeDtypeStruct((n, d), F32),
                   jax.ShapeDtypeStruct((n, ROUTER_LANES), I32),
                   jax.ShapeDtypeStruct((n, ROUTER_LANES), F32),
                   jax.ShapeDtypeStruct((SUBLANES, ROUTER_LANES), F32)],
        scratch_shapes=[pltpu.VMEM((tm + 2 * CONV_HALO, d), F32),
                        pltpu.VMEM((tm, d), F32),
                        pltpu.VMEM((SUBLANES, ROUTER_LANES), F32),
                        pltpu.VMEM((YA_BLOCKS, tm, d // YA_BLOCKS), F32)],
        compiler_params=_params("arbitrary"),
        name="merge",
    )(ret2, proj2, proj2, proj2, proj2, proj2, x2, mod3, mod3, mod3,
      wret, wconv, wmix, dw, vec, wrh, wrl, br)


DISPATCH_TILE = 512


def _dispatch_kernel(pos_ref, t_ref, xs_hbm, sem):
    tm = t_ref.shape[0]
    base = pl.program_id(0) * tm

    for r in range(tm):
        for k in range(EXPERT_TOP_K):
            dst = pos_ref[(base + r) * EXPERT_TOP_K + k]
            pltpu.make_async_copy(t_ref.at[pl.ds(r, 1)], xs_hbm.at[pl.ds(dst, 1)], sem).start()
    for k in range(EXPERT_TOP_K):
        pltpu.make_async_copy(t_ref, xs_hbm.at[pl.ds(0, tm)], sem).wait()


def _dispatch(pos_flat, t2):
    n, d = t2.shape
    tm = min(DISPATCH_TILE, n)
    return pl.pallas_call(
        _dispatch_kernel,
        grid=(n // tm,),
        in_specs=[pl.BlockSpec(memory_space=pltpu.SMEM),
                  pl.BlockSpec((tm, d), lambda i: (i, 0))],
        out_specs=pl.BlockSpec(memory_space=pl.ANY),
        out_shape=jax.ShapeDtypeStruct((n * EXPERT_TOP_K, d), F32),
        scratch_shapes=[pltpu.SemaphoreType.DMA],
        compiler_params=_params("arbitrary"),
        name="dispatch",
    )(pos_flat, t2)


EXPERT_TILE = 512


def _experts_kernel(ie_ref, it_ref, lo_ref, hi_ref, nv_ref, xs_ref, wg_ref, wu_ref, wd_ref, ys_ref,
                    wg_bf, wu_bf, wd_bf):
    k = pl.program_id(0)
    prev = jnp.maximum(k - 1, 0)
    new_expert = (k == 0) | (ie_ref[k] != ie_ref[prev])
    new_tile = (k == 0) | (it_ref[k] != it_ref[prev])

    @pl.when(new_expert)
    def _():
        wg_bf[...] = wg_ref[0, 0].astype(BF16)
        wu_bf[...] = wu_ref[0, 0].astype(BF16)
        wd_bf[...] = wd_ref[0, 0].astype(BF16)

    @pl.when(k < nv_ref[0])
    def _():
        x = xs_ref[...].astype(BF16)
        g = jnp.dot(x, wg_bf[...], preferred_element_type=F32)
        u = jnp.dot(x, wu_bf[...], preferred_element_type=F32)
        hid = (g * _sigmoid(g) * u).astype(BF16)
        y = jnp.dot(hid, wd_bf[...], preferred_element_type=F32)
        rows = lax.broadcasted_iota(I32, (xs_ref.shape[0], 1), 0)
        mine = (rows >= lo_ref[k]) & (rows < hi_ref[k])

        @pl.when(new_tile)
        def _():
            ys_ref[...] = jnp.where(mine, y, 0.0)

        @pl.when(jnp.logical_not(new_tile))
        def _():
            ys_ref[...] = jnp.where(mine, y, ys_ref[...])


def _experts(meta, xs, w_gate, w_up, w_down):
    p, d = xs.shape
    ff = w_gate.shape[-1]
    n_items = meta[0].shape[0]
    grid_spec = pltpu.PrefetchScalarGridSpec(
        num_scalar_prefetch=5,
        grid=(n_items,),
        in_specs=[pl.BlockSpec((EXPERT_TILE, d), lambda k, ie, it, lo, hi, nv: (it[k], 0)),
                  pl.BlockSpec((1, 1, d, ff), lambda k, ie, it, lo, hi, nv: (0, ie[k], 0, 0)),
                  pl.BlockSpec((1, 1, d, ff), lambda k, ie, it, lo, hi, nv: (0, ie[k], 0, 0)),
                  pl.BlockSpec((1, 1, ff, d), lambda k, ie, it, lo, hi, nv: (0, ie[k], 0, 0))],
        out_specs=pl.BlockSpec((EXPERT_TILE, d), lambda k, ie, it, lo, hi, nv: (it[k], 0)),
        scratch_shapes=[pltpu.VMEM((d, ff), BF16), pltpu.VMEM((d, ff), BF16), pltpu.VMEM((ff, d), BF16)])
    return pl.pallas_call(
        _experts_kernel,
        grid_spec=grid_spec,
        out_shape=jax.ShapeDtypeStruct((p, d), F32),
        compiler_params=_params("arbitrary"),
        name="experts",
    )(*meta, xs, w_gate, w_up, w_down)


def _expert_work_items(counts, p):
    n_tiles = p // EXPERT_TILE
    n_items = n_tiles + N_EXPERTS - 1
    offs = jnp.concatenate([jnp.zeros((1,), I32), jnp.cumsum(counts)])
    first = offs[:-1] // EXPERT_TILE
    last = (offs[1:] - 1) // EXPERT_TILE
    per = jnp.where(counts > 0, last - first + 1, 0)
    ends = jnp.cumsum(per)
    starts = ends - per
    n_valid = ends[-1]
    k = jnp.arange(n_items, dtype=I32)
    kk = jnp.minimum(k, n_valid - 1)
    ie = jnp.sum((ends[None, :] <= kk[:, None]).astype(I32), axis=1)
    it = first[ie] + (kk - starts[ie])
    lo = jnp.maximum(offs[ie], it * EXPERT_TILE) - it * EXPERT_TILE
    hi = jnp.minimum(offs[ie + 1], (it + 1) * EXPERT_TILE) - it * EXPERT_TILE
    valid = k < n_valid
    lo = jnp.where(valid, lo, 0)
    hi = jnp.where(valid, hi, 0)
    return offs, (ie.astype(I32), it.astype(I32), lo.astype(I32), hi.astype(I32), n_valid.reshape(1).astype(I32))


COMBINE_HALF = 256


def _combine_kernel(pos_ref, ys_hbm, rw_ref, h1_ref, gt_ref, ln_ref, out_ref, buf, sem):
    i = pl.program_id(0)
    n_steps = pl.num_programs(0)
    hm = COMBINE_HALF
    base = i * (2 * hm)

    def slot_copy(s, k):
        return pltpu.make_async_copy(ys_hbm.at[pl.ds(0, hm)], buf.at[s, k], sem.at[s])

    def gather(first_tok, s):
        for r in range(hm):
            for k in range(EXPERT_TOP_K):
                src = pos_ref[(first_tok + r) * EXPERT_TOP_K + k]
                pltpu.make_async_copy(ys_hbm.at[pl.ds(src, 1)], buf.at[s, k, pl.ds(r, 1)], sem.at[s]).start()

    def finish(s):
        for k in range(EXPERT_TOP_K):
            slot_copy(s, k).wait()
        rows = slice(s * hm, (s + 1) * hm)
        rw = rw_ref[rows, :]
        y = rw[:, RT_W1:RT_W1 + 1] * buf[s, 0] + rw[:, RT_W2:RT_W2 + 1] * buf[s, 1]
        out_ref[rows, :] = _ln_rows(ALPHA * h1_ref[rows, :] + gt_ref[0] * y, ln_ref[0:1, :], ln_ref[1:2, :])

    @pl.when(i == 0)
    def _():
        gather(0, 0)

    gather(base + hm, 1)
    finish(0)

    @pl.when(i + 1 < n_steps)
    def _():
        gather(base + 2 * hm, 0)

    finish(1)


def _combine(pos_flat, ys, rw, h1, mod3, ln2, seq):
    n, d = h1.shape
    tm = 2 * COMBINE_HALF
    tiles_per_seq = seq // tm
    return pl.pallas_call(
        _combine_kernel,
        grid=(n // tm,),
        in_specs=[pl.BlockSpec(memory_space=pltpu.SMEM),
                  pl.BlockSpec(memory_space=pl.ANY),
                  pl.BlockSpec((tm, ROUTER_LANES), lambda i: (i, 0)),
                  pl.BlockSpec((tm, d), lambda i: (i, 0)),
                  pl.BlockSpec((1, 1, d), lambda i: (i // tiles_per_seq, 0, 5)),
                  pl.BlockSpec(ln2.shape, lambda i: (0, 0))],
        out_specs=pl.BlockSpec((tm, d), lambda i: (i, 0)),
        out_shape=jax.ShapeDtypeStruct((n, d), F32),
        scratch_shapes=[pltpu.VMEM((2, EXPERT_TOP_K, COMBINE_HALF, d), F32),
                        pltpu.SemaphoreType.DMA((2,))],
        compiler_params=_params("arbitrary"),
        name="combine",
    )(pos_flat, ys, rw, h1, mod3, ln2)


def _rope_tables(seq):
    pos = np.arange(seq)
    r = (pos // GRID_W).astype(np.float32)
    col = (pos % GRID_W).astype(np.float32)
    n_freq = RET_QK_DIM // 4
    inv = (np.float32(ROPE_BASE) ** (-np.arange(n_freq, dtype=np.float32) / np.float32(n_freq))).astype(np.float32)
    ang = np.concatenate([r[:, None] * inv, col[:, None] * inv], axis=-1)
    ang = np.concatenate([ang, ang], axis=-1).astype(np.float32)
    sign = np.concatenate([-np.ones((RET_QK_DIM // 2,), np.float32), np.ones((RET_QK_DIM // 2,), np.float32)])
    return jnp.asarray(np.cos(ang), F32), jnp.asarray(np.sin(ang) * sign, F32)


def kernel(x, c, ctx, c_ctx, w_ada, b_ada, w_in, b_in, ret_decay_fwd, ret_decay_bwd, w_ret_out, conv_dw, conv_dw_b, conv_ln_g, conv_ln_b, w_conv_out, b_conv_out, w_mix_out, b_mix_out, ln1_g, ln1_b, w_router_grp, b_router_grp, w_router_exp, b_router_exp, w_exp_gate, w_exp_up, w_exp_down, ln2_g, ln2_b):
    b_, seq, d = x.shape
    n = b_ * seq
    assert w_ada.shape[0] == DEPTH
    mod_rows = SUBLANES
    assert b_ + 1 <= mod_rows

    cs = jnp.concatenate([c, c_ctx[None, :], jnp.zeros((mod_rows - b_ - 1, d), F32)], axis=0)
    mod = _ada(cs, w_ada[0], b_ada[0][None, :])
    mod3 = mod.reshape(mod_rows, 1, 6 * d)

    w_in3 = w_in[0].astype(BF16)
    b_in3 = b_in[0][None, :]
    dec_f = ret_decay_fwd[0].reshape(RET_HEADS, 1, 1)
    dec_b = ret_decay_bwd[0].reshape(RET_HEADS, 1, 1)

    s0f, s0b = _ctx_states(ctx, mod3, w_in3, b_in3, dec_f, dec_b, b_)

    cos, sin = _rope_tables(seq)
    x2 = x.reshape(n, d)
    proj = _proj(x2, mod3, w_in3, b_in3, cos, sin, seq, tm=min(1024, seq))

    ret = _retention(proj.reshape(b_, seq, PROJ_BLOCKS * d), s0f, s0b, dec_f, dec_b, d)

    vec = jnp.concatenate([conv_dw_b, conv_ln_g, conv_ln_b, b_conv_out, b_mix_out, ln1_g, ln1_b,
                           jnp.zeros((1, d), F32)], axis=0)
    dw = jnp.concatenate([conv_dw[0], jnp.zeros((32 - CONV_WIDTH, d), F32)], axis=0)
    pad = ROUTER_LANES - N_GROUPS - N_EXPERTS
    wr = jnp.concatenate([w_router_grp[0], w_router_exp[0], jnp.zeros((d, pad), F32)], axis=1)
    wr_hi = wr.astype(BF16)
    wr_lo = (wr - wr_hi.astype(F32)).astype(BF16)
    br = jnp.concatenate([b_router_grp[0], b_router_exp[0], jnp.zeros((pad,), F32)])[None, :]
    h1, t, route_i, route_w, cnt = _merge(
        ret.reshape(n, RET_HEADS * RET_V_DIM), proj, x2, mod3,
        w_ret_out[0].astype(BF16).reshape(-1, YA_BLOCKS, d // YA_BLOCKS).transpose(1, 0, 2),
        w_conv_out[0].astype(BF16), w_mix_out[0].astype(BF16),
        dw, vec, wr_hi, wr_lo, br, seq, tm=min(512, seq))

    counts = cnt[0, N_GROUPS:N_GROUPS + N_EXPERTS].astype(I32)
    offs, meta = _expert_work_items(counts, n * EXPERT_TOP_K)
    eid = route_i[:, RT_E1:RT_E2 + 1]
    seg_start = jnp.sum(jnp.where(eid[:, :, None] == jnp.arange(N_EXPERTS, dtype=I32), offs[:N_EXPERTS], 0), axis=-1)
    pos = seg_start + route_i[:, RT_RANK1:RT_RANK2 + 1]
    pos_flat = pos.reshape(n * EXPERT_TOP_K)

    xs = _dispatch(pos_flat, t)
    ys = _experts(meta, xs, w_exp_gate, w_exp_up, w_exp_down)
    ln2 = jnp.concatenate([ln2_g, ln2_b, jnp.zeros((SUBLANES - 2, d), F32)], axis=0)
    out = _combine(pos_flat, ys, route_w, h1, mod3, ln2, seq)
    return out.reshape(b_, seq, d)
```

```python
import functools

import jax
import jax.numpy as jnp
import numpy as np
from jax import lax
from jax.experimental import pallas as pl
from jax.experimental.pallas import tpu as pltpu

GRID_W = 64
RET_HEADS = 8
RET_QK_DIM = 128
RET_V_DIM = 256
RET_CHUNK = 256
ROPE_BASE = 10000.0
CONV_WIDTH = 31
CONV_HALO = 16
SUBLANES = 8
LANES = 128
N_GROUPS = 4
EXPERTS_PER_GROUP = 8
N_EXPERTS = N_GROUPS * EXPERTS_PER_GROUP
EXPERT_TOP_K = 2
LN_EPS = 1e-5
DEPTH = 1
ALPHA = (2.0 * DEPTH) ** 0.25
ROUTER_LANES = 128

V7X_VMEM_LIMIT = 56 * 1024 * 1024

F32 = jnp.float32
BF16 = jnp.bfloat16
I32 = jnp.int32


def _params(*sem):
    return pltpu.CompilerParams(dimension_semantics=sem, vmem_limit_bytes=V7X_VMEM_LIMIT)


def _sigmoid(v):
    return 0.5 * jnp.tanh(0.5 * v) + 0.5


def _split_bf16(v):
    hi = v.astype(BF16)
    return hi, (v - hi.astype(F32)).astype(BF16)


def _dot3(a, b_hi, b_lo):
    a_hi, a_lo = _split_bf16(a)
    return (jnp.dot(a_hi, b_hi, preferred_element_type=F32)
            + jnp.dot(a_lo, b_hi, preferred_element_type=F32)
            + jnp.dot(a_hi, b_lo, preferred_element_type=F32))


def _load_token_tiles(ref, n_tok):
    return jnp.concatenate([ref[pl.ds(j, n_tok, stride=SUBLANES), :] for j in range(SUBLANES)], axis=1)


def _store_token_tiles(ref, val):
    for j in range(SUBLANES):
        ref[pl.ds(j, val.shape[0], stride=SUBLANES), :] = val[:, j * LANES:(j + 1) * LANES]


def _ln_rows(v, g, b):
    mu = jnp.mean(v, axis=-1, keepdims=True)
    d = v - mu
    var = jnp.mean(d * d, axis=-1, keepdims=True)
    return d * lax.rsqrt(var + LN_EPS) * g + b


ADA_K_TILE = 256


def _ada_kernel(cs_ref, w_ref, b_ref, out_ref):
    k = pl.program_id(0)
    s = cs_ref[...]
    s = s * _sigmoid(s)
    w_hi, w_lo = _split_bf16(w_ref[...])
    part = _dot3(s, w_hi, w_lo)

    @pl.when(k == 0)
    def _():
        out_ref[...] = part + b_ref[...]

    @pl.when(k > 0)
    def _():
        out_ref[...] += part


def _ada(cs, w_ada, b_ada):
    rows, d = cs.shape
    cols = w_ada.shape[1]
    tk = ADA_K_TILE
    return pl.pallas_call(
        _ada_kernel,
        grid=(d // tk,),
        in_specs=[pl.BlockSpec((rows, tk), lambda k: (0, k)),
                  pl.BlockSpec((tk, cols), lambda k: (k, 0)),
                  pl.BlockSpec((1, cols), lambda k: (0, 0))],
        out_specs=pl.BlockSpec((rows, cols), lambda k: (0, 0)),
        out_shape=jax.ShapeDtypeStruct((rows, cols), F32),
        compiler_params=_params("arbitrary"),
        name="ada",
    )(cs, w_ada, b_ada)


def _log_sigmoid(v):
    return jnp.minimum(v, 0.0) - jnp.log(1.0 + jnp.exp(-jnp.abs(v)))


def _ctx_kernel(ctx_ref, sh_ref, sc_ref, wk_ref, wv0_ref, wv1_ref, bk_ref, bv0_ref, bv1_ref, df_ref, db_ref,
                sf_ref, sb_ref):
    lc = ctx_ref.shape[1]
    u = (ctx_ref[0] * (1.0 + sc_ref[0]) + sh_ref[0]).astype(BF16)
    k = (jnp.dot(u, wk_ref[...], preferred_element_type=F32) + bk_ref[...]) * RET_QK_DIM ** -0.5
    v = jnp.concatenate(
        [(jnp.dot(u, w[...], preferred_element_type=F32) + b[...]).astype(BF16)
         for w, b in ((wv0_ref, bv0_ref), (wv1_ref, bv1_ref))], axis=1)
    pos = lax.broadcasted_iota(I32, (lc, 1), 0).astype(F32)
    dn = (((0,), (0,)), ((), ()))
    for h in range(RET_HEADS):
        lgf = _log_sigmoid(df_ref[h])
        lgb = _log_sigmoid(db_ref[h])
        kh = k[:, h * RET_QK_DIM:(h + 1) * RET_QK_DIM]
        vh = v[:, h * RET_V_DIM:(h + 1) * RET_V_DIM]
        kf = (kh * jnp.exp((lc - 1.0 - pos) * lgf)).astype(BF16)
        kb = (kh * jnp.exp(pos * lgb)).astype(BF16)
        sf_ref[0, h] = lax.dot_general(kf, vh, dn, preferred_element_type=F32)
        sb_ref[0, h] = lax.dot_general(kb, vh, dn, preferred_element_type=F32)


def _ctx_states(ctx, mod3, w_in3, b_in3, dec_f, dec_b, ctx_row):
    b_, lc, d = ctx.shape
    assert RET_HEADS * RET_QK_DIM == d and RET_HEADS * RET_V_DIM == 2 * d
    st = jax.ShapeDtypeStruct((b_, RET_HEADS, RET_QK_DIM, RET_V_DIM), F32)
    st_spec = pl.BlockSpec((1, RET_HEADS, RET_QK_DIM, RET_V_DIM), lambda b: (b, 0, 0, 0))
    wblk = lambda blk: pl.BlockSpec((d, d), lambda b: (0, blk))
    bblk = lambda blk: pl.BlockSpec((1, d), lambda b: (0, blk))
    dec_spec = pl.BlockSpec((RET_HEADS, 1, 1), lambda b: (0, 0, 0))
    return pl.pallas_call(
        _ctx_kernel,
        grid=(b_,),
        in_specs=[pl.BlockSpec((1, lc, d), lambda b: (b, 0, 0)),
                  pl.BlockSpec((1, 1, d), lambda b: (ctx_row, 0, 0)),
                  pl.BlockSpec((1, 1, d), lambda b: (ctx_row, 0, 1)),
                  wblk(PB_K), wblk(PB_V), wblk(PB_V + 1),
                  bblk(PB_K), bblk(PB_V), bblk(PB_V + 1),
                  dec_spec, dec_spec],
        out_specs=[st_spec, st_spec],
        out_shape=[st, st],
        compiler_params=_params("arbitrary"),
        name="ctx",
    )(ctx, mod3, mod3, w_in3, w_in3, w_in3, b_in3, b_in3, b_in3, dec_f, dec_b)


PROJ_BLOCKS = 9
PB_Q, PB_K, PB_V, PB_G, PB_Z, PB_GATE = 0, 1, 2, 4, 6, 7


def _proj_kernel(x_ref, sh_ref, sc_ref, w_ref, b_ref, cos_ref, sin_ref, out_ref, u_scr):
    j = pl.program_id(1)
    d = x_ref.shape[1]

    @pl.when(j == 0)
    def _():
        u_scr[...] = (x_ref[...] * (1.0 + sc_ref[0]) + sh_ref[0]).astype(BF16)

    def mm(blk):
        cols = slice(blk * d, (blk + 1) * d)
        return jnp.dot(u_scr[...], w_ref[:, cols], preferred_element_type=F32) + b_ref[:, cols]

    def rope(acc):
        cos = cos_ref[...]
        sin = sin_ref[...]
        for h in range(acc.shape[1] // RET_QK_DIM):
            sl = slice(h * RET_QK_DIM, (h + 1) * RET_QK_DIM)
            seg = acc[:, sl]
            out_ref[:, sl] = (seg * cos + pltpu.roll(seg, RET_QK_DIM // 2, 1) * sin).astype(BF16)

    def block(jj):
        if jj == PB_Q:
            rope(mm(PB_Q))
        elif jj == PB_K:
            rope(mm(PB_K) * RET_QK_DIM ** -0.5)
        elif jj < PB_G:
            out_ref[...] = mm(jj).astype(BF16)
        elif jj < PB_Z:
            acc = mm(jj)
            out_ref[...] = (acc * _sigmoid(acc)).astype(BF16)
        elif jj == PB_Z:
            out_ref[...] = (mm(PB_Z) * _sigmoid(mm(PB_Z + 1))).astype(BF16)
        else:
            out_ref[...] = _sigmoid(mm(jj + 1)).astype(BF16)

    for jj in range(PROJ_BLOCKS):
        pl.when(j == jj)(functools.partial(block, jj))


def _proj(x2, mod3, w_in3, b_in3, cos, sin, seq, tm):
    n, d = x2.shape
    tiles_per_seq = seq // tm
    return pl.pallas_call(
        _proj_kernel,
        grid=(n // tm, PROJ_BLOCKS),
        in_specs=[pl.BlockSpec((tm, d), lambda i, j: (i, 0)),
                  pl.BlockSpec((1, 1, d), lambda i, j: (i // tiles_per_seq, 0, 0)),
                  pl.BlockSpec((1, 1, d), lambda i, j: (i // tiles_per_seq, 0, 1)),
                  pl.BlockSpec(memory_space=pltpu.VMEM),
                  pl.BlockSpec(memory_space=pltpu.VMEM),
                  pl.BlockSpec((tm, RET_QK_DIM), lambda i, j: (i % tiles_per_seq, 0)),
                  pl.BlockSpec((tm, RET_QK_DIM), lambda i, j: (i % tiles_per_seq, 0))],
        out_specs=pl.BlockSpec((tm, d), lambda i, j: (i, j)),
        out_shape=jax.ShapeDtypeStruct((n, PROJ_BLOCKS * d), BF16),
        scratch_shapes=[pltpu.VMEM((tm, d), BF16)],
        compiler_params=_params("arbitrary", "arbitrary"),
        name="proj",
    )(x2, mod3, mod3, w_in3, b_in3, cos, sin)


def _ret_kernel(q_ref, k_ref, v_ref, g_ref, s0f_ref, s0b_ref, df_ref, db_ref, out_ref,
                sf_all, sb_all, p_scr, o_scr):
    c_ = RET_CHUNK
    n_chunks = q_ref.shape[1] // c_
    dn_t = (((0,), (0,)), ((), ()))
    lgf = _log_sigmoid(df_ref[0])
    lgb = _log_sigmoid(db_ref[0])
    ri = lax.broadcasted_iota(I32, (c_, c_), 0)
    ci = lax.broadcasted_iota(I32, (c_, c_), 1)
    diff = (ri - ci).astype(F32)
    mask = jnp.where(diff > 0, jnp.exp(diff * lgf), jnp.where(diff < 0, jnp.exp(-diff * lgb), 2.0))
    pos = lax.broadcasted_iota(I32, (c_, 1), 0).astype(F32)
    qdec_f = jnp.exp((pos + 1.0) * lgf)
    qdec_b = jnp.exp((c_ - pos) * lgb)
    kdec_f = jnp.exp((c_ - 1.0 - pos) * lgf)
    kdec_b = jnp.exp(pos * lgb)
    cdec_f = jnp.exp(c_ * lgf)
    cdec_b = jnp.exp(c_ * lgb)

    def chunk(c):
        return slice(c * c_, (c + 1) * c_)

    for c in range(n_chunks):
        k = k_ref[0, chunk(c), :].astype(F32)
        v = v_ref[0, chunk(c), :]
        sf_all[c] = lax.dot_general((k * kdec_f).astype(BF16), v, dn_t, preferred_element_type=F32)
        sb_all[c] = lax.dot_general((k * kdec_b).astype(BF16), v, dn_t, preferred_element_type=F32)

    sf = s0f_ref[0, 0]
    sb = s0b_ref[0, 0]
    for t in range(n_chunks):
        kv = sf_all[t]
        sf_all[t] = sf
        sf = sf * cdec_f + kv
        c = n_chunks - 1 - t
        kv = sb_all[c]
        sb_all[c] = sb
        sb = sb * cdec_b + kv

    for c in range(n_chunks):
        s = lax.dot_general(q_ref[0, chunk(c), :], k_ref[0, chunk(c), :], (((1,), (1,)), ((), ())),
                            preferred_element_type=F32)
        p_scr[c] = (s * mask).astype(BF16)

    for c in range(n_chunks):
        q = q_ref[0, chunk(c), :]
        o = jnp.dot(p_scr[c], v_ref[0, chunk(c), :], preferred_element_type=F32)
        o = o + qdec_f * jnp.dot(q, sf_all[c].astype(BF16), preferred_element_type=F32)
        o = o + qdec_b * jnp.dot(q, sb_all[c].astype(BF16), preferred_element_type=F32)
        o_scr[chunk(c), :] = o

    for c in range(n_chunks):
        o = o_scr[chunk(c), :]
        mu = jnp.mean(o, axis=-1, keepdims=True)
        d = o - mu
        var = jnp.mean(d * d, axis=-1, keepdims=True)
        on = d * lax.rsqrt(var + LN_EPS)
        out_ref[0, chunk(c), :] = (on * g_ref[0, chunk(c), :].astype(F32)).astype(BF16)


def _retention(proj3, s0f, s0b, dec_f, dec_b, d):
    b_, seq, _ = proj3.shape
    qb, kb = PB_Q * d // RET_QK_DIM, PB_K * d // RET_QK_DIM
    vb, gb = PB_V * d // RET_V_DIM, PB_G * d // RET_V_DIM
    n_chunks = seq // RET_CHUNK
    st_spec = pl.BlockSpec((1, 1, RET_QK_DIM, RET_V_DIM), lambda b, h: (b, h, 0, 0))
    st_all = pltpu.VMEM((n_chunks, RET_QK_DIM, RET_V_DIM), F32)
    return pl.pallas_call(
        _ret_kernel,
        grid=(b_, RET_HEADS),
        in_specs=[pl.BlockSpec((1, seq, RET_QK_DIM), lambda b, h: (b, 0, qb + h)),
                  pl.BlockSpec((1, seq, RET_QK_DIM), lambda b, h: (b, 0, kb + h)),
                  pl.BlockSpec((1, seq, RET_V_DIM), lambda b, h: (b, 0, vb + h)),
                  pl.BlockSpec((1, seq, RET_V_DIM), lambda b, h: (b, 0, gb + h)),
                  st_spec, st_spec,
                  pl.BlockSpec((1, 1, 1), lambda b, h: (h, 0, 0)),
                  pl.BlockSpec((1, 1, 1), lambda b, h: (h, 0, 0))],
        out_specs=pl.BlockSpec((1, seq, RET_V_DIM), lambda b, h: (b, 0, h)),
        out_shape=jax.ShapeDtypeStruct((b_, seq, RET_HEADS * RET_V_DIM), BF16),
        scratch_shapes=[st_all, st_all,
                        pltpu.VMEM((n_chunks, RET_CHUNK, RET_CHUNK), BF16),
                        pltpu.VMEM((seq, RET_V_DIM), F32)],
        compiler_params=_params("arbitrary", "arbitrary"),
        name="ret",
    )(proj3, proj3, proj3, proj3, s0f, s0b, dec_f, dec_b)


VEC_CONV_B, VEC_CLN_G, VEC_CLN_B, VEC_BCONV, VEC_BMIX, VEC_LN1_G, VEC_LN1_B = range(7)
CONV_ROWS = 64
YA_BLOCKS = 4
RT_E1, RT_E2, RT_RANK1, RT_RANK2 = 0, 1, 2, 3
RT_W1, RT_W2 = 0, 1


def _merge_kernel(ret_ref, z_ref, zp_ref, zn_ref, ga_ref, gb_ref, x_ref, gt_ref, shf_ref, scf_ref,
                  wret_ref, wconv_ref, wmix_ref, dw_ref, vec_ref, wrh_ref, wrl_ref, br_ref,
                  h1_ref, t_ref, ri_ref, rw_ref, cnt_ref, zext, zc_scr, cnt_scr, ya_scr, *, tiles_per_seq):
    tm, d = x_ref.shape
    step = pl.program_id(0)
    ti = step % tiles_per_seq
    vec = vec_ref[...]

    def row(r):
        return vec[r:r + 1, :]


    has_prev = (ti > 0).astype(F32)
    has_next = (ti < tiles_per_seq - 1).astype(F32)
    zext[0:CONV_HALO, :] = zp_ref[...].astype(F32) * has_prev
    zext[CONV_HALO:CONV_HALO + tm, :] = z_ref[...].astype(F32)
    zext[CONV_HALO + tm:, :] = zn_ref[...].astype(F32) * has_next
    base = CONV_HALO - CONV_WIDTH // 2
    win_rows = CONV_ROWS + 2 * CONV_HALO

    n_lt = d // LANES
    kc = ret_ref.shape[1] // n_lt

    def conv_piece(r0, lt):
        ls = slice(lt * LANES, (lt + 1) * LANES)
        win = zext[pl.ds(r0, win_rows), ls]
        acc = jnp.zeros((CONV_ROWS, LANES), F32) + vec_ref[VEC_CONV_B:VEC_CONV_B + 1, ls]
        for s in range(SUBLANES):
            taps = [w for w in range(CONV_WIDTH) if (base + w) % SUBLANES == s]
            if not taps:
                continue
            sh = win if s == 0 else pltpu.roll(win, win_rows - s, 0)
            for w in taps:
                a = (base + w) - s
                acc = acc + sh[a:a + CONV_ROWS, :] * dw_ref[w:w + 1, ls]
        return acc

    def conv_rows(r, carry):
        r0 = pl.multiple_of(r * (2 * CONV_ROWS), 2 * CONV_ROWS)
        ya = None
        accs = []
        for lt in range(n_lt):
            part = jnp.dot(ret_ref[:, lt * kc:(lt + 1) * kc], wret_ref[r, lt * kc:(lt + 1) * kc, :],
                           preferred_element_type=F32)
            ya = part if ya is None else ya + part
            accs.append((conv_piece(r0, lt), conv_piece(r0 + CONV_ROWS, lt)))
        ya_scr[r] = ya
        for lt in range(n_lt):
            ls = slice(lt * LANES, (lt + 1) * LANES)
            zc_scr[pl.ds(r0, CONV_ROWS), ls] = accs[lt][0]
            zc_scr[pl.ds(r0 + CONV_ROWS, CONV_ROWS), ls] = accs[lt][1]
        return carry

    lax.fori_loop(0, YA_BLOCKS, conv_rows, 0)
    y_a = jnp.concatenate([ya_scr[q] for q in range(YA_BLOCKS)], axis=1)
    zc = _ln_rows(zc_scr[...], row(VEC_CLN_G), row(VEC_CLN_B))
    zc = zc * _sigmoid(zc)
    y_b = jnp.dot(zc.astype(BF16), wconv_ref[...], preferred_element_type=F32) + row(VEC_BCONV)

    mixed = ga_ref[...].astype(F32) * y_a + gb_ref[...].astype(F32) * y_b
    mix = jnp.dot(mixed.astype(BF16), wmix_ref[...], preferred_element_type=F32) + row(VEC_BMIX)
    h1 = _ln_rows(ALPHA * x_ref[...] + gt_ref[0] * mix, row(VEC_LN1_G), row(VEC_LN1_B))
    h1_ref[...] = h1
    t = h1 * (1.0 + scf_ref[0]) + shf_ref[0]
    _store_token_tiles(t_ref, t)

    logits = _dot3(t, wrh_ref[...], wrl_ref[...]) + br_ref[...]

    lane = lax.broadcasted_iota(I32, logits.shape, 1).astype(F32)
    neg = -jnp.inf
    big = float(ROUTER_LANES)
    is_grp = lane < N_GROUPS
    gl = jnp.where(is_grp, logits, neg)
    gmax = jnp.max(gl, axis=-1, keepdims=True)
    gidx = jnp.min(jnp.where(gl == gmax, lane, big), axis=-1, keepdims=True)
    gsum = jnp.sum(jnp.where(is_grp, jnp.exp(gl - gmax), 0.0), axis=-1, keepdims=True)
    grp_w = 1.0 / gsum
    lo = N_GROUPS + gidx * EXPERTS_PER_GROUP
    el = jnp.where(lane >= lo, jnp.where(lane < lo + EXPERTS_PER_GROUP, logits, neg), neg)
    m1 = jnp.max(el, axis=-1, keepdims=True)
    i1 = jnp.min(jnp.where(el == m1, lane, big), axis=-1, keepdims=True)
    el2 = jnp.where(lane == i1, neg, el)
    m2 = jnp.max(el2, axis=-1, keepdims=True)
    i2 = jnp.min(jnp.where(el2 == m2, lane, big), axis=-1, keepdims=True)
    r = jnp.exp(m2 - m1)
    w1 = grp_w / (1.0 + r)
    w2 = grp_w * r / (1.0 + r)

    @pl.when(step == 0)
    def _():
        cnt_scr[...] = jnp.zeros_like(cnt_scr)

    oh1 = lane == i1
    oh2 = lane == i2
    oh = jnp.where(oh1, 1.0, jnp.where(oh2, 1.0, 0.0))
    tri = jnp.where(lax.broadcasted_iota(I32, (tm, tm), 0) > lax.broadcasted_iota(I32, (tm, tm), 1), 1.0, 0.0)
    before = jnp.dot(tri.astype(BF16), oh.astype(BF16), preferred_element_type=F32) + cnt_scr[0:1, :]
    rank1 = jnp.sum(jnp.where(oh1, before, 0.0), axis=-1, keepdims=True)
    rank2 = jnp.sum(jnp.where(oh2, before, 0.0), axis=-1, keepdims=True)
    cnt = cnt_scr[0:1, :] + jnp.sum(oh, axis=0, keepdims=True)
    cnt_scr[...] = jnp.broadcast_to(cnt, cnt_scr.shape)
    cnt_ref[...] = jnp.broadcast_to(cnt, cnt_ref.shape)

    e1 = i1 - N_GROUPS
    e2 = i2 - N_GROUPS
    ri = jnp.where(lane == RT_E1, e1, jnp.where(lane == RT_E2, e2,
                   jnp.where(lane == RT_RANK1, rank1, jnp.where(lane == RT_RANK2, rank2, 0.0))))
    ri_ref[...] = ri.astype(I32)
    rw_ref[...] = jnp.where(lane == RT_W1, w1, jnp.where(lane == RT_W2, w2, 0.0))


def _merge(ret2, proj2, x2, mod3, wret, wconv, wmix, dw, vec, wrh, wrl, br, seq, tm):
    n, d = x2.shape
    assert tm == YA_BLOCKS * 2 * CONV_ROWS
    tiles_per_seq = seq // tm
    hb = tm // CONV_HALO
    last_hb = n // CONV_HALO - 1
    const = lambda i: (0, 0)
    bat = lambda k: (lambda i: (i // tiles_per_seq, 0, k))
    return pl.pallas_call(
        functools.partial(_merge_kernel, tiles_per_seq=tiles_per_seq),
        grid=(n // tm,),
        in_specs=[pl.BlockSpec((tm, ret2.shape[1]), lambda i: (i, 0)),
                  pl.BlockSpec((tm, d), lambda i: (i, PB_Z)),
                  pl.BlockSpec((CONV_HALO, d), lambda i: (jnp.maximum(i * hb - 1, 0), PB_Z)),
                  pl.BlockSpec((CONV_HALO, d), lambda i: (jnp.minimum((i + 1) * hb, last_hb), PB_Z)),
                  pl.BlockSpec((tm, d), lambda i: (i, PB_GATE)),
                  pl.BlockSpec((tm, d), lambda i: (i, PB_GATE + 1)),
                  pl.BlockSpec((tm, d), lambda i: (i, 0)),
                  pl.BlockSpec((1, 1, d), bat(2)),
                  pl.BlockSpec((1, 1, d), bat(3)),
                  pl.BlockSpec((1, 1, d), bat(4)),
                  pl.BlockSpec(wret.shape, lambda i: (0, 0, 0)),
                  pl.BlockSpec(wconv.shape, const),
                  pl.BlockSpec(wmix.shape, const),
                  pl.BlockSpec(dw.shape, const),
                  pl.BlockSpec(vec.shape, const),
                  pl.BlockSpec(wrh.shape, const),
                  pl.BlockSpec(wrl.shape, const),
                  pl.BlockSpec(br.shape, const)],
        out_specs=[pl.BlockSpec((tm, d), lambda i: (i, 0)),
                   pl.BlockSpec((tm * SUBLANES, LANES), lambda i: (i, 0)),
                   pl.BlockSpec((tm, ROUTER_LANES), lambda i: (i, 0)),
                   pl.BlockSpec((tm, ROUTER_LANES), lambda i: (i, 0)),
                   pl.BlockSpec((SUBLANES, ROUTER_LANES), const)],
        out_shape=[jax.ShapeDtypeStruct((n, d), F32),
                   jax.ShapeDtypeStruct((n * SUBLANES, LANES), F32),
                   jax.ShapeDtypeStruct((n, ROUTER_LANES), I32),
                   jax.ShapeDtypeStruct((n, ROUTER_LANES), F32),
                   jax.ShapeDtypeStruct((SUBLANES, ROUTER_LANES), F32)],
        scratch_shapes=[pltpu.VMEM((tm + 2 * CONV_HALO, d), F32),
                        pltpu.VMEM((tm, d), F32),
                        pltpu.VMEM((SUBLANES, ROUTER_LANES), F32),
                        pltpu.VMEM((YA_BLOCKS, tm, d // YA_BLOCKS), F32)],
        compiler_params=_params("arbitrary"),
        name="merge",
    )(ret2, proj2, proj2, proj2, proj2, proj2, x2, mod3, mod3, mod3,
      wret, wconv, wmix, dw, vec, wrh, wrl, br)


DISPATCH_TILE = 512


def _token_tile(ref, tok):
    start = tok * SUBLANES
    return ref.at[pl.ds(start if isinstance(start, int) else pl.multiple_of(start, SUBLANES), SUBLANES)]


def _dispatch_kernel(pos_ref, t_ref, xs_hbm, sem):
    tm = t_ref.shape[0] // SUBLANES
    base = pl.program_id(0) * tm

    for r in range(tm):
        for k in range(EXPERT_TOP_K):
            dst = pos_ref[(base + r) * EXPERT_TOP_K + k]
            pltpu.make_async_copy(_token_tile(t_ref, r), _token_tile(xs_hbm, dst), sem).start()
    for k in range(EXPERT_TOP_K):
        pltpu.make_async_copy(t_ref, xs_hbm.at[pl.ds(0, tm * SUBLANES)], sem).wait()


def _dispatch(pos_flat, t_tiles):
    n = t_tiles.shape[0] // SUBLANES
    tm = min(DISPATCH_TILE, n)
    return pl.pallas_call(
        _dispatch_kernel,
        grid=(n // tm,),
        in_specs=[pl.BlockSpec(memory_space=pltpu.SMEM),
                  pl.BlockSpec((tm * SUBLANES, LANES), lambda i: (i, 0))],
        out_specs=pl.BlockSpec(memory_space=pl.ANY),
        out_shape=jax.ShapeDtypeStruct((n * EXPERT_TOP_K * SUBLANES, LANES), F32),
        scratch_shapes=[pltpu.SemaphoreType.DMA],
        compiler_params=_params("arbitrary"),
        name="dispatch",
    )(pos_flat, t_tiles)


EXPERT_TILE = 512


def _experts_kernel(ie_ref, it_ref, lo_ref, hi_ref, nv_ref, xs_ref, wg_ref, wu_ref, wd_ref, ys_ref,
                    wg_bf, wu_bf, wd_bf):
    k = pl.program_id(0)
    prev = jnp.maximum(k - 1, 0)
    new_expert = (k == 0) | (ie_ref[k] != ie_ref[prev])
    new_tile = (k == 0) | (it_ref[k] != it_ref[prev])

    @pl.when(new_expert)
    def _():
        wg_bf[...] = wg_ref[0, 0].astype(BF16)
        wu_bf[...] = wu_ref[0, 0].astype(BF16)
        wd_bf[...] = wd_ref[0, 0].astype(BF16)

    @pl.when(k < nv_ref[0])
    def _():
        x = _load_token_tiles(xs_ref, EXPERT_TILE).astype(BF16)
        g = jnp.dot(x, wg_bf[...], preferred_element_type=F32)
        u = jnp.dot(x, wu_bf[...], preferred_element_type=F32)
        hid = (g * _sigmoid(g) * u).astype(BF16)
        y = jnp.dot(hid, wd_bf[...], preferred_element_type=F32)
        rows = lax.broadcasted_iota(I32, (EXPERT_TILE, 1), 0)
        mine = (rows >= lo_ref[k]) & (rows < hi_ref[k])

        @pl.when(new_tile)
        def _():
            _store_token_tiles(ys_ref, jnp.where(mine, y, 0.0))

        @pl.when(jnp.logical_not(new_tile))
        def _():
            _store_token_tiles(ys_ref, jnp.where(mine, y, _load_token_tiles(ys_ref, EXPERT_TILE)))


def _experts(meta, xs, w_gate, w_up, w_down):
    p = xs.shape[0] // SUBLANES
    d, ff = w_gate.shape[-2:]
    assert d == SUBLANES * LANES
    n_items = meta[0].shape[0]
    grid_spec = pltpu.PrefetchScalarGridSpec(
        num_scalar_prefetch=5,
        grid=(n_items,),
        in_specs=[pl.BlockSpec((EXPERT_TILE * SUBLANES, LANES), lambda k, ie, it, lo, hi, nv: (it[k], 0)),
                  pl.BlockSpec((1, 1, d, ff), lambda k, ie, it, lo, hi, nv: (0, ie[k], 0, 0)),
                  pl.BlockSpec((1, 1, d, ff), lambda k, ie, it, lo, hi, nv: (0, ie[k], 0, 0)),
                  pl.BlockSpec((1, 1, ff, d), lambda k, ie, it, lo, hi, nv: (0, ie[k], 0, 0))],
        out_specs=pl.BlockSpec((EXPERT_TILE * SUBLANES, LANES), lambda k, ie, it, lo, hi, nv: (it[k], 0)),
        scratch_shapes=[pltpu.VMEM((d, ff), BF16), pltpu.VMEM((d, ff), BF16), pltpu.VMEM((ff, d), BF16)])
    return pl.pallas_call(
        _experts_kernel,
        grid_spec=grid_spec,
        out_shape=jax.ShapeDtypeStruct((p * SUBLANES, LANES), F32),
        compiler_params=_params("arbitrary"),
        name="experts",
    )(*meta, xs, w_gate, w_up, w_down)


def _expert_work_items(counts, p):
    n_tiles = p // EXPERT_TILE
    n_items = n_tiles + N_EXPERTS - 1
    offs = jnp.concatenate([jnp.zeros((1,), I32), jnp.cumsum(counts)])
    first = offs[:-1] // EXPERT_TILE
    last = (offs[1:] - 1) // EXPERT_TILE
    per = jnp.where(counts > 0, last - first + 1, 0)
    ends = jnp.cumsum(per)
    starts = ends - per
    n_valid = ends[-1]
    k = jnp.arange(n_items, dtype=I32)
    kk = jnp.minimum(k, n_valid - 1)
    ie = jnp.sum((ends[None, :] <= kk[:, None]).astype(I32), axis=1)
    it = first[ie] + (kk - starts[ie])
    lo = jnp.maximum(offs[ie], it * EXPERT_TILE) - it * EXPERT_TILE
    hi = jnp.minimum(offs[ie + 1], (it + 1) * EXPERT_TILE) - it * EXPERT_TILE
    valid = k < n_valid
    lo = jnp.where(valid, lo, 0)
    hi = jnp.where(valid, hi, 0)
    return offs, (ie.astype(I32), it.astype(I32), lo.astype(I32), hi.astype(I32), n_valid.reshape(1).astype(I32))


COMBINE_HALF = 256


def _combine_kernel(pos_ref, ys_hbm, rw_ref, h1_ref, gt_ref, ln_ref, out_ref, buf, sem):
    i = pl.program_id(0)
    n_steps = pl.num_programs(0)
    hm = COMBINE_HALF
    base = i * (2 * hm)

    def slot_copy(s, k):
        return pltpu.make_async_copy(ys_hbm.at[pl.ds(0, hm * SUBLANES)], buf.at[s, k], sem.at[s])

    def gather(first_tok, s):
        for r in range(hm):
            for k in range(EXPERT_TOP_K):
                src = pos_ref[(first_tok + r) * EXPERT_TOP_K + k]
                pltpu.make_async_copy(_token_tile(ys_hbm, src), _token_tile(buf.at[s, k], r), sem.at[s]).start()

    def finish(s):
        for k in range(EXPERT_TOP_K):
            slot_copy(s, k).wait()
        rows = slice(s * hm, (s + 1) * hm)
        rw = rw_ref[rows, :]
        y = (rw[:, RT_W1:RT_W1 + 1] * _load_token_tiles(buf.at[s, 0], hm)
             + rw[:, RT_W2:RT_W2 + 1] * _load_token_tiles(buf.at[s, 1], hm))
        out_ref[rows, :] = _ln_rows(ALPHA * h1_ref[rows, :] + gt_ref[0] * y, ln_ref[0:1, :], ln_ref[1:2, :])

    @pl.when(i == 0)
    def _():
        gather(0, 0)

    gather(base + hm, 1)
    finish(0)

    @pl.when(i + 1 < n_steps)
    def _():
        gather(base + 2 * hm, 0)

    finish(1)


def _combine(pos_flat, ys, rw, h1, mod3, ln2, seq):
    n, d = h1.shape
    tm = 2 * COMBINE_HALF
    tiles_per_seq = seq // tm
    return pl.pallas_call(
        _combine_kernel,
        grid=(n // tm,),
        in_specs=[pl.BlockSpec(memory_space=pltpu.SMEM),
                  pl.BlockSpec(memory_space=pl.ANY),
                  pl.BlockSpec((tm, ROUTER_LANES), lambda i: (i, 0)),
                  pl.BlockSpec((tm, d), lambda i: (i, 0)),
                  pl.BlockSpec((1, 1, d), lambda i: (i // tiles_per_seq, 0, 5)),
                  pl.BlockSpec(ln2.shape, lambda i: (0, 0))],
        out_specs=pl.BlockSpec((tm, d), lambda i: (i, 0)),
        out_shape=jax.ShapeDtypeStruct((n, d), F32),
        scratch_shapes=[pltpu.VMEM((2, EXPERT_TOP_K, COMBINE_HALF * SUBLANES, LANES), F32),
                        pltpu.SemaphoreType.DMA((2,))],
        compiler_params=_params("arbitrary"),
        name="combine",
    )(pos_flat, ys, rw, h1, mod3, ln2)


def _rope_tables(seq):
    pos = np.arange(seq)
    r = (pos // GRID_W).astype(np.float32)
    col = (pos % GRID_W).astype(np.float32)
    n_freq = RET_QK_DIM // 4
    inv = (np.float32(ROPE_BASE) ** (-np.arange(n_freq, dtype=np.float32) / np.float32(n_freq))).astype(np.float32)
    ang = np.concatenate([r[:, None] * inv, col[:, None] * inv], axis=-1)
    ang = np.concatenate([ang, ang], axis=-1).astype(np.float32)
    sign = np.concatenate([-np.ones((RET_QK_DIM // 2,), np.float32), np.ones((RET_QK_DIM // 2,), np.float32)])
    return jnp.asarray(np.cos(ang), F32), jnp.asarray(np.sin(ang) * sign, F32)


def kernel(x, c, ctx, c_ctx, w_ada, b_ada, w_in, b_in, ret_decay_fwd, ret_decay_bwd, w_ret_out, conv_dw, conv_dw_b, conv_ln_g, conv_ln_b, w_conv_out, b_conv_out, w_mix_out, b_mix_out, ln1_g, ln1_b, w_router_grp, b_router_grp, w_router_exp, b_router_exp, w_exp_gate, w_exp_up, w_exp_down, ln2_g, ln2_b):
    b_, seq, d = x.shape
    n = b_ * seq
    assert w_ada.shape[0] == DEPTH
    assert d == SUBLANES * LANES
    mod_rows = SUBLANES
    assert b_ + 1 <= mod_rows

    cs = jnp.concatenate([c, c_ctx[None, :], jnp.zeros((mod_rows - b_ - 1, d), F32)], axis=0)
    mod = _ada(cs, w_ada[0], b_ada[0][None, :])
    mod3 = mod.reshape(mod_rows, 1, 6 * d)

    w_in3 = w_in[0].astype(BF16)
    b_in3 = b_in[0][None, :]
    dec_f = ret_decay_fwd[0].reshape(RET_HEADS, 1, 1)
    dec_b = ret_decay_bwd[0].reshape(RET_HEADS, 1, 1)

    s0f, s0b = _ctx_states(ctx, mod3, w_in3, b_in3, dec_f, dec_b, b_)

    cos, sin = _rope_tables(seq)
    x2 = x.reshape(n, d)
    proj = _proj(x2, mod3, w_in3, b_in3, cos, sin, seq, tm=min(1024, seq))

    ret = _retention(proj.reshape(b_, seq, PROJ_BLOCKS * d), s0f, s0b, dec_f, dec_b, d)

    vec = jnp.concatenate([conv_dw_b, conv_ln_g, conv_ln_b, b_conv_out, b_mix_out, ln1_g, ln1_b,
                           jnp.zeros((1, d), F32)], axis=0)
    dw = jnp.concatenate([conv_dw[0], jnp.zeros((32 - CONV_WIDTH, d), F32)], axis=0)
    pad = ROUTER_LANES - N_GROUPS - N_EXPERTS
    wr = jnp.concatenate([w_router_grp[0], w_router_exp[0], jnp.zeros((d, pad), F32)], axis=1)
    wr_hi = wr.astype(BF16)
    wr_lo = (wr - wr_hi.astype(F32)).astype(BF16)
    br = jnp.concatenate([b_router_grp[0], b_router_exp[0], jnp.zeros((pad,), F32)])[None, :]
    h1, t, route_i, route_w, cnt = _merge(
        ret.reshape(n, RET_HEADS * RET_V_DIM), proj, x2, mod3,
        w_ret_out[0].astype(BF16).reshape(-1, YA_BLOCKS, d // YA_BLOCKS).transpose(1, 0, 2),
        w_conv_out[0].astype(BF16), w_mix_out[0].astype(BF16),
        dw, vec, wr_hi, wr_lo, br, seq, tm=min(512, seq))

    counts = cnt[0, N_GROUPS:N_GROUPS + N_EXPERTS].astype(I32)
    offs, meta = _expert_work_items(counts, n * EXPERT_TOP_K)
    eid = route_i[:, RT_E1:RT_E2 + 1]
    seg_start = jnp.sum(jnp.where(eid[:, :, None] == jnp.arange(N_EXPERTS, dtype=I32), offs[:N_EXPERTS], 0), axis=-1)
    pos = seg_start + route_i[:, RT_RANK1:RT_RANK2 + 1]
    pos_flat = pos.reshape(n * EXPERT_TOP_K)

    xs = _dispatch(pos_flat, t)
    ys = _experts(meta, xs, w_exp_gate, w_exp_up, w_exp_down)
    ln2 = jnp.concatenate([ln2_g, ln2_b, jnp.zeros((SUBLANES - 2, d), F32)], axis=0)
    out = _combine(pos_flat, ys, route_w, h1, mod3, ln2, seq)
    return out.reshape(b_, seq, d)
```

```python
import functools

import jax
import jax.numpy as jnp
import numpy as np
from jax import lax
from jax.experimental import pallas as pl
from jax.experimental.pallas import tpu as pltpu

GRID_W = 64
RET_HEADS = 8
RET_QK_DIM = 128
RET_V_DIM = 256
RET_CHUNK = 256
ROPE_BASE = 10000.0
CONV_WIDTH = 31
CONV_HALO = 16
SUBLANES = 8
N_GROUPS = 4
EXPERTS_PER_GROUP = 8
N_EXPERTS = N_GROUPS * EXPERTS_PER_GROUP
EXPERT_TOP_K = 2
LN_EPS = 1e-5
DEPTH = 1
ALPHA = (2.0 * DEPTH) ** 0.25
ROUTER_LANES = 128

V7X_VMEM_LIMIT = 56 * 1024 * 1024

F32 = jnp.float32
BF16 = jnp.bfloat16
I32 = jnp.int32


def _params(*sem):
    return pltpu.CompilerParams(dimension_semantics=sem, vmem_limit_bytes=V7X_VMEM_LIMIT)


def _sigmoid(v):
    return 0.5 * jnp.tanh(0.5 * v) + 0.5


def _split_bf16(v):
    hi = v.astype(BF16)
    return hi, (v - hi.astype(F32)).astype(BF16)


def _dot3(a, b_hi, b_lo):
    a_hi, a_lo = _split_bf16(a)
    return (jnp.dot(a_hi, b_hi, preferred_element_type=F32)
            + jnp.dot(a_lo, b_hi, preferred_element_type=F32)
            + jnp.dot(a_hi, b_lo, preferred_element_type=F32))


def _ln_rows(v, g, b):
    mu = jnp.mean(v, axis=-1, keepdims=True)
    d = v - mu
    var = jnp.mean(d * d, axis=-1, keepdims=True)
    return d * lax.rsqrt(var + LN_EPS) * g + b


ADA_K_TILE = 256


def _ada_kernel(cs_ref, w_ref, b_ref, out_ref):
    k = pl.program_id(0)
    s = cs_ref[...]
    s = s * _sigmoid(s)
    w_hi, w_lo = _split_bf16(w_ref[...])
    part = _dot3(s, w_hi, w_lo)

    @pl.when(k == 0)
    def _():
        out_ref[...] = part + b_ref[...]

    @pl.when(k > 0)
    def _():
        out_ref[...] += part


def _ada(cs, w_ada, b_ada):
    rows, d = cs.shape
    cols = w_ada.shape[1]
    tk = ADA_K_TILE
    return pl.pallas_call(
        _ada_kernel,
        grid=(d // tk,),
        in_specs=[pl.BlockSpec((rows, tk), lambda k: (0, k)),
                  pl.BlockSpec((tk, cols), lambda k: (k, 0)),
                  pl.BlockSpec((1, cols), lambda k: (0, 0))],
        out_specs=pl.BlockSpec((rows, cols), lambda k: (0, 0)),
        out_shape=jax.ShapeDtypeStruct((rows, cols), F32),
        compiler_params=_params("arbitrary"),
        name="ada",
    )(cs, w_ada, b_ada)


def _log_sigmoid(v):
    return jnp.minimum(v, 0.0) - jnp.log(1.0 + jnp.exp(-jnp.abs(v)))


def _ctx_kernel(ctx_ref, sh_ref, sc_ref, wk_ref, wv0_ref, wv1_ref, bk_ref, bv0_ref, bv1_ref, df_ref, db_ref,
                sf_ref, sb_ref):
    lc = ctx_ref.shape[1]
    u = (ctx_ref[0] * (1.0 + sc_ref[0]) + sh_ref[0]).astype(BF16)
    k = (jnp.dot(u, wk_ref[...], preferred_element_type=F32) + bk_ref[...]) * RET_QK_DIM ** -0.5
    v = jnp.concatenate(
        [(jnp.dot(u, w[...], preferred_element_type=F32) + b[...]).astype(BF16)
         for w, b in ((wv0_ref, bv0_ref), (wv1_ref, bv1_ref))], axis=1)
    pos = lax.broadcasted_iota(I32, (lc, 1), 0).astype(F32)
    dn = (((0,), (0,)), ((), ()))
    for h in range(RET_HEADS):
        lgf = _log_sigmoid(df_ref[h])
        lgb = _log_sigmoid(db_ref[h])
        kh = k[:, h * RET_QK_DIM:(h + 1) * RET_QK_DIM]
        vh = v[:, h * RET_V_DIM:(h + 1) * RET_V_DIM]
        kf = (kh * jnp.exp((lc - 1.0 - pos) * lgf)).astype(BF16)
        kb = (kh * jnp.exp(pos * lgb)).astype(BF16)
        sf_ref[0, h] = lax.dot_general(kf, vh, dn, preferred_element_type=F32)
        sb_ref[0, h] = lax.dot_general(kb, vh, dn, preferred_element_type=F32)


def _ctx_states(ctx, mod3, w_in3, b_in3, dec_f, dec_b, ctx_row):
    b_, lc, d = ctx.shape
    assert RET_HEADS * RET_QK_DIM == d and RET_HEADS * RET_V_DIM == 2 * d
    st = jax.ShapeDtypeStruct((b_, RET_HEADS, RET_QK_DIM, RET_V_DIM), F32)
    st_spec = pl.BlockSpec((1, RET_HEADS, RET_QK_DIM, RET_V_DIM), lambda b: (b, 0, 0, 0))
    wblk = lambda blk: pl.BlockSpec((d, d), lambda b: (0, blk))
    bblk = lambda blk: pl.BlockSpec((1, d), lambda b: (0, blk))
    dec_spec = pl.BlockSpec((RET_HEADS, 1, 1), lambda b: (0, 0, 0))
    return pl.pallas_call(
        _ctx_kernel,
        grid=(b_,),
        in_specs=[pl.BlockSpec((1, lc, d), lambda b: (b, 0, 0)),
                  pl.BlockSpec((1, 1, d), lambda b: (ctx_row, 0, 0)),
                  pl.BlockSpec((1, 1, d), lambda b: (ctx_row, 0, 1)),
                  wblk(PB_K), wblk(PB_V), wblk(PB_V + 1),
                  bblk(PB_K), bblk(PB_V), bblk(PB_V + 1),
                  dec_spec, dec_spec],
        out_specs=[st_spec, st_spec],
        out_shape=[st, st],
        compiler_params=_params("arbitrary"),
        name="ctx",
    )(ctx, mod3, mod3, w_in3, w_in3, w_in3, b_in3, b_in3, b_in3, dec_f, dec_b)


PROJ_BLOCKS = 9
PB_Q, PB_K, PB_V, PB_G, PB_Z, PB_GATE = 0, 1, 2, 4, 6, 7


def _proj_kernel(x_ref, sh_ref, sc_ref, w_ref, b_ref, cos_ref, sin_ref, out_ref, u_scr):
    j = pl.program_id(1)
    d = x_ref.shape[1]

    @pl.when(j == 0)
    def _():
        u_scr[...] = (x_ref[...] * (1.0 + sc_ref[0]) + sh_ref[0]).astype(BF16)

    def mm(blk):
        cols = slice(blk * d, (blk + 1) * d)
        return jnp.dot(u_scr[...], w_ref[:, cols], preferred_element_type=F32) + b_ref[:, cols]

    def rope(acc):
        cos = cos_ref[...]
        sin = sin_ref[...]
        for h in range(acc.shape[1] // RET_QK_DIM):
            sl = slice(h * RET_QK_DIM, (h + 1) * RET_QK_DIM)
            seg = acc[:, sl]
            out_ref[:, sl] = (seg * cos + pltpu.roll(seg, RET_QK_DIM // 2, 1) * sin).astype(BF16)

    def block(jj):
        if jj == PB_Q:
            rope(mm(PB_Q))
        elif jj == PB_K:
            rope(mm(PB_K) * RET_QK_DIM ** -0.5)
        elif jj < PB_G:
            out_ref[...] = mm(jj).astype(BF16)
        elif jj < PB_Z:
            acc = mm(jj)
            out_ref[...] = (acc * _sigmoid(acc)).astype(BF16)
        elif jj == PB_Z:
            out_ref[...] = (mm(PB_Z) * _sigmoid(mm(PB_Z + 1))).astype(BF16)
        else:
            out_ref[...] = _sigmoid(mm(jj + 1)).astype(BF16)

    for jj in range(PROJ_BLOCKS):
        pl.when(j == jj)(functools.partial(block, jj))


def _proj(x2, mod3, w_in3, b_in3, cos, sin, seq, tm):
    n, d = x2.shape
    tiles_per_seq = seq // tm
    return pl.pallas_call(
        _proj_kernel,
        grid=(n // tm, PROJ_BLOCKS),
        in_specs=[pl.BlockSpec((tm, d), lambda i, j: (i, 0)),
                  pl.BlockSpec((1, 1, d), lambda i, j: (i // tiles_per_seq, 0, 0)),
                  pl.BlockSpec((1, 1, d), lambda i, j: (i // tiles_per_seq, 0, 1)),
                  pl.BlockSpec(memory_space=pltpu.VMEM),
                  pl.BlockSpec(memory_space=pltpu.VMEM),
                  pl.BlockSpec((tm, RET_QK_DIM), lambda i, j: (i % tiles_per_seq, 0)),
                  pl.BlockSpec((tm, RET_QK_DIM), lambda i, j: (i % tiles_per_seq, 0))],
        out_specs=pl.BlockSpec((tm, d), lambda i, j: (i, j)),
        out_shape=jax.ShapeDtypeStruct((n, PROJ_BLOCKS * d), BF16),
        scratch_shapes=[pltpu.VMEM((tm, d), BF16)],
        compiler_params=_params("arbitrary", "arbitrary"),
        name="proj",
    )(x2, mod3, mod3, w_in3, b_in3, cos, sin)


def _ret_kernel(q_ref, k_ref, v_ref, g_ref, s0f_ref, s0b_ref, df_ref, db_ref, out_ref,
                sf_all, sb_all, p_scr, o_scr):
    c_ = RET_CHUNK
    n_chunks = q_ref.shape[1] // c_
    dn_t = (((0,), (0,)), ((), ()))
    lgf = _log_sigmoid(df_ref[0])
    lgb = _log_sigmoid(db_ref[0])
    ri = lax.broadcasted_iota(I32, (c_, c_), 0)
    ci = lax.broadcasted_iota(I32, (c_, c_), 1)
    diff = (ri - ci).astype(F32)
    mask = jnp.where(diff > 0, jnp.exp(diff * lgf), jnp.where(diff < 0, jnp.exp(-diff * lgb), 2.0))
    pos = lax.broadcasted_iota(I32, (c_, 1), 0).astype(F32)
    qdec_f = jnp.exp((pos + 1.0) * lgf)
    qdec_b = jnp.exp((c_ - pos) * lgb)
    kdec_f = jnp.exp((c_ - 1.0 - pos) * lgf)
    kdec_b = jnp.exp(pos * lgb)
    cdec_f = jnp.exp(c_ * lgf)
    cdec_b = jnp.exp(c_ * lgb)

    def chunk(c):
        return slice(c * c_, (c + 1) * c_)

    for c in range(n_chunks):
        k = k_ref[0, chunk(c), :].astype(F32)
        v = v_ref[0, chunk(c), :]
        sf_all[c] = lax.dot_general((k * kdec_f).astype(BF16), v, dn_t, preferred_element_type=F32)
        sb_all[c] = lax.dot_general((k * kdec_b).astype(BF16), v, dn_t, preferred_element_type=F32)

    sf = s0f_ref[0, 0]
    sb = s0b_ref[0, 0]
    for t in range(n_chunks):
        kv = sf_all[t]
        sf_all[t] = sf
        sf = sf * cdec_f + kv
        c = n_chunks - 1 - t
        kv = sb_all[c]
        sb_all[c] = sb
        sb = sb * cdec_b + kv

    for c in range(n_chunks):
        s = lax.dot_general(q_ref[0, chunk(c), :], k_ref[0, chunk(c), :], (((1,), (1,)), ((), ())),
                            preferred_element_type=F32)
        p_scr[c] = (s * mask).astype(BF16)

    for c in range(n_chunks):
        q = q_ref[0, chunk(c), :]
        o = jnp.dot(p_scr[c], v_ref[0, chunk(c), :], preferred_element_type=F32)
        o = o + qdec_f * jnp.dot(q, sf_all[c].astype(BF16), preferred_element_type=F32)
        o = o + qdec_b * jnp.dot(q, sb_all[c].astype(BF16), preferred_element_type=F32)
        o_scr[chunk(c), :] = o

    for c in range(n_chunks):
        o = o_scr[chunk(c), :]
        mu = jnp.mean(o, axis=-1, keepdims=True)
        d = o - mu
        var = jnp.mean(d * d, axis=-1, keepdims=True)
        on = d * lax.rsqrt(var + LN_EPS)
        out_ref[0, chunk(c), :] = (on * g_ref[0, chunk(c), :].astype(F32)).astype(BF16)


def _retention(proj3, s0f, s0b, dec_f, dec_b, d):
    b_, seq, _ = proj3.shape
    qb, kb = PB_Q * d // RET_QK_DIM, PB_K * d // RET_QK_DIM
    vb, gb = PB_V * d // RET_V_DIM, PB_G * d // RET_V_DIM
    n_chunks = seq // RET_CHUNK
    st_spec = pl.BlockSpec((1, 1, RET_QK_DIM, RET_V_DIM), lambda b, h: (b, h, 0, 0))
    st_all = pltpu.VMEM((n_chunks, RET_QK_DIM, RET_V_DIM), F32)
    return pl.pallas_call(
        _ret_kernel,
        grid=(b_, RET_HEADS),
        in_specs=[pl.BlockSpec((1, seq, RET_QK_DIM), lambda b, h: (b, 0, qb + h)),
                  pl.BlockSpec((1, seq, RET_QK_DIM), lambda b, h: (b, 0, kb + h)),
                  pl.BlockSpec((1, seq, RET_V_DIM), lambda b, h: (b, 0, vb + h)),
                  pl.BlockSpec((1, seq, RET_V_DIM), lambda b, h: (b, 0, gb + h)),
                  st_spec, st_spec,
                  pl.BlockSpec((1, 1, 1), lambda b, h: (h, 0, 0)),
                  pl.BlockSpec((1, 1, 1), lambda b, h: (h, 0, 0))],
        out_specs=pl.BlockSpec((1, seq, RET_V_DIM), lambda b, h: (b, 0, h)),
        out_shape=jax.ShapeDtypeStruct((b_, seq, RET_HEADS * RET_V_DIM), BF16),
        scratch_shapes=[st_all, st_all,
                        pltpu.VMEM((n_chunks, RET_CHUNK, RET_CHUNK), BF16),
                        pltpu.VMEM((seq, RET_V_DIM), F32)],
        compiler_params=_params("arbitrary", "arbitrary"),
        name="ret",
    )(proj3, proj3, proj3, proj3, s0f, s0b, dec_f, dec_b)


VEC_CONV_B, VEC_CLN_G, VEC_CLN_B, VEC_BCONV, VEC_BMIX, VEC_LN1_G, VEC_LN1_B = range(7)
CONV_ROWS = 64
YA_BLOCKS = 4
LANES = 128
RT_E1, RT_E2, RT_RANK1, RT_RANK2 = 0, 1, 2, 3
RT_W1, RT_W2 = 0, 1


def _merge_kernel(ret_ref, z_ref, zp_ref, zn_ref, ga_ref, gb_ref, x_ref, gt_ref, shf_ref, scf_ref,
                  wret_ref, wconv_ref, wmix_ref, dw_ref, vec_ref, wrh_ref, wrl_ref, br_ref,
                  h1_ref, t_ref, ri_ref, rw_ref, cnt_ref, zext, zc_scr, cnt_scr, ya_scr, *, tiles_per_seq):
    tm, d = x_ref.shape
    step = pl.program_id(0)
    ti = step % tiles_per_seq
    vec = vec_ref[...]

    def row(r):
        return vec[r:r + 1, :]


    has_prev = (ti > 0).astype(F32)
    has_next = (ti < tiles_per_seq - 1).astype(F32)
    zext[0:CONV_HALO, :] = zp_ref[...].astype(F32) * has_prev
    zext[CONV_HALO:CONV_HALO + tm, :] = z_ref[...].astype(F32)
    zext[CONV_HALO + tm:, :] = zn_ref[...].astype(F32) * has_next
    base = CONV_HALO - CONV_WIDTH // 2
    win_rows = CONV_ROWS + 2 * CONV_HALO

    n_lt = d // LANES
    kc = ret_ref.shape[1] // n_lt

    def conv_piece(r0, lt):
        ls = slice(lt * LANES, (lt + 1) * LANES)
        win = zext[pl.ds(r0, win_rows), ls]
        acc = jnp.zeros((CONV_ROWS, LANES), F32) + vec_ref[VEC_CONV_B:VEC_CONV_B + 1, ls]
        for s in range(SUBLANES):
            taps = [w for w in range(CONV_WIDTH) if (base + w) % SUBLANES == s]
            if not taps:
                continue
            sh = win if s == 0 else pltpu.roll(win, win_rows - s, 0)
            for w in taps:
                a = (base + w) - s
                acc = acc + sh[a:a + CONV_ROWS, :] * dw_ref[w:w + 1, ls]
        return acc

    def conv_rows(r, carry):
        r0 = pl.multiple_of(r * (2 * CONV_ROWS), 2 * CONV_ROWS)
        ya = None
        accs = []
        for lt in range(n_lt):
            part = jnp.dot(ret_ref[:, lt * kc:(lt + 1) * kc], wret_ref[r, lt * kc:(lt + 1) * kc, :],
                           preferred_element_type=F32)
            ya = part if ya is None else ya + part
            accs.append((conv_piece(r0, lt), conv_piece(r0 + CONV_ROWS, lt)))
        ya_scr[r] = ya
        for lt in range(n_lt):
            ls = slice(lt * LANES, (lt + 1) * LANES)
            zc_scr[pl.ds(r0, CONV_ROWS), ls] = accs[lt][0]
            zc_scr[pl.ds(r0 + CONV_ROWS, CONV_ROWS), ls] = accs[lt][1]
        return carry

    lax.fori_loop(0, YA_BLOCKS, conv_rows, 0)
    y_a = jnp.concatenate([ya_scr[q] for q in range(YA_BLOCKS)], axis=1)
    zc = _ln_rows(zc_scr[...], row(VEC_CLN_G), row(VEC_CLN_B))
    zc = zc * _sigmoid(zc)
    y_b = jnp.dot(zc.astype(BF16), wconv_ref[...], preferred_element_type=F32) + row(VEC_BCONV)

    mixed = ga_ref[...].astype(F32) * y_a + gb_ref[...].astype(F32) * y_b
    mix = jnp.dot(mixed.astype(BF16), wmix_ref[...], preferred_element_type=F32) + row(VEC_BMIX)
    h1 = _ln_rows(ALPHA * x_ref[...] + gt_ref[0] * mix, row(VEC_LN1_G), row(VEC_LN1_B))
    h1_ref[...] = h1
    t = h1 * (1.0 + scf_ref[0]) + shf_ref[0]
    t_ref[...] = t

    logits = _dot3(t, wrh_ref[...], wrl_ref[...]) + br_ref[...]

    lane = lax.broadcasted_iota(I32, logits.shape, 1).astype(F32)
    neg = -jnp.inf
    big = float(ROUTER_LANES)
    is_grp = lane < N_GROUPS
    gl = jnp.where(is_grp, logits, neg)
    gmax = jnp.max(gl, axis=-1, keepdims=True)
    gidx = jnp.min(jnp.where(gl == gmax, lane, big), axis=-1, keepdims=True)
    gsum = jnp.sum(jnp.where(is_grp, jnp.exp(gl - gmax), 0.0), axis=-1, keepdims=True)
    grp_w = 1.0 / gsum
    lo = N_GROUPS + gidx * EXPERTS_PER_GROUP
    el = jnp.where(lane >= lo, jnp.where(lane < lo + EXPERTS_PER_GROUP, logits, neg), neg)
    m1 = jnp.max(el, axis=-1, keepdims=True)
    i1 = jnp.min(jnp.where(el == m1, lane, big), axis=-1, keepdims=True)
    el2 = jnp.where(lane == i1, neg, el)
    m2 = jnp.max(el2, axis=-1, keepdims=True)
    i2 = jnp.min(jnp.where(el2 == m2, lane, big), axis=-1, keepdims=True)
    r = jnp.exp(m2 - m1)
    w1 = grp_w / (1.0 + r)
    w2 = grp_w * r / (1.0 + r)

    @pl.when(step == 0)
    def _():
        cnt_scr[...] = jnp.zeros_like(cnt_scr)

    oh1 = lane == i1
    oh2 = lane == i2
    oh = jnp.where(oh1, 1.0, jnp.where(oh2, 1.0, 0.0))
    tri = jnp.where(lax.broadcasted_iota(I32, (tm, tm), 0) > lax.broadcasted_iota(I32, (tm, tm), 1), 1.0, 0.0)
    before = jnp.dot(tri.astype(BF16), oh.astype(BF16), preferred_element_type=F32) + cnt_scr[0:1, :]
    rank1 = jnp.sum(jnp.where(oh1, before, 0.0), axis=-1, keepdims=True)
    rank2 = jnp.sum(jnp.where(oh2, before, 0.0), axis=-1, keepdims=True)
    cnt = cnt_scr[0:1, :] + jnp.sum(oh, axis=0, keepdims=True)
    cnt_scr[...] = jnp.broadcast_to(cnt, cnt_scr.shape)
    cnt_ref[...] = jnp.broadcast_to(cnt, cnt_ref.shape)

    e1 = i1 - N_GROUPS
    e2 = i2 - N_GROUPS
    ri = jnp.where(lane == RT_E1, e1, jnp.where(lane == RT_E2, e2,
                   jnp.where(lane == RT_RANK1, rank1, jnp.where(lane == RT_RANK2, rank2, 0.0))))
    ri_ref[...] = ri.astype(I32)
    rw_ref[...] = jnp.where(lane == RT_W1, w1, jnp.where(lane == RT_W2, w2, 0.0))


def _merge(ret2, proj2, x2, mod3, wret, wconv, wmix, dw, vec, wrh, wrl, br, seq, tm):
    n, d = x2.shape
    assert tm == YA_BLOCKS * 2 * CONV_ROWS
    tiles_per_seq = seq // tm
    hb = tm // CONV_HALO
    last_hb = n // CONV_HALO - 1
    const = lambda i: (0, 0)
    bat = lambda k: (lambda i: (i // tiles_per_seq, 0, k))
    return pl.pallas_call(
        functools.partial(_merge_kernel, tiles_per_seq=tiles_per_seq),
        grid=(n // tm,),
        in_specs=[pl.BlockSpec((tm, ret2.shape[1]), lambda i: (i, 0)),
                  pl.BlockSpec((tm, d), lambda i: (i, PB_Z)),
                  pl.BlockSpec((CONV_HALO, d), lambda i: (jnp.maximum(i * hb - 1, 0), PB_Z)),
                  pl.BlockSpec((CONV_HALO, d), lambda i: (jnp.minimum((i + 1) * hb, last_hb), PB_Z)),
                  pl.BlockSpec((tm, d), lambda i: (i, PB_GATE)),
                  pl.BlockSpec((tm, d), lambda i: (i, PB_GATE + 1)),
                  pl.BlockSpec((tm, d), lambda i: (i, 0)),
                  pl.BlockSpec((1, 1, d), bat(2)),
                  pl.BlockSpec((1, 1, d), bat(3)),
                  pl.BlockSpec((1, 1, d), bat(4)),
                  pl.BlockSpec(wret.shape, lambda i: (0, 0, 0)),
                  pl.BlockSpec(wconv.shape, const),
                  pl.BlockSpec(wmix.shape, const),
                  pl.BlockSpec(dw.shape, const),
                  pl.BlockSpec(vec.shape, const),
                  pl.BlockSpec(wrh.shape, const),
                  pl.BlockSpec(wrl.shape, const),
                  pl.BlockSpec(br.shape, const)],
        out_specs=[pl.BlockSpec((tm, d), lambda i: (i, 0)),
                   pl.BlockSpec((tm, d), lambda i: (i, 0)),
                   pl.BlockSpec((tm, ROUTER_LANES), lambda i: (i, 0)),
                   pl.BlockSpec((tm, ROUTER_LANES), lambda i: (i, 0)),
                   pl.BlockSpec((SUBLANES, ROUTER_LANES), const)],
        out_shape=[jax.ShapeDtypeStruct((n, d), F32),
                   jax.ShapeDtypeStruct((n, d), F32),
                   jax.ShapeDtypeStruct((n, ROUTER_LANES), I32),
                   jax.ShapeDtypeStruct((n, ROUTER_LANES), F32),
                   jax.ShapeDtypeStruct((SUBLANES, ROUTER_LANES), F32)],
        scratch_shapes=[pltpu.VMEM((tm + 2 * CONV_HALO, d), F32),
                        pltpu.VMEM((tm, d), F32),
                        pltpu.VMEM((SUBLANES, ROUTER_LANES), F32),
                        pltpu.VMEM((YA_BLOCKS, tm, d // YA_BLOCKS), F32)],
        compiler_params=_params("arbitrary"),
        name="merge",
    )(ret2, proj2, proj2, proj2, proj2, proj2, x2, mod3, mod3, mod3,
      wret, wconv, wmix, dw, vec, wrh, wrl, br)


DISPATCH_TILE = 512


def _dispatch_kernel(pos_ref, t_ref, xs_hbm, sem):
    tm = t_ref.shape[0]
    base = pl.program_id(0) * tm

    for r in range(tm):
        for k in range(EXPERT_TOP_K):
            dst = pos_ref[(base + r) * EXPERT_TOP_K + k]
            pltpu.make_async_copy(t_ref.at[pl.ds(r, 1)], xs_hbm.at[pl.ds(dst, 1)], sem).start(priority=k)
    for k in range(EXPERT_TOP_K):
        pltpu.make_async_copy(t_ref, xs_hbm.at[pl.ds(0, tm)], sem).wait()


def _dispatch(pos_flat, t2):
    n, d = t2.shape
    tm = min(DISPATCH_TILE, n)
    return pl.pallas_call(
        _dispatch_kernel,
        grid=(n // tm,),
        in_specs=[pl.BlockSpec(memory_space=pltpu.SMEM),
                  pl.BlockSpec((tm, d), lambda i: (i, 0))],
        out_specs=pl.BlockSpec(memory_space=pl.ANY),
        out_shape=jax.ShapeDtypeStruct((n * EXPERT_TOP_K, d), F32),
        scratch_shapes=[pltpu.SemaphoreType.DMA],
        compiler_params=_params("arbitrary"),
        name="dispatch",
    )(pos_flat, t2)


EXPERT_TILE = 512


def _experts_kernel(ie_ref, it_ref, lo_ref, hi_ref, nv_ref, xs_ref, wg_ref, wu_ref, wd_ref, ys_ref,
                    wg_bf, wu_bf, wd_bf):
    k = pl.program_id(0)
    prev = jnp.maximum(k - 1, 0)
    new_expert = (k == 0) | (ie_ref[k] != ie_ref[prev])
    new_tile = (k == 0) | (it_ref[k] != it_ref[prev])

    @pl.when(new_expert)
    def _():
        wg_bf[...] = wg_ref[0, 0].astype(BF16)
        wu_bf[...] = wu_ref[0, 0].astype(BF16)
        wd_bf[...] = wd_ref[0, 0].astype(BF16)

    @pl.when(k < nv_ref[0])
    def _():
        x = xs_ref[...].astype(BF16)
        g = jnp.dot(x, wg_bf[...], preferred_element_type=F32)
        u = jnp.dot(x, wu_bf[...], preferred_element_type=F32)
        hid = (g * _sigmoid(g) * u).astype(BF16)
        y = jnp.dot(hid, wd_bf[...], preferred_element_type=F32)
        rows = lax.broadcasted_iota(I32, (xs_ref.shape[0], 1), 0)
        mine = (rows >= lo_ref[k]) & (rows < hi_ref[k])

        @pl.when(new_tile)
        def _():
            ys_ref[...] = jnp.where(mine, y, 0.0)

        @pl.when(jnp.logical_not(new_tile))
        def _():
            ys_ref[...] = jnp.where(mine, y, ys_ref[...])


def _experts(meta, xs, w_gate, w_up, w_down):
    p, d = xs.shape
    ff = w_gate.shape[-1]
    n_items = meta[0].shape[0]
    grid_spec = pltpu.PrefetchScalarGridSpec(
        num_scalar_prefetch=5,
        grid=(n_items,),
        in_specs=[pl.BlockSpec((EXPERT_TILE, d), lambda k, ie, it, lo, hi, nv: (it[k], 0)),
                  pl.BlockSpec((1, 1, d, ff), lambda k, ie, it, lo, hi, nv: (0, ie[k], 0, 0)),
                  pl.BlockSpec((1, 1, d, ff), lambda k, ie, it, lo, hi, nv: (0, ie[k], 0, 0)),
                  pl.BlockSpec((1, 1, ff, d), lambda k, ie, it, lo, hi, nv: (0, ie[k], 0, 0))],
        out_specs=pl.BlockSpec((EXPERT_TILE, d), lambda k, ie, it, lo, hi, nv: (it[k], 0)),
        scratch_shapes=[pltpu.VMEM((d, ff), BF16), pltpu.VMEM((d, ff), BF16), pltpu.VMEM((ff, d), BF16)])
    return pl.pallas_call(
        _experts_kernel,
        grid_spec=grid_spec,
        out_shape=jax.ShapeDtypeStruct((p, d), F32),
        compiler_params=_params("arbitrary"),
        name="experts",
    )(*meta, xs, w_gate, w_up, w_down)


def _expert_work_items(counts, p):
    n_tiles = p // EXPERT_TILE
    n_items = n_tiles + N_EXPERTS - 1
    offs = jnp.concatenate([jnp.zeros((1,), I32), jnp.cumsum(counts)])
    first = offs[:-1] // EXPERT_TILE
    last = (offs[1:] - 1) // EXPERT_TILE
    per = jnp.where(counts > 0, last - first + 1, 0)
    ends = jnp.cumsum(per)
    starts = ends - per
    n_valid = ends[-1]
    k = jnp.arange(n_items, dtype=I32)
    kk = jnp.minimum(k, n_valid - 1)
    ie = jnp.sum((ends[None, :] <= kk[:, None]).astype(I32), axis=1)
    it = first[ie] + (kk - starts[ie])
    lo = jnp.maximum(offs[ie], it * EXPERT_TILE) - it * EXPERT_TILE
    hi = jnp.minimum(offs[ie + 1], (it + 1) * EXPERT_TILE) - it * EXPERT_TILE
    valid = k < n_valid
    lo = jnp.where(valid, lo, 0)
    hi = jnp.where(valid, hi, 0)
    return offs, (ie.astype(I32), it.astype(I32), lo.astype(I32), hi.astype(I32), n_valid.reshape(1).astype(I32))


COMBINE_HALF = 256


def _combine_kernel(pos_ref, ys_hbm, rw_ref, h1_ref, gt_ref, ln_ref, out_ref, buf, sem):
    i = pl.program_id(0)
    n_steps = pl.num_programs(0)
    hm = COMBINE_HALF
    base = i * (2 * hm)

    def slot_copy(s, k):
        return pltpu.make_async_copy(ys_hbm.at[pl.ds(0, hm)], buf.at[s, k], sem.at[s])

    def gather(first_tok, s):
        for r in range(hm):
            for k in range(EXPERT_TOP_K):
                src = pos_ref[(first_tok + r) * EXPERT_TOP_K + k]
                pltpu.make_async_copy(ys_hbm.at[pl.ds(src, 1)], buf.at[s, k, pl.ds(r, 1)],
                                      sem.at[s]).start(priority=k)

    def finish(s):
        for k in range(EXPERT_TOP_K):
            slot_copy(s, k).wait()
        rows = slice(s * hm, (s + 1) * hm)
        rw = rw_ref[rows, :]
        y = rw[:, RT_W1:RT_W1 + 1] * buf[s, 0] + rw[:, RT_W2:RT_W2 + 1] * buf[s, 1]
        out_ref[rows, :] = _ln_rows(ALPHA * h1_ref[rows, :] + gt_ref[0] * y, ln_ref[0:1, :], ln_ref[1:2, :])

    @pl.when(i == 0)
    def _():
        gather(0, 0)

    gather(base + hm, 1)
    finish(0)

    @pl.when(i + 1 < n_steps)
    def _():
        gather(base + 2 * hm, 0)

    finish(1)


def _combine(pos_flat, ys, rw, h1, mod3, ln2, seq):
    n, d = h1.shape
    tm = 2 * COMBINE_HALF
    tiles_per_seq = seq // tm
    return pl.pallas_call(
        _combine_kernel,
        grid=(n // tm,),
        in_specs=[pl.BlockSpec(memory_space=pltpu.SMEM),
                  pl.BlockSpec(memory_space=pl.ANY),
                  pl.BlockSpec((tm, ROUTER_LANES), lambda i: (i, 0)),
                  pl.BlockSpec((tm, d), lambda i: (i, 0)),
                  pl.BlockSpec((1, 1, d), lambda i: (i // tiles_per_seq, 0, 5)),
                  pl.BlockSpec(ln2.shape, lambda i: (0, 0))],
        out_specs=pl.BlockSpec((tm, d), lambda i: (i, 0)),
        out_shape=jax.ShapeDtypeStruct((n, d), F32),
        scratch_shapes=[pltpu.VMEM((2, EXPERT_TOP_K, COMBINE_HALF, d), F32),
                        pltpu.SemaphoreType.DMA((2,))],
        compiler_params=_params("arbitrary"),
        name="combine",
    )(pos_flat, ys, rw, h1, mod3, ln2)


def _rope_tables(seq):
    pos = np.arange(seq)
    r = (pos // GRID_W).astype(np.float32)
    col = (pos % GRID_W).astype(np.float32)
    n_freq = RET_QK_DIM // 4
    inv = (np.float32(ROPE_BASE) ** (-np.arange(n_freq, dtype=np.float32) / np.float32(n_freq))).astype(np.float32)
    ang = np.concatenate([r[:, None] * inv, col[:, None] * inv], axis=-1)
    ang = np.concatenate([ang, ang], axis=-1).astype(np.float32)
    sign = np.concatenate([-np.ones((RET_QK_DIM // 2,), np.float32), np.ones((RET_QK_DIM // 2,), np.float32)])
    return jnp.asarray(np.cos(ang), F32), jnp.asarray(np.sin(ang) * sign, F32)


def kernel(x, c, ctx, c_ctx, w_ada, b_ada, w_in, b_in, ret_decay_fwd, ret_decay_bwd, w_ret_out, conv_dw, conv_dw_b, conv_ln_g, conv_ln_b, w_conv_out, b_conv_out, w_mix_out, b_mix_out, ln1_g, ln1_b, w_router_grp, b_router_grp, w_router_exp, b_router_exp, w_exp_gate, w_exp_up, w_exp_down, ln2_g, ln2_b):
    b_, seq, d = x.shape
    n = b_ * seq
    assert w_ada.shape[0] == DEPTH
    mod_rows = SUBLANES
    assert b_ + 1 <= mod_rows

    cs = jnp.concatenate([c, c_ctx[None, :], jnp.zeros((mod_rows - b_ - 1, d), F32)], axis=0)
    mod = _ada(cs, w_ada[0], b_ada[0][None, :])
    mod3 = mod.reshape(mod_rows, 1, 6 * d)

    w_in3 = w_in[0].astype(BF16)
    b_in3 = b_in[0][None, :]
    dec_f = ret_decay_fwd[0].reshape(RET_HEADS, 1, 1)
    dec_b = ret_decay_bwd[0].reshape(RET_HEADS, 1, 1)

    s0f, s0b = _ctx_states(ctx, mod3, w_in3, b_in3, dec_f, dec_b, b_)

    cos, sin = _rope_tables(seq)
    x2 = x.reshape(n, d)
    proj = _proj(x2, mod3, w_in3, b_in3, cos, sin, seq, tm=min(1024, seq))

    ret = _retention(proj.reshape(b_, seq, PROJ_BLOCKS * d), s0f, s0b, dec_f, dec_b, d)

    vec = jnp.concatenate([conv_dw_b, conv_ln_g, conv_ln_b, b_conv_out, b_mix_out, ln1_g, ln1_b,
                           jnp.zeros((1, d), F32)], axis=0)
    dw = jnp.concatenate([conv_dw[0], jnp.zeros((32 - CONV_WIDTH, d), F32)], axis=0)
    pad = ROUTER_LANES - N_GROUPS - N_EXPERTS
    wr = jnp.concatenate([w_router_grp[0], w_router_exp[0], jnp.zeros((d, pad), F32)], axis=1)
    wr_hi = wr.astype(BF16)
    wr_lo = (wr - wr_hi.astype(F32)).astype(BF16)
    br = jnp.concatenate([b_router_grp[0], b_router_exp[0], jnp.zeros((pad,), F32)])[None, :]
    h1, t, route_i, route_w, cnt = _merge(
        ret.reshape(n, RET_HEADS * RET_V_DIM), proj, x2, mod3,
        w_ret_out[0].astype(BF16).reshape(-1, YA_BLOCKS, d // YA_BLOCKS).transpose(1, 0, 2),
        w_conv_out[0].astype(BF16), w_mix_out[0].astype(BF16),
        dw, vec, wr_hi, wr_lo, br, seq, tm=min(512, seq))

    counts = cnt[0, N_GROUPS:N_GROUPS + N_EXPERTS].astype(I32)
    offs, meta = _expert_work_items(counts, n * EXPERT_TOP_K)
    eid = route_i[:, RT_E1:RT_E2 + 1]
    seg_start = jnp.sum(jnp.where(eid[:, :, None] == jnp.arange(N_EXPERTS, dtype=I32), offs[:N_EXPERTS], 0), axis=-1)
    pos = seg_start + route_i[:, RT_RANK1:RT_RANK2 + 1]
    pos_flat = pos.reshape(n * EXPERT_TOP_K)

    xs = _dispatch(pos_flat, t)
    ys = _experts(meta, xs, w_exp_gate, w_exp_up, w_exp_down)
    ln2 = jnp.concatenate([ln2_g, ln2_b, jnp.zeros((SUBLANES - 2, d), F32)], axis=0)
    out = _combine(pos_flat, ys, route_w, h1, mod3, ln2, seq)
    return out.reshape(b_, seq, d)
```

```python
import functools

import jax
import jax.numpy as jnp
import numpy as np
from jax import lax
from jax.experimental import pallas as pl
from jax.experimental.pallas import tpu as pltpu

GRID_W = 64
RET_HEADS = 8
RET_QK_DIM = 128
RET_V_DIM = 256
RET_CHUNK = 256
ROPE_BASE = 10000.0
CONV_WIDTH = 31
CONV_HALO = 16
SUBLANES = 8
N_GROUPS = 4
EXPERTS_PER_GROUP = 8
N_EXPERTS = N_GROUPS * EXPERTS_PER_GROUP
EXPERT_TOP_K = 2
LN_EPS = 1e-5
DEPTH = 1
ALPHA = (2.0 * DEPTH) ** 0.25
ROUTER_LANES = 128

V7X_VMEM_LIMIT = 56 * 1024 * 1024

F32 = jnp.float32
BF16 = jnp.bfloat16
I32 = jnp.int32


def _params(*sem):
    return pltpu.CompilerParams(dimension_semantics=sem, vmem_limit_bytes=V7X_VMEM_LIMIT)


def _sigmoid(v):
    return 0.5 * jnp.tanh(0.5 * v) + 0.5


def _split_bf16(v):
    hi = v.astype(BF16)
    return hi, (v - hi.astype(F32)).astype(BF16)


def _dot3(a, b_hi, b_lo):
    a_hi, a_lo = _split_bf16(a)
    return (jnp.dot(a_hi, b_hi, preferred_element_type=F32)
            + jnp.dot(a_lo, b_hi, preferred_element_type=F32)
            + jnp.dot(a_hi, b_lo, preferred_element_type=F32))


def _ln_rows(v, g, b):
    mu = jnp.mean(v, axis=-1, keepdims=True)
    d = v - mu
    var = jnp.mean(d * d, axis=-1, keepdims=True)
    return d * lax.rsqrt(var + LN_EPS) * g + b


ADA_K_TILE = 256


def _ada_kernel(cs_ref, w_ref, b_ref, out_ref):
    k = pl.program_id(0)
    s = cs_ref[...]
    s = s * _sigmoid(s)
    w_hi, w_lo = _split_bf16(w_ref[...])
    part = _dot3(s, w_hi, w_lo)

    @pl.when(k == 0)
    def _():
        out_ref[...] = part + b_ref[...]

    @pl.when(k > 0)
    def _():
        out_ref[...] += part


def _ada(cs, w_ada, b_ada):
    rows, d = cs.shape
    cols = w_ada.shape[1]
    tk = ADA_K_TILE
    return pl.pallas_call(
        _ada_kernel,
        grid=(d // tk,),
        in_specs=[pl.BlockSpec((rows, tk), lambda k: (0, k)),
                  pl.BlockSpec((tk, cols), lambda k: (k, 0)),
                  pl.BlockSpec((1, cols), lambda k: (0, 0))],
        out_specs=pl.BlockSpec((rows, cols), lambda k: (0, 0)),
        out_shape=jax.ShapeDtypeStruct((rows, cols), F32),
        compiler_params=_params("arbitrary"),
        name="ada",
    )(cs, w_ada, b_ada)


def _log_sigmoid(v):
    return jnp.minimum(v, 0.0) - jnp.log(1.0 + jnp.exp(-jnp.abs(v)))


def _ctx_kernel(ctx_ref, sh_ref, sc_ref, wk_ref, wv0_ref, wv1_ref, bk_ref, bv0_ref, bv1_ref, df_ref, db_ref,
                sf_ref, sb_ref):
    lc = ctx_ref.shape[1]
    u = (ctx_ref[0] * (1.0 + sc_ref[0]) + sh_ref[0]).astype(BF16)
    k = (jnp.dot(u, wk_ref[...], preferred_element_type=F32) + bk_ref[...]) * RET_QK_DIM ** -0.5
    v = jnp.concatenate(
        [(jnp.dot(u, w[...], preferred_element_type=F32) + b[...]).astype(BF16)
         for w, b in ((wv0_ref, bv0_ref), (wv1_ref, bv1_ref))], axis=1)
    pos = lax.broadcasted_iota(I32, (lc, 1), 0).astype(F32)
    dn = (((0,), (0,)), ((), ()))
    for h in range(RET_HEADS):
        lgf = _log_sigmoid(df_ref[h])
        lgb = _log_sigmoid(db_ref[h])
        kh = k[:, h * RET_QK_DIM:(h + 1) * RET_QK_DIM]
        vh = v[:, h * RET_V_DIM:(h + 1) * RET_V_DIM]
        kf = (kh * jnp.exp((lc - 1.0 - pos) * lgf)).astype(BF16)
        kb = (kh * jnp.exp(pos * lgb)).astype(BF16)
        sf_ref[0, h] = lax.dot_general(kf, vh, dn, preferred_element_type=F32)
        sb_ref[0, h] = lax.dot_general(kb, vh, dn, preferred_element_type=F32)


def _ctx_states(ctx, mod3, w_in3, b_in3, dec_f, dec_b, ctx_row):
    b_, lc, d = ctx.shape
    assert RET_HEADS * RET_QK_DIM == d and RET_HEADS * RET_V_DIM == 2 * d
    st = jax.ShapeDtypeStruct((b_, RET_HEADS, RET_QK_DIM, RET_V_DIM), F32)
    st_spec = pl.BlockSpec((1, RET_HEADS, RET_QK_DIM, RET_V_DIM), lambda b: (b, 0, 0, 0))
    wblk = lambda blk: pl.BlockSpec((d, d), lambda b: (0, blk))
    bblk = lambda blk: pl.BlockSpec((1, d), lambda b: (0, blk))
    dec_spec = pl.BlockSpec((RET_HEADS, 1, 1), lambda b: (0, 0, 0))
    return pl.pallas_call(
        _ctx_kernel,
        grid=(b_,),
        in_specs=[pl.BlockSpec((1, lc, d), lambda b: (b, 0, 0)),
                  pl.BlockSpec((1, 1, d), lambda b: (ctx_row, 0, 0)),
                  pl.BlockSpec((1, 1, d), lambda b: (ctx_row, 0, 1)),
                  wblk(PB_K), wblk(PB_V), wblk(PB_V + 1),
                  bblk(PB_K), bblk(PB_V), bblk(PB_V + 1),
                  dec_spec, dec_spec],
        out_specs=[st_spec, st_spec],
        out_shape=[st, st],
        compiler_params=_params("arbitrary"),
        name="ctx",
    )(ctx, mod3, mod3, w_in3, w_in3, w_in3, b_in3, b_in3, b_in3, dec_f, dec_b)


PROJ_BLOCKS = 8
PB_Q, PB_K, PB_V, PB_G, PB_GATE = 0, 1, 2, 4, 6
WB_GLU, WB_GATE = 6, 8
PROJ_STEPS = PROJ_BLOCKS // 2 + 1


def _proj_kernel(x_ref, sh_ref, sc_ref, w_ref, b_ref, cos_ref, sin_ref, out_ref, z_ref, u_scr):
    j = pl.program_id(1)
    d = x_ref.shape[1]
    halves = (slice(0, d), slice(d, 2 * d))

    @pl.when(j == 0)
    def _():
        u_scr[...] = (x_ref[...] * (1.0 + sc_ref[0]) + sh_ref[0]).astype(BF16)

    def mm(blk):
        cols = slice(blk * d, (blk + 1) * d)
        return jnp.dot(u_scr[...], w_ref[:, cols], preferred_element_type=F32) + b_ref[:, cols]

    def rope(acc, half):
        cos = cos_ref[...]
        sin = sin_ref[...]
        for h in range(acc.shape[1] // RET_QK_DIM):
            seg = acc[:, h * RET_QK_DIM:(h + 1) * RET_QK_DIM]
            sl = slice(half.start + h * RET_QK_DIM, half.start + (h + 1) * RET_QK_DIM)
            out_ref[:, sl] = (seg * cos + pltpu.roll(seg, RET_QK_DIM // 2, 1) * sin).astype(BF16)

    def step(jj):
        if jj == 0:
            rope(mm(PB_Q), halves[0])
            rope(mm(PB_K) * RET_QK_DIM ** -0.5, halves[1])
        elif jj == 1:
            for hh, half in enumerate(halves):
                out_ref[:, half] = mm(PB_V + hh).astype(BF16)
        elif jj == 2:
            for hh, half in enumerate(halves):
                acc = mm(PB_G + hh)
                out_ref[:, half] = (acc * _sigmoid(acc)).astype(BF16)
        elif jj == 3:
            for hh, half in enumerate(halves):
                out_ref[:, half] = _sigmoid(mm(WB_GATE + hh)).astype(BF16)
        else:
            z_ref[...] = (mm(WB_GLU) * _sigmoid(mm(WB_GLU + 1))).astype(BF16)

    for jj in range(PROJ_STEPS):
        pl.when(j == jj)(functools.partial(step, jj))


def _proj(x2, mod3, w_in3, b_in3, cos, sin, seq, tm):
    n, d = x2.shape
    tiles_per_seq = seq // tm
    last_pair = PROJ_BLOCKS // 2 - 1
    return pl.pallas_call(
        _proj_kernel,
        grid=(n // tm, PROJ_STEPS),
        in_specs=[pl.BlockSpec((tm, d), lambda i, j: (i, 0)),
                  pl.BlockSpec((1, 1, d), lambda i, j: (i // tiles_per_seq, 0, 0)),
                  pl.BlockSpec((1, 1, d), lambda i, j: (i // tiles_per_seq, 0, 1)),
                  pl.BlockSpec(memory_space=pltpu.VMEM),
                  pl.BlockSpec(memory_space=pltpu.VMEM),
                  pl.BlockSpec((tm, RET_QK_DIM), lambda i, j: (i % tiles_per_seq, 0)),
                  pl.BlockSpec((tm, RET_QK_DIM), lambda i, j: (i % tiles_per_seq, 0))],
        out_specs=[pl.BlockSpec((tm, 2 * d), lambda i, j: (i, jnp.minimum(j, last_pair))),
                   pl.BlockSpec((tm, d), lambda i, j: (i, 0))],
        out_shape=[jax.ShapeDtypeStruct((n, PROJ_BLOCKS * d), BF16),
                   jax.ShapeDtypeStruct((n, d), BF16)],
        scratch_shapes=[pltpu.VMEM((tm, d), BF16)],
        compiler_params=_params("arbitrary", "arbitrary"),
        name="proj",
    )(x2, mod3, mod3, w_in3, b_in3, cos, sin)


def _ret_kernel(q_ref, k_ref, v_ref, g_ref, s0f_ref, s0b_ref, df_ref, db_ref, out_ref,
                sf_all, sb_all, p_scr, o_scr):
    c_ = RET_CHUNK
    n_chunks = q_ref.shape[1] // c_
    dn_t = (((0,), (0,)), ((), ()))
    lgf = _log_sigmoid(df_ref[0])
    lgb = _log_sigmoid(db_ref[0])
    ri = lax.broadcasted_iota(I32, (c_, c_), 0)
    ci = lax.broadcasted_iota(I32, (c_, c_), 1)
    diff = (ri - ci).astype(F32)
    mask = jnp.where(diff > 0, jnp.exp(diff * lgf), jnp.where(diff < 0, jnp.exp(-diff * lgb), 2.0))
    pos = lax.broadcasted_iota(I32, (c_, 1), 0).astype(F32)
    qdec_f = jnp.exp((pos + 1.0) * lgf)
    qdec_b = jnp.exp((c_ - pos) * lgb)
    kdec_f = jnp.exp((c_ - 1.0 - pos) * lgf)
    kdec_b = jnp.exp(pos * lgb)
    cdec_f = jnp.exp(c_ * lgf)
    cdec_b = jnp.exp(c_ * lgb)

    def chunk(c):
        return slice(c * c_, (c + 1) * c_)

    for c in range(n_chunks):
        k = k_ref[0, chunk(c), :].astype(F32)
        v = v_ref[0, chunk(c), :]
        sf_all[c] = lax.dot_general((k * kdec_f).astype(BF16), v, dn_t, preferred_element_type=F32)
        sb_all[c] = lax.dot_general((k * kdec_b).astype(BF16), v, dn_t, preferred_element_type=F32)

    sf = s0f_ref[0, 0]
    sb = s0b_ref[0, 0]
    for t in range(n_chunks):
        kv = sf_all[t]
        sf_all[t] = sf
        sf = sf * cdec_f + kv
        c = n_chunks - 1 - t
        kv = sb_all[c]
        sb_all[c] = sb
        sb = sb * cdec_b + kv

    for c in range(n_chunks):
        s = lax.dot_general(q_ref[0, chunk(c), :], k_ref[0, chunk(c), :], (((1,), (1,)), ((), ())),
                            preferred_element_type=F32)
        p_scr[c] = (s * mask).astype(BF16)

    for c in range(n_chunks):
        q = q_ref[0, chunk(c), :]
        o = jnp.dot(p_scr[c], v_ref[0, chunk(c), :], preferred_element_type=F32)
        o = o + qdec_f * jnp.dot(q, sf_all[c].astype(BF16), preferred_element_type=F32)
        o = o + qdec_b * jnp.dot(q, sb_all[c].astype(BF16), preferred_element_type=F32)
        o_scr[chunk(c), :] = o

    for c in range(n_chunks):
        o = o_scr[chunk(c), :]
        mu = jnp.mean(o, axis=-1, keepdims=True)
        d = o - mu
        var = jnp.mean(d * d, axis=-1, keepdims=True)
        on = d * lax.rsqrt(var + LN_EPS)
        out_ref[0, chunk(c), :] = (on * g_ref[0, chunk(c), :].astype(F32)).astype(BF16)


def _retention(proj3, s0f, s0b, dec_f, dec_b, d):
    b_, seq, _ = proj3.shape
    qb, kb = PB_Q * d // RET_QK_DIM, PB_K * d // RET_QK_DIM
    vb, gb = PB_V * d // RET_V_DIM, PB_G * d // RET_V_DIM
    n_chunks = seq // RET_CHUNK
    st_spec = pl.BlockSpec((1, 1, RET_QK_DIM, RET_V_DIM), lambda b, h: (b, h, 0, 0))
    st_all = pltpu.VMEM((n_chunks, RET_QK_DIM, RET_V_DIM), F32)
    return pl.pallas_call(
        _ret_kernel,
        grid=(b_, RET_HEADS),
        in_specs=[pl.BlockSpec((1, seq, RET_QK_DIM), lambda b, h: (b, 0, qb + h)),
                  pl.BlockSpec((1, seq, RET_QK_DIM), lambda b, h: (b, 0, kb + h)),
                  pl.BlockSpec((1, seq, RET_V_DIM), lambda b, h: (b, 0, vb + h)),
                  pl.BlockSpec((1, seq, RET_V_DIM), lambda b, h: (b, 0, gb + h)),
                  st_spec, st_spec,
                  pl.BlockSpec((1, 1, 1), lambda b, h: (h, 0, 0)),
                  pl.BlockSpec((1, 1, 1), lambda b, h: (h, 0, 0))],
        out_specs=pl.BlockSpec((1, seq, RET_V_DIM), lambda b, h: (b, 0, h)),
        out_shape=jax.ShapeDtypeStruct((b_, seq, RET_HEADS * RET_V_DIM), BF16),
        scratch_shapes=[st_all, st_all,
                        pltpu.VMEM((n_chunks, RET_CHUNK, RET_CHUNK), BF16),
                        pltpu.VMEM((seq, RET_V_DIM), F32)],
        compiler_params=_params("arbitrary", "arbitrary"),
        name="ret",
    )(proj3, proj3, proj3, proj3, s0f, s0b, dec_f, dec_b)


VEC_CONV_B, VEC_CLN_G, VEC_CLN_B, VEC_BCONV, VEC_BMIX, VEC_LN1_G, VEC_LN1_B = range(7)
CONV_ROWS = 64
YA_BLOCKS = 4
LANES = 128
RT_E1, RT_E2, RT_RANK1, RT_RANK2 = 0, 1, 2, 3
RT_W1, RT_W2 = 0, 1


def _merge_kernel(ret_ref, z_ref, zp_ref, zn_ref, ga_ref, gb_ref, x_ref, gt_ref, shf_ref, scf_ref,
                  wret_ref, wconv_ref, wmix_ref, dw_ref, vec_ref, wrh_ref, wrl_ref, br_ref,
                  h1_ref, t_ref, ri_ref, rw_ref, cnt_ref, zext, zc_scr, cnt_scr, ya_scr, *, tiles_per_seq):
    tm, d = x_ref.shape
    step = pl.program_id(0)
    ti = step % tiles_per_seq
    vec = vec_ref[...]

    def row(r):
        return vec[r:r + 1, :]


    has_prev = (ti > 0).astype(F32)
    has_next = (ti < tiles_per_seq - 1).astype(F32)
    zext[0:CONV_HALO, :] = zp_ref[...].astype(F32) * has_prev
    zext[CONV_HALO:CONV_HALO + tm, :] = z_ref[...].astype(F32)
    zext[CONV_HALO + tm:, :] = zn_ref[...].astype(F32) * has_next
    base = CONV_HALO - CONV_WIDTH // 2
    win_rows = CONV_ROWS + 2 * CONV_HALO

    n_lt = d // LANES
    kc = ret_ref.shape[1] // n_lt

    def conv_piece(r0, lt):
        ls = slice(lt * LANES, (lt + 1) * LANES)
        win = zext[pl.ds(r0, win_rows), ls]
        acc = jnp.zeros((CONV_ROWS, LANES), F32) + vec_ref[VEC_CONV_B:VEC_CONV_B + 1, ls]
        for s in range(SUBLANES):
            taps = [w for w in range(CONV_WIDTH) if (base + w) % SUBLANES == s]
            if not taps:
                continue
            sh = win if s == 0 else pltpu.roll(win, win_rows - s, 0)
            for w in taps:
                a = (base + w) - s
                acc = acc + sh[a:a + CONV_ROWS, :] * dw_ref[w:w + 1, ls]
        return acc

    def conv_rows(r, carry):
        r0 = pl.multiple_of(r * (2 * CONV_ROWS), 2 * CONV_ROWS)
        ya = None
        accs = []
        for lt in range(n_lt):
            part = jnp.dot(ret_ref[:, lt * kc:(lt + 1) * kc], wret_ref[r, lt * kc:(lt + 1) * kc, :],
                           preferred_element_type=F32)
            ya = part if ya is None else ya + part
            accs.append((conv_piece(r0, lt), conv_piece(r0 + CONV_ROWS, lt)))
        ya_scr[r] = ya
        for lt in range(n_lt):
            ls = slice(lt * LANES, (lt + 1) * LANES)
            zc_scr[pl.ds(r0, CONV_ROWS), ls] = accs[lt][0]
            zc_scr[pl.ds(r0 + CONV_ROWS, CONV_ROWS), ls] = accs[lt][1]
        return carry

    lax.fori_loop(0, YA_BLOCKS, conv_rows, 0)
    y_a = jnp.concatenate([ya_scr[q] for q in range(YA_BLOCKS)], axis=1)
    zc = _ln_rows(zc_scr[...], row(VEC_CLN_G), row(VEC_CLN_B))
    zc = zc * _sigmoid(zc)
    y_b = jnp.dot(zc.astype(BF16), wconv_ref[...], preferred_element_type=F32) + row(VEC_BCONV)

    mixed = ga_ref[...].astype(F32) * y_a + gb_ref[...].astype(F32) * y_b
    mix = jnp.dot(mixed.astype(BF16), wmix_ref[...], preferred_element_type=F32) + row(VEC_BMIX)
    h1 = _ln_rows(ALPHA * x_ref[...] + gt_ref[0] * mix, row(VEC_LN1_G), row(VEC_LN1_B))
    h1_ref[...] = h1
    t = h1 * (1.0 + scf_ref[0]) + shf_ref[0]
    t_ref[...] = t

    logits = _dot3(t, wrh_ref[...], wrl_ref[...]) + br_ref[...]

    lane = lax.broadcasted_iota(I32, logits.shape, 1).astype(F32)
    neg = -jnp.inf
    big = float(ROUTER_LANES)
    is_grp = lane < N_GROUPS
    gl = jnp.where(is_grp, logits, neg)
    gmax = jnp.max(gl, axis=-1, keepdims=True)
    gidx = jnp.min(jnp.where(gl == gmax, lane, big), axis=-1, keepdims=True)
    gsum = jnp.sum(jnp.where(is_grp, jnp.exp(gl - gmax), 0.0), axis=-1, keepdims=True)
    grp_w = 1.0 / gsum
    lo = N_GROUPS + gidx * EXPERTS_PER_GROUP
    el = jnp.where(lane >= lo, jnp.where(lane < lo + EXPERTS_PER_GROUP, logits, neg), neg)
    m1 = jnp.max(el, axis=-1, keepdims=True)
    i1 = jnp.min(jnp.where(el == m1, lane, big), axis=-1, keepdims=True)
    el2 = jnp.where(lane == i1, neg, el)
    m2 = jnp.max(el2, axis=-1, keepdims=True)
    i2 = jnp.min(jnp.where(el2 == m2, lane, big), axis=-1, keepdims=True)
    r = jnp.exp(m2 - m1)
    w1 = grp_w / (1.0 + r)
    w2 = grp_w * r / (1.0 + r)

    @pl.when(step == 0)
    def _():
        cnt_scr[...] = jnp.zeros_like(cnt_scr)

    oh1 = lane == i1
    oh2 = lane == i2
    oh = jnp.where(oh1, 1.0, jnp.where(oh2, 1.0, 0.0))
    tri = jnp.where(lax.broadcasted_iota(I32, (tm, tm), 0) > lax.broadcasted_iota(I32, (tm, tm), 1), 1.0, 0.0)
    before = jnp.dot(tri.astype(BF16), oh.astype(BF16), preferred_element_type=F32) + cnt_scr[0:1, :]
    rank1 = jnp.sum(jnp.where(oh1, before, 0.0), axis=-1, keepdims=True)
    rank2 = jnp.sum(jnp.where(oh2, before, 0.0), axis=-1, keepdims=True)
    cnt = cnt_scr[0:1, :] + jnp.sum(oh, axis=0, keepdims=True)
    cnt_scr[...] = jnp.broadcast_to(cnt, cnt_scr.shape)
    cnt_ref[...] = jnp.broadcast_to(cnt, cnt_ref.shape)

    e1 = i1 - N_GROUPS
    e2 = i2 - N_GROUPS
    ri = jnp.where(lane == RT_E1, e1, jnp.where(lane == RT_E2, e2,
                   jnp.where(lane == RT_RANK1, rank1, jnp.where(lane == RT_RANK2, rank2, 0.0))))
    ri_ref[...] = ri.astype(I32)
    rw_ref[...] = jnp.where(lane == RT_W1, w1, jnp.where(lane == RT_W2, w2, 0.0))


def _merge(ret2, proj2, z2, x2, mod3, wret, wconv, wmix, dw, vec, wrh, wrl, br, seq, tm):
    n, d = x2.shape
    assert tm == YA_BLOCKS * 2 * CONV_ROWS
    tiles_per_seq = seq // tm
    hb = tm // CONV_HALO
    last_hb = n // CONV_HALO - 1
    const = lambda i: (0, 0)
    bat = lambda k: (lambda i: (i // tiles_per_seq, 0, k))
    return pl.pallas_call(
        functools.partial(_merge_kernel, tiles_per_seq=tiles_per_seq),
        grid=(n // tm,),
        in_specs=[pl.BlockSpec((tm, ret2.shape[1]), lambda i: (i, 0)),
                  pl.BlockSpec((tm, d), lambda i: (i, 0)),
                  pl.BlockSpec((CONV_HALO, d), lambda i: (jnp.maximum(i * hb - 1, 0), 0)),
                  pl.BlockSpec((CONV_HALO, d), lambda i: (jnp.minimum((i + 1) * hb, last_hb), 0)),
                  pl.BlockSpec((tm, d), lambda i: (i, PB_GATE)),
                  pl.BlockSpec((tm, d), lambda i: (i, PB_GATE + 1)),
                  pl.BlockSpec((tm, d), lambda i: (i, 0)),
                  pl.BlockSpec((1, 1, d), bat(2)),
                  pl.BlockSpec((1, 1, d), bat(3)),
                  pl.BlockSpec((1, 1, d), bat(4)),
                  pl.BlockSpec(wret.shape, lambda i: (0, 0, 0)),
                  pl.BlockSpec(wconv.shape, const),
                  pl.BlockSpec(wmix.shape, const),
                  pl.BlockSpec(dw.shape, const),
                  pl.BlockSpec(vec.shape, const),
                  pl.BlockSpec(wrh.shape, const),
                  pl.BlockSpec(wrl.shape, const),
                  pl.BlockSpec(br.shape, const)],
        out_specs=[pl.BlockSpec((tm, d), lambda i: (i, 0)),
                   pl.BlockSpec((tm, d), lambda i: (i, 0)),
                   pl.BlockSpec((tm, ROUTER_LANES), lambda i: (i, 0)),
                   pl.BlockSpec((tm, ROUTER_LANES), lambda i: (i, 0)),
                   pl.BlockSpec((SUBLANES, ROUTER_LANES), const)],
        out_shape=[jax.ShapeDtypeStruct((n, d), F32),
                   jax.ShapeDtypeStruct((n, d), F32),
                   jax.ShapeDtypeStruct((n, ROUTER_LANES), I32),
                   jax.ShapeDtypeStruct((n, ROUTER_LANES), F32),
                   jax.ShapeDtypeStruct((SUBLANES, ROUTER_LANES), F32)],
        scratch_shapes=[pltpu.VMEM((tm + 2 * CONV_HALO, d), F32),
                        pltpu.VMEM((tm, d), F32),
                        pltpu.VMEM((SUBLANES, ROUTER_LANES), F32),
                        pltpu.VMEM((YA_BLOCKS, tm, d // YA_BLOCKS), F32)],
        compiler_params=_params("arbitrary"),
        name="merge",
    )(ret2, z2, z2, z2, proj2, proj2, x2, mod3, mod3, mod3,
      wret, wconv, wmix, dw, vec, wrh, wrl, br)


DISPATCH_TILE = 512


def _dispatch_kernel(pos_ref, t_ref, xs_hbm, sem):
    tm = t_ref.shape[0]
    base = pl.program_id(0) * tm

    for r in range(tm):
        for k in range(EXPERT_TOP_K):
            dst = pos_ref[(base + r) * EXPERT_TOP_K + k]
            pltpu.make_async_copy(t_ref.at[pl.ds(r, 1)], xs_hbm.at[pl.ds(dst, 1)], sem).start(priority=k)
    for k in range(EXPERT_TOP_K):
        pltpu.make_async_copy(t_ref, xs_hbm.at[pl.ds(0, tm)], sem).wait()


def _dispatch(pos_flat, t2):
    n, d = t2.shape
    tm = min(DISPATCH_TILE, n)
    return pl.pallas_call(
        _dispatch_kernel,
        grid=(n // tm,),
        in_specs=[pl.BlockSpec(memory_space=pltpu.SMEM),
                  pl.BlockSpec((tm, d), lambda i: (i, 0))],
        out_specs=pl.BlockSpec(memory_space=pl.ANY),
        out_shape=jax.ShapeDtypeStruct((n * EXPERT_TOP_K, d), F32),
        scratch_shapes=[pltpu.SemaphoreType.DMA],
        compiler_params=_params("arbitrary"),
        name="dispatch",
    )(pos_flat, t2)


EXPERT_TILE = 512


def _experts_kernel(ie_ref, it_ref, lo_ref, hi_ref, nv_ref, xs_ref, wg_ref, wu_ref, wd_ref, ys_ref,
                    wg_bf, wu_bf, wd_bf):
    k = pl.program_id(0)
    prev = jnp.maximum(k - 1, 0)
    new_expert = (k == 0) | (ie_ref[k] != ie_ref[prev])
    new_tile = (k == 0) | (it_ref[k] != it_ref[prev])

    @pl.when(new_expert)
    def _():
        wg_bf[...] = wg_ref[0, 0].astype(BF16)
        wu_bf[...] = wu_ref[0, 0].astype(BF16)
        wd_bf[...] = wd_ref[0, 0].astype(BF16)

    @pl.when(k < nv_ref[0])
    def _():
        x = xs_ref[...].astype(BF16)
        g = jnp.dot(x, wg_bf[...], preferred_element_type=F32)
        u = jnp.dot(x, wu_bf[...], preferred_element_type=F32)
        hid = (g * _sigmoid(g) * u).astype(BF16)
        y = jnp.dot(hid, wd_bf[...], preferred_element_type=F32)
        rows = lax.broadcasted_iota(I32, (xs_ref.shape[0], 1), 0)
        mine = (rows >= lo_ref[k]) & (rows < hi_ref[k])

        @pl.when(new_tile)
        def _():
            ys_ref[...] = jnp.where(mine, y, 0.0)

        @pl.when(jnp.logical_not(new_tile))
        def _():
            ys_ref[...] = jnp.where(mine, y, ys_ref[...])


def _experts(meta, xs, w_gate, w_up, w_down):
    p, d = xs.shape
    ff = w_gate.shape[-1]
    n_items = meta[0].shape[0]
    grid_spec = pltpu.PrefetchScalarGridSpec(
        num_scalar_prefetch=5,
        grid=(n_items,),
        in_specs=[pl.BlockSpec((EXPERT_TILE, d), lambda k, ie, it, lo, hi, nv: (it[k], 0)),
                  pl.BlockSpec((1, 1, d, ff), lambda k, ie, it, lo, hi, nv: (0, ie[k], 0, 0)),
                  pl.BlockSpec((1, 1, d, ff), lambda k, ie, it, lo, hi, nv: (0, ie[k], 0, 0)),
                  pl.BlockSpec((1, 1, ff, d), lambda k, ie, it, lo, hi, nv: (0, ie[k], 0, 0))],
        out_specs=pl.BlockSpec((EXPERT_TILE, d), lambda k, ie, it, lo, hi, nv: (it[k], 0)),
        scratch_shapes=[pltpu.VMEM((d, ff), BF16), pltpu.VMEM((d, ff), BF16), pltpu.VMEM((ff, d), BF16)])
    return pl.pallas_call(
        _experts_kernel,
        grid_spec=grid_spec,
        out_shape=jax.ShapeDtypeStruct((p, d), F32),
        compiler_params=_params("arbitrary"),
        name="experts",
    )(*meta, xs, w_gate, w_up, w_down)


def _expert_work_items(counts, p):
    n_tiles = p // EXPERT_TILE
    n_items = n_tiles + N_EXPERTS - 1
    offs = jnp.concatenate([jnp.zeros((1,), I32), jnp.cumsum(counts)])
    first = offs[:-1] // EXPERT_TILE
    last = (offs[1:] - 1) // EXPERT_TILE
    per = jnp.where(counts > 0, last - first + 1, 0)
    ends = jnp.cumsum(per)
    starts = ends - per
    n_valid = ends[-1]
    k = jnp.arange(n_items, dtype=I32)
    kk = jnp.minimum(k, n_valid - 1)
    ie = jnp.sum((ends[None, :] <= kk[:, None]).astype(I32), axis=1)
    it = first[ie] + (kk - starts[ie])
    lo = jnp.maximum(offs[ie], it * EXPERT_TILE) - it * EXPERT_TILE
    hi = jnp.minimum(offs[ie + 1], (it + 1) * EXPERT_TILE) - it * EXPERT_TILE
    valid = k < n_valid
    lo = jnp.where(valid, lo, 0)
    hi = jnp.where(valid, hi, 0)
    return offs, (ie.astype(I32), it.astype(I32), lo.astype(I32), hi.astype(I32), n_valid.reshape(1).astype(I32))


COMBINE_HALF = 256


def _combine_kernel(pos_ref, ys_hbm, rw_ref, h1_ref, gt_ref, ln_ref, out_ref, buf, sem):
    i = pl.program_id(0)
    n_steps = pl.num_programs(0)
    hm = COMBINE_HALF
    base = i * (2 * hm)

    def slot_copy(s, k):
        return pltpu.make_async_copy(ys_hbm.at[pl.ds(0, hm)], buf.at[s, k], sem.at[s])

    def gather(first_tok, s):
        for r in range(hm):
            for k in range(EXPERT_TOP_K):
                src = pos_ref[(first_tok + r) * EXPERT_TOP_K + k]
                pltpu.make_async_copy(ys_hbm.at[pl.ds(src, 1)], buf.at[s, k, pl.ds(r, 1)],
                                      sem.at[s]).start(priority=k)

    def finish(s):
        for k in range(EXPERT_TOP_K):
            slot_copy(s, k).wait()
        rows = slice(s * hm, (s + 1) * hm)
        rw = rw_ref[rows, :]
        y = rw[:, RT_W1:RT_W1 + 1] * buf[s, 0] + rw[:, RT_W2:RT_W2 + 1] * buf[s, 1]
        out_ref[rows, :] = _ln_rows(ALPHA * h1_ref[rows, :] + gt_ref[0] * y, ln_ref[0:1, :], ln_ref[1:2, :])

    @pl.when(i == 0)
    def _():
        gather(0, 0)

    gather(base + hm, 1)
    finish(0)

    @pl.when(i + 1 < n_steps)
    def _():
        gather(base + 2 * hm, 0)

    finish(1)


def _combine(pos_flat, ys, rw, h1, mod3, ln2, seq):
    n, d = h1.shape
    tm = 2 * COMBINE_HALF
    tiles_per_seq = seq // tm
    return pl.pallas_call(
        _combine_kernel,
        grid=(n // tm,),
        in_specs=[pl.BlockSpec(memory_space=pltpu.SMEM),
                  pl.BlockSpec(memory_space=pl.ANY),
                  pl.BlockSpec((tm, ROUTER_LANES), lambda i: (i, 0)),
                  pl.BlockSpec((tm, d), lambda i: (i, 0)),
                  pl.BlockSpec((1, 1, d), lambda i: (i // tiles_per_seq, 0, 5)),
                  pl.BlockSpec(ln2.shape, lambda i: (0, 0))],
        out_specs=pl.BlockSpec((tm, d), lambda i: (i, 0)),
        out_shape=jax.ShapeDtypeStruct((n, d), F32),
        scratch_shapes=[pltpu.VMEM((2, EXPERT_TOP_K, COMBINE_HALF, d), F32),
                        pltpu.SemaphoreType.DMA((2,))],
        compiler_params=_params("arbitrary"),
        name="combine",
    )(pos_flat, ys, rw, h1, mod3, ln2)


def _rope_tables(seq):
    pos = np.arange(seq)
    r = (pos // GRID_W).astype(np.float32)
    col = (pos % GRID_W).astype(np.float32)
    n_freq = RET_QK_DIM // 4
    inv = (np.float32(ROPE_BASE) ** (-np.arange(n_freq, dtype=np.float32) / np.float32(n_freq))).astype(np.float32)
    ang = np.concatenate([r[:, None] * inv, col[:, None] * inv], axis=-1)
    ang = np.concatenate([ang, ang], axis=-1).astype(np.float32)
    sign = np.concatenate([-np.ones((RET_QK_DIM // 2,), np.float32), np.ones((RET_QK_DIM // 2,), np.float32)])
    return jnp.asarray(np.cos(ang), F32), jnp.asarray(np.sin(ang) * sign, F32)


def kernel(x, c, ctx, c_ctx, w_ada, b_ada, w_in, b_in, ret_decay_fwd, ret_decay_bwd, w_ret_out, conv_dw, conv_dw_b, conv_ln_g, conv_ln_b, w_conv_out, b_conv_out, w_mix_out, b_mix_out, ln1_g, ln1_b, w_router_grp, b_router_grp, w_router_exp, b_router_exp, w_exp_gate, w_exp_up, w_exp_down, ln2_g, ln2_b):
    b_, seq, d = x.shape
    n = b_ * seq
    assert w_ada.shape[0] == DEPTH
    mod_rows = SUBLANES
    assert b_ + 1 <= mod_rows

    cs = jnp.concatenate([c, c_ctx[None, :], jnp.zeros((mod_rows - b_ - 1, d), F32)], axis=0)
    mod = _ada(cs, w_ada[0], b_ada[0][None, :])
    mod3 = mod.reshape(mod_rows, 1, 6 * d)

    w_in3 = w_in[0].astype(BF16)
    b_in3 = b_in[0][None, :]
    dec_f = ret_decay_fwd[0].reshape(RET_HEADS, 1, 1)
    dec_b = ret_decay_bwd[0].reshape(RET_HEADS, 1, 1)

    s0f, s0b = _ctx_states(ctx, mod3, w_in3, b_in3, dec_f, dec_b, b_)

    cos, sin = _rope_tables(seq)
    x2 = x.reshape(n, d)
    proj, z = _proj(x2, mod3, w_in3, b_in3, cos, sin, seq, tm=min(1024, seq))

    ret = _retention(proj.reshape(b_, seq, PROJ_BLOCKS * d), s0f, s0b, dec_f, dec_b, d)

    vec = jnp.concatenate([conv_dw_b, conv_ln_g, conv_ln_b, b_conv_out, b_mix_out, ln1_g, ln1_b,
                           jnp.zeros((1, d), F32)], axis=0)
    tap_rows = -(-CONV_WIDTH // SUBLANES) * SUBLANES
    dw = jnp.concatenate([conv_dw[0], jnp.zeros((tap_rows - CONV_WIDTH, d), F32)], axis=0)
    pad = ROUTER_LANES - N_GROUPS - N_EXPERTS
    wr = jnp.concatenate([w_router_grp[0], w_router_exp[0], jnp.zeros((d, pad), F32)], axis=1)
    wr_hi = wr.astype(BF16)
    wr_lo = (wr - wr_hi.astype(F32)).astype(BF16)
    br = jnp.concatenate([b_router_grp[0], b_router_exp[0], jnp.zeros((pad,), F32)])[None, :]
    h1, t, route_i, route_w, cnt = _merge(
        ret.reshape(n, RET_HEADS * RET_V_DIM), proj, z, x2, mod3,
        w_ret_out[0].astype(BF16).reshape(-1, YA_BLOCKS, d // YA_BLOCKS).transpose(1, 0, 2),
        w_conv_out[0].astype(BF16), w_mix_out[0].astype(BF16),
        dw, vec, wr_hi, wr_lo, br, seq, tm=min(512, seq))

    counts = cnt[0, N_GROUPS:N_GROUPS + N_EXPERTS].astype(I32)
    offs, meta = _expert_work_items(counts, n * EXPERT_TOP_K)
    eid = route_i[:, RT_E1:RT_E2 + 1]
    seg_start = jnp.sum(jnp.where(eid[:, :, None] == jnp.arange(N_EXPERTS, dtype=I32), offs[:N_EXPERTS], 0), axis=-1)
    pos = seg_start + route_i[:, RT_RANK1:RT_RANK2 + 1]
    pos_flat = pos.reshape(n * EXPERT_TOP_K)

    xs = _dispatch(pos_flat, t)
    ys = _experts(meta, xs, w_exp_gate, w_exp_up, w_exp_down)
    ln2 = jnp.concatenate([ln2_g, ln2_b, jnp.zeros((SUBLANES - 2, d), F32)], axis=0)
    out = _combine(pos_flat, ys, route_w, h1, mod3, ln2, seq)
    return out.reshape(b_, seq, d)
```

```python
import functools

import jax
import jax.numpy as jnp
import numpy as np
from jax import lax
from jax.experimental import pallas as pl
from jax.experimental.pallas import tpu as pltpu

GRID_W = 64
RET_HEADS = 8
RET_QK_DIM = 128
RET_V_DIM = 256
RET_CHUNK = 256
ROPE_BASE = 10000.0
CONV_WIDTH = 31
CONV_HALO = 16
SUBLANES = 8
N_GROUPS = 4
EXPERTS_PER_GROUP = 8
N_EXPERTS = N_GROUPS * EXPERTS_PER_GROUP
EXPERT_TOP_K = 2
LN_EPS = 1e-5
DEPTH = 1
ALPHA = (2.0 * DEPTH) ** 0.25
ROUTER_LANES = 128

V7X_VMEM_LIMIT = 56 * 1024 * 1024

F32 = jnp.float32
BF16 = jnp.bfloat16
I32 = jnp.int32


def _params(*sem):
    return pltpu.CompilerParams(dimension_semantics=sem, vmem_limit_bytes=V7X_VMEM_LIMIT)


def _sigmoid(v):
    return 0.5 * jnp.tanh(0.5 * v) + 0.5


def _split_bf16(v):
    hi = v.astype(BF16)
    return hi, (v - hi.astype(F32)).astype(BF16)


def _dot3(a, b_hi, b_lo):
    a_hi, a_lo = _split_bf16(a)
    return (jnp.dot(a_hi, b_hi, preferred_element_type=F32)
            + jnp.dot(a_lo, b_hi, preferred_element_type=F32)
            + jnp.dot(a_hi, b_lo, preferred_element_type=F32))


def _ln_rows(v, g, b):
    mu = jnp.mean(v, axis=-1, keepdims=True)
    d = v - mu
    var = jnp.mean(d * d, axis=-1, keepdims=True)
    return d * lax.rsqrt(var + LN_EPS) * g + b


ADA_K_TILE = 256


def _ada_kernel(cs_ref, w_ref, b_ref, out_ref):
    k = pl.program_id(0)
    s = cs_ref[...]
    s = s * _sigmoid(s)
    w_hi, w_lo = _split_bf16(w_ref[...])
    part = _dot3(s, w_hi, w_lo)

    @pl.when(k == 0)
    def _():
        out_ref[...] = part + b_ref[...]

    @pl.when(k > 0)
    def _():
        out_ref[...] += part


def _ada(cs, w_ada, b_ada):
    rows, d = cs.shape
    cols = w_ada.shape[1]
    tk = ADA_K_TILE
    return pl.pallas_call(
        _ada_kernel,
        grid=(d // tk,),
        in_specs=[pl.BlockSpec((rows, tk), lambda k: (0, k)),
                  pl.BlockSpec((tk, cols), lambda k: (k, 0)),
                  pl.BlockSpec((1, cols), lambda k: (0, 0))],
        out_specs=pl.BlockSpec((rows, cols), lambda k: (0, 0)),
        out_shape=jax.ShapeDtypeStruct((rows, cols), F32),
        compiler_params=_params("arbitrary"),
        name="ada",
    )(cs, w_ada, b_ada)


def _log_sigmoid(v):
    return jnp.minimum(v, 0.0) - jnp.log(1.0 + jnp.exp(-jnp.abs(v)))


def _ctx_kernel(ctx_ref, sh_ref, sc_ref, wk_ref, wv0_ref, wv1_ref, bk_ref, bv0_ref, bv1_ref, df_ref, db_ref,
                sf_ref, sb_ref):
    lc = ctx_ref.shape[1]
    u = (ctx_ref[0] * (1.0 + sc_ref[0]) + sh_ref[0]).astype(BF16)
    k = (jnp.dot(u, wk_ref[...], preferred_element_type=F32) + bk_ref[...]) * RET_QK_DIM ** -0.5
    v = jnp.concatenate(
        [(jnp.dot(u, w[...], preferred_element_type=F32) + b[...]).astype(BF16)
         for w, b in ((wv0_ref, bv0_ref), (wv1_ref, bv1_ref))], axis=1)
    pos = lax.broadcasted_iota(I32, (lc, 1), 0).astype(F32)
    dn = (((0,), (0,)), ((), ()))
    for h in range(RET_HEADS):
        lgf = _log_sigmoid(df_ref[h])
        lgb = _log_sigmoid(db_ref[h])
        kh = k[:, h * RET_QK_DIM:(h + 1) * RET_QK_DIM]
        vh = v[:, h * RET_V_DIM:(h + 1) * RET_V_DIM]
        kf = (kh * jnp.exp((lc - 1.0 - pos) * lgf)).astype(BF16)
        kb = (kh * jnp.exp(pos * lgb)).astype(BF16)
        sf_ref[0, h] = lax.dot_general(kf, vh, dn, preferred_element_type=F32)
        sb_ref[0, h] = lax.dot_general(kb, vh, dn, preferred_element_type=F32)


def _ctx_states(ctx, mod3, w_in3, b_in3, dec_f, dec_b, ctx_row):
    b_, lc, d = ctx.shape
    assert RET_HEADS * RET_QK_DIM == d and RET_HEADS * RET_V_DIM == 2 * d
    st = jax.ShapeDtypeStruct((b_, RET_HEADS, RET_QK_DIM, RET_V_DIM), F32)
    st_spec = pl.BlockSpec((1, RET_HEADS, RET_QK_DIM, RET_V_DIM), lambda b: (b, 0, 0, 0))
    wblk = lambda blk: pl.BlockSpec((d, d), lambda b: (0, blk))
    bblk = lambda blk: pl.BlockSpec((1, d), lambda b: (0, blk))
    dec_spec = pl.BlockSpec((RET_HEADS, 1, 1), lambda b: (0, 0, 0))
    return pl.pallas_call(
        _ctx_kernel,
        grid=(b_,),
        in_specs=[pl.BlockSpec((1, lc, d), lambda b: (b, 0, 0)),
                  pl.BlockSpec((1, 1, d), lambda b: (ctx_row, 0, 0)),
                  pl.BlockSpec((1, 1, d), lambda b: (ctx_row, 0, 1)),
                  wblk(PB_K), wblk(PB_V), wblk(PB_V + 1),
                  bblk(PB_K), bblk(PB_V), bblk(PB_V + 1),
                  dec_spec, dec_spec],
        out_specs=[st_spec, st_spec],
        out_shape=[st, st],
        compiler_params=_params("arbitrary"),
        name="ctx",
    )(ctx, mod3, mod3, w_in3, w_in3, w_in3, b_in3, b_in3, b_in3, dec_f, dec_b)


PROJ_BLOCKS = 8
PB_Q, PB_K, PB_V, PB_G, PB_GATE = 0, 1, 2, 4, 6
WB_GLU, WB_GATE = 6, 8
PROJ_STEPS = PROJ_BLOCKS // 2 + 1


def _proj_kernel(x_ref, sh_ref, sc_ref, w_ref, b_ref, cos_ref, sin_ref, out_ref, z_ref, u_scr):
    j = pl.program_id(1)
    d = x_ref.shape[1]
    halves = (slice(0, d), slice(d, 2 * d))

    @pl.when(j == 0)
    def _():
        u_scr[...] = (x_ref[...] * (1.0 + sc_ref[0]) + sh_ref[0]).astype(BF16)

    def mm(blk):
        cols = slice(blk * d, (blk + 1) * d)
        return jnp.dot(u_scr[...], w_ref[:, cols], preferred_element_type=F32) + b_ref[:, cols]

    def rope(acc, half):
        cos = cos_ref[...]
        sin = sin_ref[...]
        for h in range(acc.shape[1] // RET_QK_DIM):
            seg = acc[:, h * RET_QK_DIM:(h + 1) * RET_QK_DIM]
            sl = slice(half.start + h * RET_QK_DIM, half.start + (h + 1) * RET_QK_DIM)
            out_ref[:, sl] = (seg * cos + pltpu.roll(seg, RET_QK_DIM // 2, 1) * sin).astype(BF16)

    def step(jj):
        if jj == 0:
            rope(mm(PB_Q), halves[0])
            rope(mm(PB_K) * RET_QK_DIM ** -0.5, halves[1])
        elif jj == 1:
            for hh, half in enumerate(halves):
                out_ref[:, half] = mm(PB_V + hh).astype(BF16)
        elif jj == 2:
            for hh, half in enumerate(halves):
                acc = mm(PB_G + hh)
                out_ref[:, half] = (acc * _sigmoid(acc)).astype(BF16)
        elif jj == 3:
            for hh, half in enumerate(halves):
                out_ref[:, half] = _sigmoid(mm(WB_GATE + hh)).astype(BF16)
        else:
            z_ref[...] = (mm(WB_GLU) * _sigmoid(mm(WB_GLU + 1))).astype(BF16)

    for jj in range(PROJ_STEPS):
        pl.when(j == jj)(functools.partial(step, jj))


def _proj(x2, mod3, w_in3, b_in3, cos, sin, seq, tm):
    n, d = x2.shape
    tiles_per_seq = seq // tm
    last_pair = PROJ_BLOCKS // 2 - 1
    return pl.pallas_call(
        _proj_kernel,
        grid=(n // tm, PROJ_STEPS),
        in_specs=[pl.BlockSpec((tm, d), lambda i, j: (i, 0)),
                  pl.BlockSpec((1, 1, d), lambda i, j: (i // tiles_per_seq, 0, 0)),
                  pl.BlockSpec((1, 1, d), lambda i, j: (i // tiles_per_seq, 0, 1)),
                  pl.BlockSpec(memory_space=pltpu.VMEM),
                  pl.BlockSpec(memory_space=pltpu.VMEM),
                  pl.BlockSpec((tm, RET_QK_DIM), lambda i, j: (i % tiles_per_seq, 0)),
                  pl.BlockSpec((tm, RET_QK_DIM), lambda i, j: (i % tiles_per_seq, 0))],
        out_specs=[pl.BlockSpec((tm, 2 * d), lambda i, j: (i, jnp.minimum(j, last_pair))),
                   pl.BlockSpec((tm, d), lambda i, j: (i, 0))],
        out_shape=[jax.ShapeDtypeStruct((n, PROJ_BLOCKS * d), BF16),
                   jax.ShapeDtypeStruct((n, d), BF16)],
        scratch_shapes=[pltpu.VMEM((tm, d), BF16)],
        compiler_params=_params("arbitrary", "arbitrary"),
        name="proj",
    )(x2, mod3, mod3, w_in3, b_in3, cos, sin)


def _ret_kernel(q_ref, k_ref, v_ref, g_ref, s0f_ref, s0b_ref, df_ref, db_ref, out_ref,
                sf_all, sb_all, p_scr, o_scr):
    c_ = RET_CHUNK
    n_chunks = q_ref.shape[1] // c_
    dn_t = (((0,), (0,)), ((), ()))
    lgf = _log_sigmoid(df_ref[0])
    lgb = _log_sigmoid(db_ref[0])
    ri = lax.broadcasted_iota(I32, (c_, c_), 0)
    ci = lax.broadcasted_iota(I32, (c_, c_), 1)
    diff = (ri - ci).astype(F32)
    mask = jnp.where(diff > 0, jnp.exp(diff * lgf), jnp.where(diff < 0, jnp.exp(-diff * lgb), 2.0))
    pos = lax.broadcasted_iota(I32, (c_, 1), 0).astype(F32)
    qdec_f = jnp.exp((pos + 1.0) * lgf)
    qdec_b = jnp.exp((c_ - pos) * lgb)
    kdec_f = jnp.exp((c_ - 1.0 - pos) * lgf)
    kdec_b = jnp.exp(pos * lgb)
    cdec_f = jnp.exp(c_ * lgf)
    cdec_b = jnp.exp(c_ * lgb)

    def chunk(c):
        return slice(c * c_, (c + 1) * c_)

    for c in range(n_chunks):
        k = k_ref[0, chunk(c), :].astype(F32)
        v = v_ref[0, chunk(c), :]
        sf_all[c] = lax.dot_general((k * kdec_f).astype(BF16), v, dn_t, preferred_element_type=F32)
        sb_all[c] = lax.dot_general((k * kdec_b).astype(BF16), v, dn_t, preferred_element_type=F32)

    sf = s0f_ref[0, 0]
    sb = s0b_ref[0, 0]
    for t in range(n_chunks):
        kv = sf_all[t]
        sf_all[t] = sf
        sf = sf * cdec_f + kv
        c = n_chunks - 1 - t
        kv = sb_all[c]
        sb_all[c] = sb
        sb = sb * cdec_b + kv

    for c in range(n_chunks):
        s = lax.dot_general(q_ref[0, chunk(c), :], k_ref[0, chunk(c), :], (((1,), (1,)), ((), ())),
                            preferred_element_type=F32)
        p_scr[c] = (s * mask).astype(BF16)

    for c in range(n_chunks):
        q = q_ref[0, chunk(c), :]
        o = jnp.dot(p_scr[c], v_ref[0, chunk(c), :], preferred_element_type=F32)
        o = o + qdec_f * jnp.dot(q, sf_all[c].astype(BF16), preferred_element_type=F32)
        o = o + qdec_b * jnp.dot(q, sb_all[c].astype(BF16), preferred_element_type=F32)
        o_scr[chunk(c), :] = o

    for c in range(n_chunks):
        o = o_scr[chunk(c), :]
        mu = jnp.mean(o, axis=-1, keepdims=True)
        d = o - mu
        var = jnp.mean(d * d, axis=-1, keepdims=True)
        on = d * lax.rsqrt(var + LN_EPS)
        out_ref[0, chunk(c), :] = (on * g_ref[0, chunk(c), :].astype(F32)).astype(BF16)


def _retention(proj3, s0f, s0b, dec_f, dec_b, d):
    b_, seq, _ = proj3.shape
    qb, kb = PB_Q * d // RET_QK_DIM, PB_K * d // RET_QK_DIM
    vb, gb = PB_V * d // RET_V_DIM, PB_G * d // RET_V_DIM
    n_chunks = seq // RET_CHUNK
    st_spec = pl.BlockSpec((1, 1, RET_QK_DIM, RET_V_DIM), lambda b, h: (b, h, 0, 0))
    st_all = pltpu.VMEM((n_chunks, RET_QK_DIM, RET_V_DIM), F32)
    return pl.pallas_call(
        _ret_kernel,
        grid=(b_, RET_HEADS),
        in_specs=[pl.BlockSpec((1, seq, RET_QK_DIM), lambda b, h: (b, 0, qb + h)),
                  pl.BlockSpec((1, seq, RET_QK_DIM), lambda b, h: (b, 0, kb + h)),
                  pl.BlockSpec((1, seq, RET_V_DIM), lambda b, h: (b, 0, vb + h)),
                  pl.BlockSpec((1, seq, RET_V_DIM), lambda b, h: (b, 0, gb + h)),
                  st_spec, st_spec,
                  pl.BlockSpec((1, 1, 1), lambda b, h: (h, 0, 0)),
                  pl.BlockSpec((1, 1, 1), lambda b, h: (h, 0, 0))],
        out_specs=pl.BlockSpec((1, seq, RET_V_DIM), lambda b, h: (b, 0, h)),
        out_shape=jax.ShapeDtypeStruct((b_, seq, RET_HEADS * RET_V_DIM), BF16),
        scratch_shapes=[st_all, st_all,
                        pltpu.VMEM((n_chunks, RET_CHUNK, RET_CHUNK), BF16),
                        pltpu.VMEM((seq, RET_V_DIM), F32)],
        compiler_params=_params("arbitrary", "arbitrary"),
        name="ret",
    )(proj3, proj3, proj3, proj3, s0f, s0b, dec_f, dec_b)


VEC_CONV_B, VEC_CLN_G, VEC_CLN_B, VEC_BCONV, VEC_BMIX, VEC_LN1_G, VEC_LN1_B = range(7)
CONV_ROWS = 64
YA_BLOCKS = 4
LANES = 128
RT_E1, RT_E2, RT_RANK1, RT_RANK2 = 0, 1, 2, 3
RT_W1, RT_W2 = 0, 1


def _merge_kernel(ret_ref, z_ref, zp_ref, zn_ref, ga_ref, gb_ref, x_ref, gt_ref, shf_ref, scf_ref,
                  wret_ref, wconv_ref, wmix_ref, dw_ref, vec_ref, wrh_ref, wrl_ref, br_ref,
                  h1_ref, t_ref, ri_ref, rw_ref, cnt_ref, zext, zc_scr, cnt_scr, ya_scr, *, tiles_per_seq):
    tm, d = x_ref.shape
    step = pl.program_id(0)
    ti = step % tiles_per_seq
    vec = vec_ref[...]

    def row(r):
        return vec[r:r + 1, :]


    has_prev = (ti > 0).astype(F32)
    has_next = (ti < tiles_per_seq - 1).astype(F32)
    zext[0:CONV_HALO, :] = zp_ref[...].astype(F32) * has_prev
    zext[CONV_HALO:CONV_HALO + tm, :] = z_ref[...].astype(F32)
    zext[CONV_HALO + tm:, :] = zn_ref[...].astype(F32) * has_next
    base = CONV_HALO - CONV_WIDTH // 2
    win_rows = CONV_ROWS + 2 * CONV_HALO

    n_lt = d // LANES
    kc = ret_ref.shape[1] // n_lt

    def conv_piece(r0, lt):
        ls = slice(lt * LANES, (lt + 1) * LANES)
        win = zext[pl.ds(r0, win_rows), ls]
        acc = jnp.zeros((CONV_ROWS, LANES), F32) + vec_ref[VEC_CONV_B:VEC_CONV_B + 1, ls]
        for s in range(SUBLANES):
            taps = [w for w in range(CONV_WIDTH) if (base + w) % SUBLANES == s]
            if not taps:
                continue
            sh = win if s == 0 else pltpu.roll(win, win_rows - s, 0)
            for w in taps:
                a = (base + w) - s
                acc = acc + sh[a:a + CONV_ROWS, :] * dw_ref[w:w + 1, ls]
        return acc

    def conv_rows(r, carry):
        r0 = pl.multiple_of(r * (2 * CONV_ROWS), 2 * CONV_ROWS)
        ya = None
        accs = []
        for lt in range(n_lt):
            part = jnp.dot(ret_ref[:, lt * kc:(lt + 1) * kc], wret_ref[r, lt * kc:(lt + 1) * kc, :],
                           preferred_element_type=F32)
            ya = part if ya is None else ya + part
            accs.append((conv_piece(r0, lt), conv_piece(r0 + CONV_ROWS, lt)))
        ya_scr[r] = ya
        for lt in range(n_lt):
            ls = slice(lt * LANES, (lt + 1) * LANES)
            zc_scr[pl.ds(r0, CONV_ROWS), ls] = accs[lt][0]
            zc_scr[pl.ds(r0 + CONV_ROWS, CONV_ROWS), ls] = accs[lt][1]
        return carry

    lax.fori_loop(0, YA_BLOCKS, conv_rows, 0)
    y_a = jnp.concatenate([ya_scr[q] for q in range(YA_BLOCKS)], axis=1)
    zc = _ln_rows(zc_scr[...], row(VEC_CLN_G), row(VEC_CLN_B))
    zc = zc * _sigmoid(zc)
    y_b = jnp.dot(zc.astype(BF16), wconv_ref[...], preferred_element_type=F32) + row(VEC_BCONV)

    mixed = ga_ref[...].astype(F32) * y_a + gb_ref[...].astype(F32) * y_b
    mix = jnp.dot(mixed.astype(BF16), wmix_ref[...], preferred_element_type=F32) + row(VEC_BMIX)
    h1 = _ln_rows(ALPHA * x_ref[...] + gt_ref[0] * mix, row(VEC_LN1_G), row(VEC_LN1_B))
    h1_ref[...] = h1
    t = h1 * (1.0 + scf_ref[0]) + shf_ref[0]
    t_ref[...] = t

    logits = _dot3(t, wrh_ref[...], wrl_ref[...]) + br_ref[...]

    lane = lax.broadcasted_iota(I32, logits.shape, 1).astype(F32)
    neg = -jnp.inf
    big = float(ROUTER_LANES)
    is_grp = lane < N_GROUPS
    gl = jnp.where(is_grp, logits, neg)
    gmax = jnp.max(gl, axis=-1, keepdims=True)
    gidx = jnp.min(jnp.where(gl == gmax, lane, big), axis=-1, keepdims=True)
    gsum = jnp.sum(jnp.where(is_grp, jnp.exp(gl - gmax), 0.0), axis=-1, keepdims=True)
    grp_w = 1.0 / gsum
    lo = N_GROUPS + gidx * EXPERTS_PER_GROUP
    el = jnp.where(lane >= lo, jnp.where(lane < lo + EXPERTS_PER_GROUP, logits, neg), neg)
    m1 = jnp.max(el, axis=-1, keepdims=True)
    i1 = jnp.min(jnp.where(el == m1, lane, big), axis=-1, keepdims=True)
    el2 = jnp.where(lane == i1, neg, el)
    m2 = jnp.max(el2, axis=-1, keepdims=True)
    i2 = jnp.min(jnp.where(el2 == m2, lane, big), axis=-1, keepdims=True)
    r = jnp.exp(m2 - m1)
    w1 = grp_w / (1.0 + r)
    w2 = grp_w * r / (1.0 + r)

    @pl.when(step == 0)
    def _():
        cnt_scr[...] = jnp.zeros_like(cnt_scr)

    oh1 = lane == i1
    oh2 = lane == i2
    oh = jnp.where(oh1, 1.0, jnp.where(oh2, 1.0, 0.0))
    tri = jnp.where(lax.broadcasted_iota(I32, (tm, tm), 0) > lax.broadcasted_iota(I32, (tm, tm), 1), 1.0, 0.0)
    before = jnp.dot(tri.astype(BF16), oh.astype(BF16), preferred_element_type=F32) + cnt_scr[0:1, :]
    rank1 = jnp.sum(jnp.where(oh1, before, 0.0), axis=-1, keepdims=True)
    rank2 = jnp.sum(jnp.where(oh2, before, 0.0), axis=-1, keepdims=True)
    cnt = cnt_scr[0:1, :] + jnp.sum(oh, axis=0, keepdims=True)
    cnt_scr[...] = jnp.broadcast_to(cnt, cnt_scr.shape)
    cnt_ref[...] = jnp.broadcast_to(cnt, cnt_ref.shape)

    e1 = i1 - N_GROUPS
    e2 = i2 - N_GROUPS
    ri = jnp.where(lane == RT_E1, e1, jnp.where(lane == RT_E2, e2,
                   jnp.where(lane == RT_RANK1, rank1, jnp.where(lane == RT_RANK2, rank2, 0.0))))
    ri_ref[...] = ri.astype(I32)
    rw_ref[...] = jnp.where(lane == RT_W1, w1, jnp.where(lane == RT_W2, w2, 0.0))


def _merge(ret2, proj2, z2, x2, mod3, wret, wconv, wmix, dw, vec, wrh, wrl, br, seq, tm):
    n, d = x2.shape
    assert tm == YA_BLOCKS * 2 * CONV_ROWS
    tiles_per_seq = seq // tm
    hb = tm // CONV_HALO
    last_hb = n // CONV_HALO - 1
    const = lambda i: (0, 0)
    bat = lambda k: (lambda i: (i // tiles_per_seq, 0, k))
    return pl.pallas_call(
        functools.partial(_merge_kernel, tiles_per_seq=tiles_per_seq),
        grid=(n // tm,),
        in_specs=[pl.BlockSpec((tm, ret2.shape[1]), lambda i: (i, 0)),
                  pl.BlockSpec((tm, d), lambda i: (i, 0)),
                  pl.BlockSpec((CONV_HALO, d), lambda i: (jnp.maximum(i * hb - 1, 0), 0)),
                  pl.BlockSpec((CONV_HALO, d), lambda i: (jnp.minimum((i + 1) * hb, last_hb), 0)),
                  pl.BlockSpec((tm, d), lambda i: (i, PB_GATE)),
                  pl.BlockSpec((tm, d), lambda i: (i, PB_GATE + 1)),
                  pl.BlockSpec((tm, d), lambda i: (i, 0)),
                  pl.BlockSpec((1, 1, d), bat(2)),
                  pl.BlockSpec((1, 1, d), bat(3)),
                  pl.BlockSpec((1, 1, d), bat(4)),
                  pl.BlockSpec(wret.shape, lambda i: (0, 0, 0)),
                  pl.BlockSpec(wconv.shape, const),
                  pl.BlockSpec(wmix.shape, const),
                  pl.BlockSpec(dw.shape, const),
                  pl.BlockSpec(vec.shape, const),
                  pl.BlockSpec(wrh.shape, const),
                  pl.BlockSpec(wrl.shape, const),
                  pl.BlockSpec(br.shape, const)],
        out_specs=[pl.BlockSpec((tm, d), lambda i: (i, 0)),
                   pl.BlockSpec((tm, d), lambda i: (i, 0)),
                   pl.BlockSpec((tm, ROUTER_LANES), lambda i: (i, 0)),
                   pl.BlockSpec((tm, ROUTER_LANES), lambda i: (i, 0)),
                   pl.BlockSpec((SUBLANES, ROUTER_LANES), const)],
        out_shape=[jax.ShapeDtypeStruct((n, d), F32),
                   jax.ShapeDtypeStruct((n, d), F32),
                   jax.ShapeDtypeStruct((n, ROUTER_LANES), I32),
                   jax.ShapeDtypeStruct((n, ROUTER_LANES), F32),
                   jax.ShapeDtypeStruct((SUBLANES, ROUTER_LANES), F32)],
        scratch_shapes=[pltpu.VMEM((tm + 2 * CONV_HALO, d), F32),
                        pltpu.VMEM((tm, d), F32),
                        pltpu.VMEM((SUBLANES, ROUTER_LANES), F32),
                        pltpu.VMEM((YA_BLOCKS, tm, d // YA_BLOCKS), F32)],
        compiler_params=_params("arbitrary"),
        name="merge",
    )(ret2, z2, z2, z2, proj2, proj2, x2, mod3, mod3, mod3,
      wret, wconv, wmix, dw, vec, wrh, wrl, br)


DISPATCH_TILE = 1024


def _dispatch_kernel(pos_ref, t_ref, xs_hbm, sem):
    tm = t_ref.shape[0]
    base = pl.program_id(0) * tm

    for r in range(tm):
        for k in range(EXPERT_TOP_K):
            dst = pos_ref[(base + r) * EXPERT_TOP_K + k]
            pltpu.make_async_copy(t_ref.at[pl.ds(r, 1)], xs_hbm.at[pl.ds(dst, 1)], sem).start(priority=k)
    for k in range(EXPERT_TOP_K):
        pltpu.make_async_copy(t_ref, xs_hbm.at[pl.ds(0, tm)], sem).wait()


def _dispatch(pos_flat, t2):
    n, d = t2.shape
    tm = min(DISPATCH_TILE, n)
    return pl.pallas_call(
        _dispatch_kernel,
        grid=(n // tm,),
        in_specs=[pl.BlockSpec(memory_space=pltpu.SMEM),
                  pl.BlockSpec((tm, d), lambda i: (i, 0))],
        out_specs=pl.BlockSpec(memory_space=pl.ANY),
        out_shape=jax.ShapeDtypeStruct((n * EXPERT_TOP_K, d), F32),
        scratch_shapes=[pltpu.SemaphoreType.DMA],
        compiler_params=_params("arbitrary"),
        name="dispatch",
    )(pos_flat, t2)


EXPERT_TILE = 512


def _experts_kernel(ie_ref, it_ref, lo_ref, hi_ref, nv_ref, xs_ref, wg_ref, wu_ref, wd_ref, ys_ref,
                    wg_bf, wu_bf, wd_bf):
    k = pl.program_id(0)
    prev = jnp.maximum(k - 1, 0)
    new_expert = (k == 0) | (ie_ref[k] != ie_ref[prev])
    new_tile = (k == 0) | (it_ref[k] != it_ref[prev])

    @pl.when(new_expert)
    def _():
        wg_bf[...] = wg_ref[0, 0].astype(BF16)
        wu_bf[...] = wu_ref[0, 0].astype(BF16)
        wd_bf[...] = wd_ref[0, 0].astype(BF16)

    @pl.when(k < nv_ref[0])
    def _():
        x = xs_ref[...].astype(BF16)
        g = jnp.dot(x, wg_bf[...], preferred_element_type=F32)
        u = jnp.dot(x, wu_bf[...], preferred_element_type=F32)
        hid = (g * _sigmoid(g) * u).astype(BF16)
        y = jnp.dot(hid, wd_bf[...], preferred_element_type=F32)
        rows = lax.broadcasted_iota(I32, (xs_ref.shape[0], 1), 0)
        mine = (rows >= lo_ref[k]) & (rows < hi_ref[k])

        @pl.when(new_tile)
        def _():
            ys_ref[...] = jnp.where(mine, y, 0.0)

        @pl.when(jnp.logical_not(new_tile))
        def _():
            ys_ref[...] = jnp.where(mine, y, ys_ref[...])


def _experts(meta, xs, w_gate, w_up, w_down):
    p, d = xs.shape
    ff = w_gate.shape[-1]
    n_items = meta[0].shape[0]
    grid_spec = pltpu.PrefetchScalarGridSpec(
        num_scalar_prefetch=5,
        grid=(n_items,),
        in_specs=[pl.BlockSpec((EXPERT_TILE, d), lambda k, ie, it, lo, hi, nv: (it[k], 0)),
                  pl.BlockSpec((1, 1, d, ff), lambda k, ie, it, lo, hi, nv: (0, ie[k], 0, 0)),
                  pl.BlockSpec((1, 1, d, ff), lambda k, ie, it, lo, hi, nv: (0, ie[k], 0, 0)),
                  pl.BlockSpec((1, 1, ff, d), lambda k, ie, it, lo, hi, nv: (0, ie[k], 0, 0))],
        out_specs=pl.BlockSpec((EXPERT_TILE, d), lambda k, ie, it, lo, hi, nv: (it[k], 0)),
        scratch_shapes=[pltpu.VMEM((d, ff), BF16), pltpu.VMEM((d, ff), BF16), pltpu.VMEM((ff, d), BF16)])
    return pl.pallas_call(
        _experts_kernel,
        grid_spec=grid_spec,
        out_shape=jax.ShapeDtypeStruct((p, d), F32),
        compiler_params=_params("arbitrary"),
        name="experts",
    )(*meta, xs, w_gate, w_up, w_down)


def _expert_work_items(counts, p):
    n_tiles = p // EXPERT_TILE
    n_items = n_tiles + N_EXPERTS - 1
    offs = jnp.concatenate([jnp.zeros((1,), I32), jnp.cumsum(counts)])
    first = offs[:-1] // EXPERT_TILE
    last = (offs[1:] - 1) // EXPERT_TILE
    per = jnp.where(counts > 0, last - first + 1, 0)
    ends = jnp.cumsum(per)
    starts = ends - per
    n_valid = ends[-1]
    k = jnp.arange(n_items, dtype=I32)
    kk = jnp.minimum(k, n_valid - 1)
    ie = jnp.sum((ends[None, :] <= kk[:, None]).astype(I32), axis=1)
    it = first[ie] + (kk - starts[ie])
    lo = jnp.maximum(offs[ie], it * EXPERT_TILE) - it * EXPERT_TILE
    hi = jnp.minimum(offs[ie + 1], (it + 1) * EXPERT_TILE) - it * EXPERT_TILE
    valid = k < n_valid
    lo = jnp.where(valid, lo, 0)
    hi = jnp.where(valid, hi, 0)
    return offs, (ie.astype(I32), it.astype(I32), lo.astype(I32), hi.astype(I32), n_valid.reshape(1).astype(I32))


COMBINE_HALF = 512


def _combine_kernel(pos_ref, ys_hbm, rw_ref, h1_ref, gt_ref, ln_ref, out_ref, buf, sem):
    i = pl.program_id(0)
    n_steps = pl.num_programs(0)
    hm = COMBINE_HALF
    base = i * (2 * hm)

    def slot_copy(s, k):
        return pltpu.make_async_copy(ys_hbm.at[pl.ds(0, hm)], buf.at[s, k], sem.at[s])

    def gather(first_tok, s):
        for r in range(hm):
            for k in range(EXPERT_TOP_K):
                src = pos_ref[(first_tok + r) * EXPERT_TOP_K + k]
                pltpu.make_async_copy(ys_hbm.at[pl.ds(src, 1)], buf.at[s, k, pl.ds(r, 1)],
                                      sem.at[s]).start(priority=k)

    def finish(s):
        for k in range(EXPERT_TOP_K):
            slot_copy(s, k).wait()
        rows = slice(s * hm, (s + 1) * hm)
        rw = rw_ref[rows, :]
        y = rw[:, RT_W1:RT_W1 + 1] * buf[s, 0] + rw[:, RT_W2:RT_W2 + 1] * buf[s, 1]
        out_ref[rows, :] = _ln_rows(ALPHA * h1_ref[rows, :] + gt_ref[0] * y, ln_ref[0:1, :], ln_ref[1:2, :])

    @pl.when(i == 0)
    def _():
        gather(0, 0)

    gather(base + hm, 1)
    finish(0)

    @pl.when(i + 1 < n_steps)
    def _():
        gather(base + 2 * hm, 0)

    finish(1)


def _combine(pos_flat, ys, rw, h1, mod3, ln2, seq):
    n, d = h1.shape
    tm = 2 * COMBINE_HALF
    tiles_per_seq = seq // tm
    return pl.pallas_call(
        _combine_kernel,
        grid=(n // tm,),
        in_specs=[pl.BlockSpec(memory_space=pltpu.SMEM),
                  pl.BlockSpec(memory_space=pl.ANY),
                  pl.BlockSpec((tm, ROUTER_LANES), lambda i: (i, 0)),
                  pl.BlockSpec((tm, d), lambda i: (i, 0)),
                  pl.BlockSpec((1, 1, d), lambda i: (i // tiles_per_seq, 0, 5)),
                  pl.BlockSpec(ln2.shape, lambda i: (0, 0))],
        out_specs=pl.BlockSpec((tm, d), lambda i: (i, 0)),
        out_shape=jax.ShapeDtypeStruct((n, d), F32),
        scratch_shapes=[pltpu.VMEM((2, EXPERT_TOP_K, COMBINE_HALF, d), F32),
                        pltpu.SemaphoreType.DMA((2,))],
        compiler_params=_params("arbitrary"),
        name="combine",
    )(pos_flat, ys, rw, h1, mod3, ln2)


def _rope_tables(seq):
    pos = np.arange(seq)
    r = (pos // GRID_W).astype(np.float32)
    col = (pos % GRID_W).astype(np.float32)
    n_freq = RET_QK_DIM // 4
    inv = (np.float32(ROPE_BASE) ** (-np.arange(n_freq, dtype=np.float32) / np.float32(n_freq))).astype(np.float32)
    ang = np.concatenate([r[:, None] * inv, col[:, None] * inv], axis=-1)
    ang = np.concatenate([ang, ang], axis=-1).astype(np.float32)
    sign = np.concatenate([-np.ones((RET_QK_DIM // 2,), np.float32), np.ones((RET_QK_DIM // 2,), np.float32)])
    return jnp.asarray(np.cos(ang), F32), jnp.asarray(np.sin(ang) * sign, F32)


def kernel(x, c, ctx, c_ctx, w_ada, b_ada, w_in, b_in, ret_decay_fwd, ret_decay_bwd, w_ret_out, conv_dw, conv_dw_b, conv_ln_g, conv_ln_b, w_conv_out, b_conv_out, w_mix_out, b_mix_out, ln1_g, ln1_b, w_router_grp, b_router_grp, w_router_exp, b_router_exp, w_exp_gate, w_exp_up, w_exp_down, ln2_g, ln2_b):
    b_, seq, d = x.shape
    n = b_ * seq
    assert w_ada.shape[0] == DEPTH
    mod_rows = SUBLANES
    assert b_ + 1 <= mod_rows

    cs = jnp.concatenate([c, c_ctx[None, :], jnp.zeros((mod_rows - b_ - 1, d), F32)], axis=0)
    mod = _ada(cs, w_ada[0], b_ada[0][None, :])
    mod3 = mod.reshape(mod_rows, 1, 6 * d)

    w_in3 = w_in[0].astype(BF16)
    b_in3 = b_in[0][None, :]
    dec_f = ret_decay_fwd[0].reshape(RET_HEADS, 1, 1)
    dec_b = ret_decay_bwd[0].reshape(RET_HEADS, 1, 1)

    s0f, s0b = _ctx_states(ctx, mod3, w_in3, b_in3, dec_f, dec_b, b_)

    cos, sin = _rope_tables(seq)
    x2 = x.reshape(n, d)
    proj, z = _proj(x2, mod3, w_in3, b_in3, cos, sin, seq, tm=min(1024, seq))

    ret = _retention(proj.reshape(b_, seq, PROJ_BLOCKS * d), s0f, s0b, dec_f, dec_b, d)

    vec = jnp.concatenate([conv_dw_b, conv_ln_g, conv_ln_b, b_conv_out, b_mix_out, ln1_g, ln1_b,
                           jnp.zeros((1, d), F32)], axis=0)
    tap_rows = -(-CONV_WIDTH // SUBLANES) * SUBLANES
    dw = jnp.concatenate([conv_dw[0], jnp.zeros((tap_rows - CONV_WIDTH, d), F32)], axis=0)
    pad = ROUTER_LANES - N_GROUPS - N_EXPERTS
    wr = jnp.concatenate([w_router_grp[0], w_router_exp[0], jnp.zeros((d, pad), F32)], axis=1)
    wr_hi = wr.astype(BF16)
    wr_lo = (wr - wr_hi.astype(F32)).astype(BF16)
    br = jnp.concatenate([b_router_grp[0], b_router_exp[0], jnp.zeros((pad,), F32)])[None, :]
    h1, t, route_i, route_w, cnt = _merge(
        ret.reshape(n, RET_HEADS * RET_V_DIM), proj, z, x2, mod3,
        w_ret_out[0].astype(BF16).reshape(-1, YA_BLOCKS, d // YA_BLOCKS).transpose(1, 0, 2),
        w_conv_out[0].astype(BF16), w_mix_out[0].astype(BF16),
        dw, vec, wr_hi, wr_lo, br, seq, tm=min(512, seq))

    counts = cnt[0, N_GROUPS:N_GROUPS + N_EXPERTS].astype(I32)
    offs, meta = _expert_work_items(counts, n * EXPERT_TOP_K)
    eid = route_i[:, RT_E1:RT_E2 + 1]
    seg_start = jnp.sum(jnp.where(eid[:, :, None] == jnp.arange(N_EXPERTS, dtype=I32), offs[:N_EXPERTS], 0), axis=-1)
    pos = seg_start + route_i[:, RT_RANK1:RT_RANK2 + 1]
    pos_flat = pos.reshape(n * EXPERT_TOP_K)

    xs = _dispatch(pos_flat, t)
    ys = _experts(meta, xs, w_exp_gate, w_exp_up, w_exp_down)
    ln2 = jnp.concatenate([ln2_g, ln2_b, jnp.zeros((SUBLANES - 2, d), F32)], axis=0)
    out = _combine(pos_flat, ys, route_w, h1, mod3, ln2, seq)
    return out.reshape(b_, seq, d)
```

```python
import functools

import jax
import jax.numpy as jnp
import numpy as np
from jax import lax
from jax.experimental import pallas as pl
from jax.experimental.pallas import tpu as pltpu

GRID_W = 64
RET_HEADS = 8
RET_QK_DIM = 128
RET_V_DIM = 256
RET_CHUNK = 256
ROPE_BASE = 10000.0
CONV_WIDTH = 31
CONV_HALO = 16
SUBLANES = 8
N_GROUPS = 4
EXPERTS_PER_GROUP = 8
N_EXPERTS = N_GROUPS * EXPERTS_PER_GROUP
EXPERT_TOP_K = 2
LN_EPS = 1e-5
DEPTH = 1
ALPHA = (2.0 * DEPTH) ** 0.25
ROUTER_LANES = 128

V7X_VMEM_LIMIT = 56 * 1024 * 1024

F32 = jnp.float32
BF16 = jnp.bfloat16
I32 = jnp.int32


def _params(*sem):
    return pltpu.CompilerParams(dimension_semantics=sem, vmem_limit_bytes=V7X_VMEM_LIMIT)


def _sigmoid(v):
    return 0.5 * jnp.tanh(0.5 * v) + 0.5


def _split_bf16(v):
    hi = v.astype(BF16)
    return hi, (v - hi.astype(F32)).astype(BF16)


def _dot3(a, b_hi, b_lo):
    a_hi, a_lo = _split_bf16(a)
    return (jnp.dot(a_hi, b_hi, preferred_element_type=F32)
            + jnp.dot(a_lo, b_hi, preferred_element_type=F32)
            + jnp.dot(a_hi, b_lo, preferred_element_type=F32))


def _ln_rows(v, g, b):
    mu = jnp.mean(v, axis=-1, keepdims=True)
    d = v - mu
    var = jnp.mean(d * d, axis=-1, keepdims=True)
    return d * lax.rsqrt(var + LN_EPS) * g + b


ADA_K_TILE = 256


def _ada_kernel(cs_ref, w_ref, b_ref, out_ref):
    k = pl.program_id(0)
    s = cs_ref[...]
    s = s * _sigmoid(s)
    w_hi, w_lo = _split_bf16(w_ref[...])
    part = _dot3(s, w_hi, w_lo)

    @pl.when(k == 0)
    def _():
        out_ref[...] = part + b_ref[...]

    @pl.when(k > 0)
    def _():
        out_ref[...] += part


def _ada(cs, w_ada, b_ada):
    rows, d = cs.shape
    cols = w_ada.shape[1]
    tk = ADA_K_TILE
    return pl.pallas_call(
        _ada_kernel,
        grid=(d // tk,),
        in_specs=[pl.BlockSpec((rows, tk), lambda k: (0, k)),
                  pl.BlockSpec((tk, cols), lambda k: (k, 0)),
                  pl.BlockSpec((1, cols), lambda k: (0, 0))],
        out_specs=pl.BlockSpec((rows, cols), lambda k: (0, 0)),
        out_shape=jax.ShapeDtypeStruct((rows, cols), F32),
        compiler_params=_params("arbitrary"),
        name="ada",
    )(cs, w_ada, b_ada)


def _log_sigmoid(v):
    return jnp.minimum(v, 0.0) - jnp.log(1.0 + jnp.exp(-jnp.abs(v)))


def _ctx_kernel(ctx_ref, sh_ref, sc_ref, wk_ref, wv0_ref, wv1_ref, bk_ref, bv0_ref, bv1_ref, df_ref, db_ref,
                sf_ref, sb_ref):
    lc = ctx_ref.shape[1]
    u = (ctx_ref[0] * (1.0 + sc_ref[0]) + sh_ref[0]).astype(BF16)
    k = (jnp.dot(u, wk_ref[...], preferred_element_type=F32) + bk_ref[...]) * RET_QK_DIM ** -0.5
    v = jnp.concatenate(
        [(jnp.dot(u, w[...], preferred_element_type=F32) + b[...]).astype(BF16)
         for w, b in ((wv0_ref, bv0_ref), (wv1_ref, bv1_ref))], axis=1)
    pos = lax.broadcasted_iota(I32, (lc, 1), 0).astype(F32)
    dn = (((0,), (0,)), ((), ()))
    for h in range(RET_HEADS):
        lgf = _log_sigmoid(df_ref[h])
        lgb = _log_sigmoid(db_ref[h])
        kh = k[:, h * RET_QK_DIM:(h + 1) * RET_QK_DIM]
        vh = v[:, h * RET_V_DIM:(h + 1) * RET_V_DIM]
        kf = (kh * jnp.exp((lc - 1.0 - pos) * lgf)).astype(BF16)
        kb = (kh * jnp.exp(pos * lgb)).astype(BF16)
        sf_ref[0, h] = lax.dot_general(kf, vh, dn, preferred_element_type=F32)
        sb_ref[0, h] = lax.dot_general(kb, vh, dn, preferred_element_type=F32)


def _ctx_states(ctx, mod3, w_in3, b_in3, dec_f, dec_b, ctx_row):
    b_, lc, d = ctx.shape
    assert RET_HEADS * RET_QK_DIM == d and RET_HEADS * RET_V_DIM == 2 * d
    st = jax.ShapeDtypeStruct((b_, RET_HEADS, RET_QK_DIM, RET_V_DIM), F32)
    st_spec = pl.BlockSpec((1, RET_HEADS, RET_QK_DIM, RET_V_DIM), lambda b: (b, 0, 0, 0))
    wblk = lambda blk: pl.BlockSpec((d, d), lambda b: (0, blk))
    bblk = lambda blk: pl.BlockSpec((1, d), lambda b: (0, blk))
    dec_spec = pl.BlockSpec((RET_HEADS, 1, 1), lambda b: (0, 0, 0))
    return pl.pallas_call(
        _ctx_kernel,
        grid=(b_,),
        in_specs=[pl.BlockSpec((1, lc, d), lambda b: (b, 0, 0)),
                  pl.BlockSpec((1, 1, d), lambda b: (ctx_row, 0, 0)),
                  pl.BlockSpec((1, 1, d), lambda b: (ctx_row, 0, 1)),
                  wblk(PB_K), wblk(PB_V), wblk(PB_V + 1),
                  bblk(PB_K), bblk(PB_V), bblk(PB_V + 1),
                  dec_spec, dec_spec],
        out_specs=[st_spec, st_spec],
        out_shape=[st, st],
        compiler_params=_params("arbitrary"),
        name="ctx",
    )(ctx, mod3, mod3, w_in3, w_in3, w_in3, b_in3, b_in3, b_in3, dec_f, dec_b)


PROJ_BLOCKS = 8
PB_Q, PB_K, PB_V, PB_G, PB_GATE = 0, 1, 2, 4, 6
WB_GLU, WB_GATE = 6, 8
PROJ_STEPS = PROJ_BLOCKS // 2 + 1


def _proj_kernel(x_ref, sh_ref, sc_ref, w_ref, b_ref, cos_ref, sin_ref, out_ref, z_ref, u_scr):
    j = pl.program_id(1)
    d = x_ref.shape[1]
    halves = (slice(0, d), slice(d, 2 * d))

    @pl.when(j == 0)
    def _():
        u_scr[...] = (x_ref[...] * (1.0 + sc_ref[0]) + sh_ref[0]).astype(BF16)

    def mm(blk):
        cols = slice(blk * d, (blk + 1) * d)
        return jnp.dot(u_scr[...], w_ref[:, cols], preferred_element_type=F32) + b_ref[:, cols]

    def rope(acc, half):
        cos = cos_ref[...]
        sin = sin_ref[...]
        for h in range(acc.shape[1] // RET_QK_DIM):
            seg = acc[:, h * RET_QK_DIM:(h + 1) * RET_QK_DIM]
            sl = slice(half.start + h * RET_QK_DIM, half.start + (h + 1) * RET_QK_DIM)
            out_ref[:, sl] = (seg * cos + pltpu.roll(seg, RET_QK_DIM // 2, 1) * sin).astype(BF16)

    def step(jj):
        if jj == 0:
            rope(mm(PB_Q), halves[0])
            rope(mm(PB_K) * RET_QK_DIM ** -0.5, halves[1])
        elif jj == 1:
            for hh, half in enumerate(halves):
                out_ref[:, half] = mm(PB_V + hh).astype(BF16)
        elif jj == 2:
            for hh, half in enumerate(halves):
                acc = mm(PB_G + hh)
                out_ref[:, half] = (acc * _sigmoid(acc)).astype(BF16)
        elif jj == 3:
            for hh, half in enumerate(halves):
                out_ref[:, half] = _sigmoid(mm(WB_GATE + hh)).astype(BF16)
        else:
            z_ref[...] = (mm(WB_GLU) * _sigmoid(mm(WB_GLU + 1))).astype(BF16)

    for jj in range(PROJ_STEPS):
        pl.when(j == jj)(functools.partial(step, jj))


def _proj(x2, mod3, w_in3, b_in3, cos, sin, seq, tm):
    n, d = x2.shape
    tiles_per_seq = seq // tm
    last_pair = PROJ_BLOCKS // 2 - 1
    return pl.pallas_call(
        _proj_kernel,
        grid=(n // tm, PROJ_STEPS),
        in_specs=[pl.BlockSpec((tm, d), lambda i, j: (i, 0)),
                  pl.BlockSpec((1, 1, d), lambda i, j: (i // tiles_per_seq, 0, 0)),
                  pl.BlockSpec((1, 1, d), lambda i, j: (i // tiles_per_seq, 0, 1)),
                  pl.BlockSpec(memory_space=pltpu.VMEM),
                  pl.BlockSpec(memory_space=pltpu.VMEM),
                  pl.BlockSpec((tm, RET_QK_DIM), lambda i, j: (i % tiles_per_seq, 0)),
                  pl.BlockSpec((tm, RET_QK_DIM), lambda i, j: (i % tiles_per_seq, 0))],
        out_specs=[pl.BlockSpec((tm, 2 * d), lambda i, j: (i, jnp.minimum(j, last_pair))),
                   pl.BlockSpec((tm, d), lambda i, j: (i, 0))],
        out_shape=[jax.ShapeDtypeStruct((n, PROJ_BLOCKS * d), BF16),
                   jax.ShapeDtypeStruct((n, d), BF16)],
        scratch_shapes=[pltpu.VMEM((tm, d), BF16)],
        compiler_params=_params("arbitrary", "arbitrary"),
        name="proj",
    )(x2, mod3, mod3, w_in3, b_in3, cos, sin)


def _ret_kernel(q_ref, k_ref, v_ref, g_ref, s0f_ref, s0b_ref, df_ref, db_ref, out_ref,
                sf_all, sb_all, p_scr, o_scr):
    c_ = RET_CHUNK
    n_chunks = q_ref.shape[1] // c_
    dn_t = (((0,), (0,)), ((), ()))
    lgf = _log_sigmoid(df_ref[0])
    lgb = _log_sigmoid(db_ref[0])
    ri = lax.broadcasted_iota(I32, (c_, c_), 0)
    ci = lax.broadcasted_iota(I32, (c_, c_), 1)
    diff = (ri - ci).astype(F32)
    mask = jnp.where(diff > 0, jnp.exp(diff * lgf), jnp.where(diff < 0, jnp.exp(-diff * lgb), 2.0))
    pos = lax.broadcasted_iota(I32, (c_, 1), 0).astype(F32)
    qdec_f = jnp.exp((pos + 1.0) * lgf)
    qdec_b = jnp.exp((c_ - pos) * lgb)
    kdec_f = jnp.exp((c_ - 1.0 - pos) * lgf)
    kdec_b = jnp.exp(pos * lgb)
    cdec_f = jnp.exp(c_ * lgf)
    cdec_b = jnp.exp(c_ * lgb)

    def chunk(c):
        return slice(c * c_, (c + 1) * c_)

    for c in range(n_chunks):
        k = k_ref[0, chunk(c), :].astype(F32)
        v = v_ref[0, chunk(c), :]
        sf_all[c] = lax.dot_general((k * kdec_f).astype(BF16), v, dn_t, preferred_element_type=F32)
        sb_all[c] = lax.dot_general((k * kdec_b).astype(BF16), v, dn_t, preferred_element_type=F32)

    sf = s0f_ref[0, 0]
    sb = s0b_ref[0, 0]
    for t in range(n_chunks):
        kv = sf_all[t]
        sf_all[t] = sf
        sf = sf * cdec_f + kv
        c = n_chunks - 1 - t
        kv = sb_all[c]
        sb_all[c] = sb
        sb = sb * cdec_b + kv

    for c in range(n_chunks):
        s = lax.dot_general(q_ref[0, chunk(c), :], k_ref[0, chunk(c), :], (((1,), (1,)), ((), ())),
                            preferred_element_type=F32)
        p_scr[c] = (s * mask).astype(BF16)

    for c in range(n_chunks):
        q = q_ref[0, chunk(c), :]
        o = jnp.dot(p_scr[c], v_ref[0, chunk(c), :], preferred_element_type=F32)
        o = o + qdec_f * jnp.dot(q, sf_all[c].astype(BF16), preferred_element_type=F32)
        o = o + qdec_b * jnp.dot(q, sb_all[c].astype(BF16), preferred_element_type=F32)
        o_scr[chunk(c), :] = o

    for c in range(n_chunks):
        o = o_scr[chunk(c), :]
        mu = jnp.mean(o, axis=-1, keepdims=True)
        d = o - mu
        var = jnp.mean(d * d, axis=-1, keepdims=True)
        on = d * lax.rsqrt(var + LN_EPS)
        out_ref[0, chunk(c), :] = (on * g_ref[0, chunk(c), :].astype(F32)).astype(BF16)


def _retention(proj3, s0f, s0b, dec_f, dec_b, d):
    b_, seq, _ = proj3.shape
    qb, kb = PB_Q * d // RET_QK_DIM, PB_K * d // RET_QK_DIM
    vb, gb = PB_V * d // RET_V_DIM, PB_G * d // RET_V_DIM
    n_chunks = seq // RET_CHUNK
    st_spec = pl.BlockSpec((1, 1, RET_QK_DIM, RET_V_DIM), lambda b, h: (b, h, 0, 0))
    st_all = pltpu.VMEM((n_chunks, RET_QK_DIM, RET_V_DIM), F32)
    return pl.pallas_call(
        _ret_kernel,
        grid=(b_, RET_HEADS),
        in_specs=[pl.BlockSpec((1, seq, RET_QK_DIM), lambda b, h: (b, 0, qb + h)),
                  pl.BlockSpec((1, seq, RET_QK_DIM), lambda b, h: (b, 0, kb + h)),
                  pl.BlockSpec((1, seq, RET_V_DIM), lambda b, h: (b, 0, vb + h)),
                  pl.BlockSpec((1, seq, RET_V_DIM), lambda b, h: (b, 0, gb + h)),
                  st_spec, st_spec,
                  pl.BlockSpec((1, 1, 1), lambda b, h: (h, 0, 0)),
                  pl.BlockSpec((1, 1, 1), lambda b, h: (h, 0, 0))],
        out_specs=pl.BlockSpec((1, seq, RET_V_DIM), lambda b, h: (b, 0, h)),
        out_shape=jax.ShapeDtypeStruct((b_, seq, RET_HEADS * RET_V_DIM), BF16),
        scratch_shapes=[st_all, st_all,
                        pltpu.VMEM((n_chunks, RET_CHUNK, RET_CHUNK), BF16),
                        pltpu.VMEM((seq, RET_V_DIM), F32)],
        compiler_params=_params("arbitrary", "arbitrary"),
        name="ret",
    )(proj3, proj3, proj3, proj3, s0f, s0b, dec_f, dec_b)


VEC_CONV_B, VEC_CLN_G, VEC_CLN_B, VEC_BCONV, VEC_BMIX, VEC_LN1_G, VEC_LN1_B = range(7)
CONV_ROWS = 64
YA_BLOCKS = 4
LANES = 128
RT_E1, RT_E2, RT_RANK1, RT_RANK2 = 0, 1, 2, 3
RT_W1, RT_W2 = 0, 1


def _merge_kernel(ret_ref, z_ref, zp_ref, zn_ref, ga_ref, gb_ref, x_ref, gt_ref, shf_ref, scf_ref,
                  wret_ref, wconv_ref, wmix_ref, dw_ref, vec_ref, wrh_ref, wrl_ref, br_ref,
                  h1_ref, t_ref, ri_ref, rw_ref, cnt_ref, zext, zc_scr, cnt_scr, ya_scr, *, tiles_per_seq):
    tm, d = x_ref.shape
    step = pl.program_id(0)
    ti = step % tiles_per_seq
    vec = vec_ref[...]

    def row(r):
        return vec[r:r + 1, :]


    has_prev = (ti > 0).astype(F32)
    has_next = (ti < tiles_per_seq - 1).astype(F32)
    zext[0:CONV_HALO, :] = zp_ref[...].astype(F32) * has_prev
    zext[CONV_HALO:CONV_HALO + tm, :] = z_ref[...].astype(F32)
    zext[CONV_HALO + tm:, :] = zn_ref[...].astype(F32) * has_next
    base = CONV_HALO - CONV_WIDTH // 2
    win_rows = CONV_ROWS + 2 * CONV_HALO

    n_lt = d // LANES
    kc = ret_ref.shape[1] // n_lt

    def conv_piece(r0, lt):
        ls = slice(lt * LANES, (lt + 1) * LANES)
        win = zext[pl.ds(r0, win_rows), ls]
        acc = jnp.zeros((CONV_ROWS, LANES), F32) + vec_ref[VEC_CONV_B:VEC_CONV_B + 1, ls]
        for s in range(SUBLANES):
            taps = [w for w in range(CONV_WIDTH) if (base + w) % SUBLANES == s]
            if not taps:
                continue
            sh = win if s == 0 else pltpu.roll(win, win_rows - s, 0)
            for w in taps:
                a = (base + w) - s
                acc = acc + sh[a:a + CONV_ROWS, :] * dw_ref[w:w + 1, ls]
        return acc

    def conv_rows(r, carry):
        r0 = pl.multiple_of(r * (2 * CONV_ROWS), 2 * CONV_ROWS)
        ya = None
        accs = []
        for lt in range(n_lt):
            part = jnp.dot(ret_ref[:, lt * kc:(lt + 1) * kc], wret_ref[r, lt * kc:(lt + 1) * kc, :],
                           preferred_element_type=F32)
            ya = part if ya is None else ya + part
            accs.append((conv_piece(r0, lt), conv_piece(r0 + CONV_ROWS, lt)))
        ya_scr[r] = ya
        for lt in range(n_lt):
            ls = slice(lt * LANES, (lt + 1) * LANES)
            zc_scr[pl.ds(r0, CONV_ROWS), ls] = accs[lt][0]
            zc_scr[pl.ds(r0 + CONV_ROWS, CONV_ROWS), ls] = accs[lt][1]
        return carry

    lax.fori_loop(0, YA_BLOCKS, conv_rows, 0)
    y_a = jnp.concatenate([ya_scr[q] for q in range(YA_BLOCKS)], axis=1)
    zc = _ln_rows(zc_scr[...], row(VEC_CLN_G), row(VEC_CLN_B))
    zc = zc * _sigmoid(zc)
    y_b = jnp.dot(zc.astype(BF16), wconv_ref[...], preferred_element_type=F32) + row(VEC_BCONV)

    mixed = ga_ref[...].astype(F32) * y_a + gb_ref[...].astype(F32) * y_b
    mix = jnp.dot(mixed.astype(BF16), wmix_ref[...], preferred_element_type=F32) + row(VEC_BMIX)
    h1 = _ln_rows(ALPHA * x_ref[...] + gt_ref[0] * mix, row(VEC_LN1_G), row(VEC_LN1_B))
    h1_ref[...] = h1
    t = h1 * (1.0 + scf_ref[0]) + shf_ref[0]
    t_ref[...] = t

    logits = _dot3(t, wrh_ref[...], wrl_ref[...]) + br_ref[...]

    lane = lax.broadcasted_iota(I32, logits.shape, 1).astype(F32)
    neg = -jnp.inf
    big = float(ROUTER_LANES)
    is_grp = lane < N_GROUPS
    gl = jnp.where(is_grp, logits, neg)
    gmax = jnp.max(gl, axis=-1, keepdims=True)
    gidx = jnp.min(jnp.where(gl == gmax, lane, big), axis=-1, keepdims=True)
    gsum = jnp.sum(jnp.where(is_grp, jnp.exp(gl - gmax), 0.0), axis=-1, keepdims=True)
    grp_w = 1.0 / gsum
    lo = N_GROUPS + gidx * EXPERTS_PER_GROUP
    el = jnp.where(lane >= lo, jnp.where(lane < lo + EXPERTS_PER_GROUP, logits, neg), neg)
    m1 = jnp.max(el, axis=-1, keepdims=True)
    i1 = jnp.min(jnp.where(el == m1, lane, big), axis=-1, keepdims=True)
    el2 = jnp.where(lane == i1, neg, el)
    m2 = jnp.max(el2, axis=-1, keepdims=True)
    i2 = jnp.min(jnp.where(el2 == m2, lane, big), axis=-1, keepdims=True)
    r = jnp.exp(m2 - m1)
    w1 = grp_w / (1.0 + r)
    w2 = grp_w * r / (1.0 + r)

    @pl.when(step == 0)
    def _():
        cnt_scr[...] = jnp.zeros_like(cnt_scr)

    oh1 = lane == i1
    oh2 = lane == i2
    oh = jnp.where(oh1, 1.0, jnp.where(oh2, 1.0, 0.0))
    tri = jnp.where(lax.broadcasted_iota(I32, (tm, tm), 0) > lax.broadcasted_iota(I32, (tm, tm), 1), 1.0, 0.0)
    before = jnp.dot(tri.astype(BF16), oh.astype(BF16), preferred_element_type=F32) + cnt_scr[0:1, :]
    rank1 = jnp.sum(jnp.where(oh1, before, 0.0), axis=-1, keepdims=True)
    rank2 = jnp.sum(jnp.where(oh2, before, 0.0), axis=-1, keepdims=True)
    cnt = cnt_scr[0:1, :] + jnp.sum(oh, axis=0, keepdims=True)
    cnt_scr[...] = jnp.broadcast_to(cnt, cnt_scr.shape)
    cnt_ref[...] = jnp.broadcast_to(cnt, cnt_ref.shape)

    e1 = i1 - N_GROUPS
    e2 = i2 - N_GROUPS
    ri = jnp.where(lane == RT_E1, e1, jnp.where(lane == RT_E2, e2,
                   jnp.where(lane == RT_RANK1, rank1, jnp.where(lane == RT_RANK2, rank2, 0.0))))
    ri_ref[...] = ri.astype(I32)
    rw_ref[...] = jnp.where(lane == RT_W1, w1, jnp.where(lane == RT_W2, w2, 0.0))


def _merge(ret2, proj2, z2, x2, mod3, wret, wconv, wmix, dw, vec, wrh, wrl, br, seq, tm):
    n, d = x2.shape
    assert tm == YA_BLOCKS * 2 * CONV_ROWS
    tiles_per_seq = seq // tm
    hb = tm // CONV_HALO
    last_hb = n // CONV_HALO - 1
    const = lambda i: (0, 0)
    bat = lambda k: (lambda i: (i // tiles_per_seq, 0, k))
    return pl.pallas_call(
        functools.partial(_merge_kernel, tiles_per_seq=tiles_per_seq),
        grid=(n // tm,),
        in_specs=[pl.BlockSpec((tm, ret2.shape[1]), lambda i: (i, 0)),
                  pl.BlockSpec((tm, d), lambda i: (i, 0)),
                  pl.BlockSpec((CONV_HALO, d), lambda i: (jnp.maximum(i * hb - 1, 0), 0)),
                  pl.BlockSpec((CONV_HALO, d), lambda i: (jnp.minimum((i + 1) * hb, last_hb), 0)),
                  pl.BlockSpec((tm, d), lambda i: (i, PB_GATE)),
                  pl.BlockSpec((tm, d), lambda i: (i, PB_GATE + 1)),
                  pl.BlockSpec((tm, d), lambda i: (i, 0)),
                  pl.BlockSpec((1, 1, d), bat(2)),
                  pl.BlockSpec((1, 1, d), bat(3)),
                  pl.BlockSpec((1, 1, d), bat(4)),
                  pl.BlockSpec(wret.shape, lambda i: (0, 0, 0)),
                  pl.BlockSpec(wconv.shape, const),
                  pl.BlockSpec(wmix.shape, const),
                  pl.BlockSpec(dw.shape, const),
                  pl.BlockSpec(vec.shape, const),
                  pl.BlockSpec(wrh.shape, const),
                  pl.BlockSpec(wrl.shape, const),
                  pl.BlockSpec(br.shape, const)],
        out_specs=[pl.BlockSpec((tm, d), lambda i: (i, 0)),
                   pl.BlockSpec((tm, d), lambda i: (i, 0)),
                   pl.BlockSpec((tm, ROUTER_LANES), lambda i: (i, 0)),
                   pl.BlockSpec((tm, ROUTER_LANES), lambda i: (i, 0)),
                   pl.BlockSpec((SUBLANES, ROUTER_LANES), const)],
        out_shape=[jax.ShapeDtypeStruct((n, d), F32),
                   jax.ShapeDtypeStruct((n, d), F32),
                   jax.ShapeDtypeStruct((n, ROUTER_LANES), I32),
                   jax.ShapeDtypeStruct((n, ROUTER_LANES), F32),
                   jax.ShapeDtypeStruct((SUBLANES, ROUTER_LANES), F32)],
        scratch_shapes=[pltpu.VMEM((tm + 2 * CONV_HALO, d), F32),
                        pltpu.VMEM((tm, d), F32),
                        pltpu.VMEM((SUBLANES, ROUTER_LANES), F32),
                        pltpu.VMEM((YA_BLOCKS, tm, d // YA_BLOCKS), F32)],
        compiler_params=_params("arbitrary"),
        name="merge",
    )(ret2, z2, z2, z2, proj2, proj2, x2, mod3, mod3, mod3,
      wret, wconv, wmix, dw, vec, wrh, wrl, br)


DISPATCH_TILE = 1024


def _dispatch_kernel(pos_ref, t_ref, xs_hbm, sem):
    tm = t_ref.shape[0]
    base = pl.program_id(0) * tm

    for r in range(tm):
        for k in range(EXPERT_TOP_K):
            dst = pos_ref[(base + r) * EXPERT_TOP_K + k]
            pltpu.make_async_copy(t_ref.at[pl.ds(r, 1)], xs_hbm.at[pl.ds(dst, 1)], sem).start(priority=k)
    for k in range(EXPERT_TOP_K):
        pltpu.make_async_copy(t_ref, xs_hbm.at[pl.ds(0, tm)], sem).wait()


def _dispatch(pos_flat, t2):
    n, d = t2.shape
    tm = min(DISPATCH_TILE, n)
    return pl.pallas_call(
        _dispatch_kernel,
        grid=(n // tm,),
        in_specs=[pl.BlockSpec(memory_space=pltpu.SMEM),
                  pl.BlockSpec((tm, d), lambda i: (i, 0))],
        out_specs=pl.BlockSpec(memory_space=pl.ANY),
        out_shape=jax.ShapeDtypeStruct((n * EXPERT_TOP_K, d), F32),
        scratch_shapes=[pltpu.SemaphoreType.DMA],
        compiler_params=_params("arbitrary"),
        name="dispatch",
    )(pos_flat, t2)


EXPERT_TILE = 512


def _experts_kernel(ie_ref, it_ref, lo_ref, hi_ref, nv_ref, xs_ref, wg_ref, wu_ref, wd_ref, ys_ref,
                    wg_bf, wu_bf, wd_bf):
    k = pl.program_id(0)
    prev = jnp.maximum(k - 1, 0)
    new_expert = (k == 0) | (ie_ref[k] != ie_ref[prev])
    new_tile = (k == 0) | (it_ref[k] != it_ref[prev])

    @pl.when(new_expert)
    def _():
        wg_bf[...] = wg_ref[0, 0].astype(BF16)
        wu_bf[...] = wu_ref[0, 0].astype(BF16)
        wd_bf[...] = wd_ref[0, 0].astype(BF16)

    @pl.when(k < nv_ref[0])
    def _():
        x = xs_ref[...].astype(BF16)
        g = jnp.dot(x, wg_bf[...], preferred_element_type=F32)
        u = jnp.dot(x, wu_bf[...], preferred_element_type=F32)
        hid = (g * _sigmoid(g) * u).astype(BF16)
        y = jnp.dot(hid, wd_bf[...], preferred_element_type=F32)
        rows = lax.broadcasted_iota(I32, (xs_ref.shape[0], 1), 0)
        mine = (rows >= lo_ref[k]) & (rows < hi_ref[k])

        @pl.when(new_tile)
        def _():
            ys_ref[...] = jnp.where(mine, y, 0.0)

        @pl.when(jnp.logical_not(new_tile))
        def _():
            ys_ref[...] = jnp.where(mine, y, ys_ref[...])


def _experts(meta, xs, w_gate, w_up, w_down):
    p, d = xs.shape
    ff = w_gate.shape[-1]
    n_items = meta[0].shape[0]
    grid_spec = pltpu.PrefetchScalarGridSpec(
        num_scalar_prefetch=5,
        grid=(n_items,),
        in_specs=[pl.BlockSpec((EXPERT_TILE, d), lambda k, ie, it, lo, hi, nv: (it[k], 0)),
                  pl.BlockSpec((1, 1, d, ff), lambda k, ie, it, lo, hi, nv: (0, ie[k], 0, 0)),
                  pl.BlockSpec((1, 1, d, ff), lambda k, ie, it, lo, hi, nv: (0, ie[k], 0, 0)),
                  pl.BlockSpec((1, 1, ff, d), lambda k, ie, it, lo, hi, nv: (0, ie[k], 0, 0))],
        out_specs=pl.BlockSpec((EXPERT_TILE, d), lambda k, ie, it, lo, hi, nv: (it[k], 0)),
        scratch_shapes=[pltpu.VMEM((d, ff), BF16), pltpu.VMEM((d, ff), BF16), pltpu.VMEM((ff, d), BF16)])
    return pl.pallas_call(
        _experts_kernel,
        grid_spec=grid_spec,
        out_shape=jax.ShapeDtypeStruct((p, d), F32),
        compiler_params=_params("arbitrary"),
        name="experts",
    )(*meta, xs, w_gate, w_up, w_down)


def _expert_work_items(counts, p):
    n_tiles = p // EXPERT_TILE
    n_items = n_tiles + N_EXPERTS - 1
    offs = jnp.concatenate([jnp.zeros((1,), I32), jnp.cumsum(counts)])
    first = offs[:-1] // EXPERT_TILE
    last = (offs[1:] - 1) // EXPERT_TILE
    per = jnp.where(counts > 0, last - first + 1, 0)
    ends = jnp.cumsum(per)
    starts = ends - per
    n_valid = ends[-1]
    k = jnp.arange(n_items, dtype=I32)
    kk = jnp.minimum(k, n_valid - 1)
    ie = jnp.sum((ends[None, :] <= kk[:, None]).astype(I32), axis=1)
    it = first[ie] + (kk - starts[ie])
    lo = jnp.maximum(offs[ie], it * EXPERT_TILE) - it * EXPERT_TILE
    hi = jnp.minimum(offs[ie + 1], (it + 1) * EXPERT_TILE) - it * EXPERT_TILE
    valid = k < n_valid
    lo = jnp.where(valid, lo, 0)
    hi = jnp.where(valid, hi, 0)
    return offs, (ie.astype(I32), it.astype(I32), lo.astype(I32), hi.astype(I32), n_valid.reshape(1).astype(I32))


COMBINE_HALF = 256


def _combine_kernel(pos_ref, ys_hbm, rw_ref, h1_ref, gt_ref, ln_ref, out_ref, buf, sem):
    i = pl.program_id(0)
    n_steps = pl.num_programs(0)
    hm = COMBINE_HALF
    base = i * (2 * hm)

    def slot_copy(s, k):
        return pltpu.make_async_copy(ys_hbm.at[pl.ds(0, hm)], buf.at[s, k], sem.at[s])

    def gather(first_tok, s):
        for r in range(hm):
            for k in range(EXPERT_TOP_K):
                src = pos_ref[(first_tok + r) * EXPERT_TOP_K + k]
                pltpu.make_async_copy(ys_hbm.at[pl.ds(src, 1)], buf.at[s, k, pl.ds(r, 1)],
                                      sem.at[s]).start(priority=k)

    def finish(s):
        for k in range(EXPERT_TOP_K):
            slot_copy(s, k).wait()
        rows = slice(s * hm, (s + 1) * hm)
        rw = rw_ref[rows, :]
        y = rw[:, RT_W1:RT_W1 + 1] * buf[s, 0] + rw[:, RT_W2:RT_W2 + 1] * buf[s, 1]
        out_ref[rows, :] = _ln_rows(ALPHA * h1_ref[rows, :] + gt_ref[0] * y, ln_ref[0:1, :], ln_ref[1:2, :])

    @pl.when(i == 0)
    def _():
        gather(0, 0)

    gather(base + hm, 1)
    finish(0)

    @pl.when(i + 1 < n_steps)
    def _():
        gather(base + 2 * hm, 0)

    finish(1)


def _combine(pos_flat, ys, rw, h1, mod3, ln2, seq):
    n, d = h1.shape
    tm = 2 * COMBINE_HALF
    tiles_per_seq = seq // tm
    return pl.pallas_call(
        _combine_kernel,
        grid=(n // tm,),
        in_specs=[pl.BlockSpec(memory_space=pltpu.SMEM),
                  pl.BlockSpec(memory_space=pl.ANY),
                  pl.BlockSpec((tm, ROUTER_LANES), lambda i: (i, 0)),
                  pl.BlockSpec((tm, d), lambda i: (i, 0)),
                  pl.BlockSpec((1, 1, d), lambda i: (i // tiles_per_seq, 0, 5)),
                  pl.BlockSpec(ln2.shape, lambda i: (0, 0))],
        out_specs=pl.BlockSpec((tm, d), lambda i: (i, 0)),
        out_shape=jax.ShapeDtypeStruct((n, d), F32),
        scratch_shapes=[pltpu.VMEM((2, EXPERT_TOP_K, COMBINE_HALF, d), F32),
                        pltpu.SemaphoreType.DMA((2,))],
        compiler_params=_params("arbitrary"),
        name="combine",
    )(pos_flat, ys, rw, h1, mod3, ln2)


def _rope_tables(seq):
    pos = np.arange(seq)
    r = (pos // GRID_W).astype(np.float32)
    col = (pos % GRID_W).astype(np.float32)
    n_freq = RET_QK_DIM // 4
    inv = (np.float32(ROPE_BASE) ** (-np.arange(n_freq, dtype=np.float32) / np.float32(n_freq))).astype(np.float32)
    ang = np.concatenate([r[:, None] * inv, col[:, None] * inv], axis=-1)
    ang = np.concatenate([ang, ang], axis=-1).astype(np.float32)
    sign = np.concatenate([-np.ones((RET_QK_DIM // 2,), np.float32), np.ones((RET_QK_DIM // 2,), np.float32)])
    return jnp.asarray(np.cos(ang), F32), jnp.asarray(np.sin(ang) * sign, F32)


def kernel(x, c, ctx, c_ctx, w_ada, b_ada, w_in, b_in, ret_decay_fwd, ret_decay_bwd, w_ret_out, conv_dw, conv_dw_b, conv_ln_g, conv_ln_b, w_conv_out, b_conv_out, w_mix_out, b_mix_out, ln1_g, ln1_b, w_router_grp, b_router_grp, w_router_exp, b_router_exp, w_exp_gate, w_exp_up, w_exp_down, ln2_g, ln2_b):
    b_, seq, d = x.shape
    n = b_ * seq
    assert w_ada.shape[0] == DEPTH
    mod_rows = SUBLANES
    assert b_ + 1 <= mod_rows

    cs = jnp.concatenate([c, c_ctx[None, :], jnp.zeros((mod_rows - b_ - 1, d), F32)], axis=0)
    mod = _ada(cs, w_ada[0], b_ada[0][None, :])
    mod3 = mod.reshape(mod_rows, 1, 6 * d)

    w_in3 = w_in[0].astype(BF16)
    b_in3 = b_in[0][None, :]
    dec_f = ret_decay_fwd[0].reshape(RET_HEADS, 1, 1)
    dec_b = ret_decay_bwd[0].reshape(RET_HEADS, 1, 1)

    s0f, s0b = _ctx_states(ctx, mod3, w_in3, b_in3, dec_f, dec_b, b_)

    cos, sin = _rope_tables(seq)
    x2 = x.reshape(n, d)
    proj, z = _proj(x2, mod3, w_in3, b_in3, cos, sin, seq, tm=min(1024, seq))

    ret = _retention(proj.reshape(b_, seq, PROJ_BLOCKS * d), s0f, s0b, dec_f, dec_b, d)

    vec = jnp.concatenate([conv_dw_b, conv_ln_g, conv_ln_b, b_conv_out, b_mix_out, ln1_g, ln1_b,
                           jnp.zeros((1, d), F32)], axis=0)
    tap_rows = -(-CONV_WIDTH // SUBLANES) * SUBLANES
    dw = jnp.concatenate([conv_dw[0], jnp.zeros((tap_rows - CONV_WIDTH, d), F32)], axis=0)
    pad = ROUTER_LANES - N_GROUPS - N_EXPERTS
    wr = jnp.concatenate([w_router_grp[0], w_router_exp[0], jnp.zeros((d, pad), F32)], axis=1)
    wr_hi = wr.astype(BF16)
    wr_lo = (wr - wr_hi.astype(F32)).astype(BF16)
    br = jnp.concatenate([b_router_grp[0], b_router_exp[0], jnp.zeros((pad,), F32)])[None, :]
    h1, t, route_i, route_w, cnt = _merge(
        ret.reshape(n, RET_HEADS * RET_V_DIM), proj, z, x2, mod3,
        w_ret_out[0].astype(BF16).reshape(-1, YA_BLOCKS, d // YA_BLOCKS).transpose(1, 0, 2),
        w_conv_out[0].astype(BF16), w_mix_out[0].astype(BF16),
        dw, vec, wr_hi, wr_lo, br, seq, tm=min(512, seq))

    counts = cnt[0, N_GROUPS:N_GROUPS + N_EXPERTS].astype(I32)
    offs, meta = _expert_work_items(counts, n * EXPERT_TOP_K)
    eid = route_i[:, RT_E1:RT_E2 + 1]
    seg_start = jnp.sum(jnp.where(eid[:, :, None] == jnp.arange(N_EXPERTS, dtype=I32), offs[:N_EXPERTS], 0), axis=-1)
    pos = seg_start + route_i[:, RT_RANK1:RT_RANK2 + 1]
    pos_flat = pos.reshape(n * EXPERT_TOP_K)

    xs = _dispatch(pos_flat, t)
    ys = _experts(meta, xs, w_exp_gate, w_exp_up, w_exp_down)
    ln2 = jnp.concatenate([ln2_g, ln2_b, jnp.zeros((SUBLANES - 2, d), F32)], axis=0)
    out = _combine(pos_flat, ys, route_w, h1, mod3, ln2, seq)
    return out.reshape(b_, seq, d)
```

```python
import functools

import jax
import jax.numpy as jnp
import numpy as np
from jax import lax
from jax.experimental import pallas as pl
from jax.experimental.pallas import tpu as pltpu

GRID_W = 64
RET_HEADS = 8
RET_QK_DIM = 128
RET_V_DIM = 256
RET_CHUNK = 256
ROPE_BASE = 10000.0
CONV_WIDTH = 31
CONV_HALO = 16
SUBLANES = 8
N_GROUPS = 4
EXPERTS_PER_GROUP = 8
N_EXPERTS = N_GROUPS * EXPERTS_PER_GROUP
EXPERT_TOP_K = 2
LN_EPS = 1e-5
DEPTH = 1
ALPHA = (2.0 * DEPTH) ** 0.25
ROUTER_LANES = 128

V7X_VMEM_LIMIT = 56 * 1024 * 1024

F32 = jnp.float32
BF16 = jnp.bfloat16
I32 = jnp.int32


def _params(*sem):
    return pltpu.CompilerParams(dimension_semantics=sem, vmem_limit_bytes=V7X_VMEM_LIMIT)


def _sigmoid(v):
    return 0.5 * jnp.tanh(0.5 * v) + 0.5


def _split_bf16(v):
    hi = v.astype(BF16)
    return hi, (v - hi.astype(F32)).astype(BF16)


def _dot3(a, b_hi, b_lo):
    a_hi, a_lo = _split_bf16(a)
    return (jnp.dot(a_hi, b_hi, preferred_element_type=F32)
            + jnp.dot(a_lo, b_hi, preferred_element_type=F32)
            + jnp.dot(a_hi, b_lo, preferred_element_type=F32))


def _ln_rows(v, g, b):
    mu = jnp.mean(v, axis=-1, keepdims=True)
    d = v - mu
    var = jnp.mean(d * d, axis=-1, keepdims=True)
    return d * lax.rsqrt(var + LN_EPS) * g + b


ADA_K_TILE = 256


def _ada_kernel(cs_ref, w_ref, b_ref, out_ref):
    k = pl.program_id(0)
    s = cs_ref[...]
    s = s * _sigmoid(s)
    w_hi, w_lo = _split_bf16(w_ref[...])
    part = _dot3(s, w_hi, w_lo)

    @pl.when(k == 0)
    def _():
        out_ref[...] = part + b_ref[...]

    @pl.when(k > 0)
    def _():
        out_ref[...] += part


def _ada(cs, w_ada, b_ada):
    rows, d = cs.shape
    cols = w_ada.shape[1]
    tk = ADA_K_TILE
    return pl.pallas_call(
        _ada_kernel,
        grid=(d // tk,),
        in_specs=[pl.BlockSpec((rows, tk), lambda k: (0, k)),
                  pl.BlockSpec((tk, cols), lambda k: (k, 0)),
                  pl.BlockSpec((1, cols), lambda k: (0, 0))],
        out_specs=pl.BlockSpec((rows, cols), lambda k: (0, 0)),
        out_shape=jax.ShapeDtypeStruct((rows, cols), F32),
        compiler_params=_params("arbitrary"),
        name="ada",
    )(cs, w_ada, b_ada)


def _log_sigmoid(v):
    return jnp.minimum(v, 0.0) - jnp.log(1.0 + jnp.exp(-jnp.abs(v)))


def _ctx_kernel(ctx_ref, sh_ref, sc_ref, wk_ref, wv0_ref, wv1_ref, bk_ref, bv0_ref, bv1_ref, df_ref, db_ref,
                sf_ref, sb_ref):
    lc = ctx_ref.shape[1]
    u = (ctx_ref[0] * (1.0 + sc_ref[0]) + sh_ref[0]).astype(BF16)
    k = (jnp.dot(u, wk_ref[...], preferred_element_type=F32) + bk_ref[...]) * RET_QK_DIM ** -0.5
    v = jnp.concatenate(
        [(jnp.dot(u, w[...], preferred_element_type=F32) + b[...]).astype(BF16)
         for w, b in ((wv0_ref, bv0_ref), (wv1_ref, bv1_ref))], axis=1)
    pos = lax.broadcasted_iota(I32, (lc, 1), 0).astype(F32)
    dn = (((0,), (0,)), ((), ()))
    for h in range(RET_HEADS):
        lgf = _log_sigmoid(df_ref[h])
        lgb = _log_sigmoid(db_ref[h])
        kh = k[:, h * RET_QK_DIM:(h + 1) * RET_QK_DIM]
        vh = v[:, h * RET_V_DIM:(h + 1) * RET_V_DIM]
        kf = (kh * jnp.exp((lc - 1.0 - pos) * lgf)).astype(BF16)
        kb = (kh * jnp.exp(pos * lgb)).astype(BF16)
        sf_ref[0, h] = lax.dot_general(kf, vh, dn, preferred_element_type=F32)
        sb_ref[0, h] = lax.dot_general(kb, vh, dn, preferred_element_type=F32)


def _ctx_states(ctx, mod3, w_in3, b_in3, dec_f, dec_b, ctx_row):
    b_, lc, d = ctx.shape
    assert RET_HEADS * RET_QK_DIM == d and RET_HEADS * RET_V_DIM == 2 * d
    st = jax.ShapeDtypeStruct((b_, RET_HEADS, RET_QK_DIM, RET_V_DIM), F32)
    st_spec = pl.BlockSpec((1, RET_HEADS, RET_QK_DIM, RET_V_DIM), lambda b: (b, 0, 0, 0))
    wblk = lambda blk: pl.BlockSpec((d, d), lambda b: (0, blk))
    bblk = lambda blk: pl.BlockSpec((1, d), lambda b: (0, blk))
    dec_spec = pl.BlockSpec((RET_HEADS, 1, 1), lambda b: (0, 0, 0))
    return pl.pallas_call(
        _ctx_kernel,
        grid=(b_,),
        in_specs=[pl.BlockSpec((1, lc, d), lambda b: (b, 0, 0)),
                  pl.BlockSpec((1, 1, d), lambda b: (ctx_row, 0, 0)),
                  pl.BlockSpec((1, 1, d), lambda b: (ctx_row, 0, 1)),
                  wblk(PB_K), wblk(PB_V), wblk(PB_V + 1),
                  bblk(PB_K), bblk(PB_V), bblk(PB_V + 1),
                  dec_spec, dec_spec],
        out_specs=[st_spec, st_spec],
        out_shape=[st, st],
        compiler_params=_params("arbitrary"),
        name="ctx",
    )(ctx, mod3, mod3, w_in3, w_in3, w_in3, b_in3, b_in3, b_in3, dec_f, dec_b)


PROJ_BLOCKS = 8
PB_Q, PB_K, PB_V, PB_G, PB_GATE = 0, 1, 2, 4, 6
WB_GLU, WB_GATE = 6, 8
PROJ_STEPS = PROJ_BLOCKS // 2 + 1


def _proj_kernel(x_ref, sh_ref, sc_ref, w_ref, b_ref, cos_ref, sin_ref, out_ref, z_ref, u_scr):
    j = pl.program_id(1)
    d = x_ref.shape[1]
    halves = (slice(0, d), slice(d, 2 * d))

    @pl.when(j == 0)
    def _():
        u_scr[...] = (x_ref[...] * (1.0 + sc_ref[0]) + sh_ref[0]).astype(BF16)

    def mm(blk):
        cols = slice(blk * d, (blk + 1) * d)
        return jnp.dot(u_scr[...], w_ref[:, cols], preferred_element_type=F32) + b_ref[:, cols]

    def rope(acc, half):
        cos = cos_ref[...]
        sin = sin_ref[...]
        for h in range(acc.shape[1] // RET_QK_DIM):
            seg = acc[:, h * RET_QK_DIM:(h + 1) * RET_QK_DIM]
            sl = slice(half.start + h * RET_QK_DIM, half.start + (h + 1) * RET_QK_DIM)
            out_ref[:, sl] = (seg * cos + pltpu.roll(seg, RET_QK_DIM // 2, 1) * sin).astype(BF16)

    def step(jj):
        if jj == 0:
            rope(mm(PB_Q), halves[0])
            rope(mm(PB_K) * RET_QK_DIM ** -0.5, halves[1])
        elif jj == 1:
            for hh, half in enumerate(halves):
                out_ref[:, half] = mm(PB_V + hh).astype(BF16)
        elif jj == 2:
            for hh, half in enumerate(halves):
                acc = mm(PB_G + hh)
                out_ref[:, half] = (acc * _sigmoid(acc)).astype(BF16)
        elif jj == 3:
            for hh, half in enumerate(halves):
                out_ref[:, half] = _sigmoid(mm(WB_GATE + hh)).astype(BF16)
        else:
            z_ref[...] = (mm(WB_GLU) * _sigmoid(mm(WB_GLU + 1))).astype(BF16)

    for jj in range(PROJ_STEPS):
        pl.when(j == jj)(functools.partial(step, jj))


def _proj(x2, mod3, w_in3, b_in3, cos, sin, seq, tm):
    n, d = x2.shape
    tiles_per_seq = seq // tm
    last_pair = PROJ_BLOCKS // 2 - 1
    return pl.pallas_call(
        _proj_kernel,
        grid=(n // tm, PROJ_STEPS),
        in_specs=[pl.BlockSpec((tm, d), lambda i, j: (i, 0)),
                  pl.BlockSpec((1, 1, d), lambda i, j: (i // tiles_per_seq, 0, 0)),
                  pl.BlockSpec((1, 1, d), lambda i, j: (i // tiles_per_seq, 0, 1)),
                  pl.BlockSpec(memory_space=pltpu.VMEM),
                  pl.BlockSpec(memory_space=pltpu.VMEM),
                  pl.BlockSpec((tm, RET_QK_DIM), lambda i, j: (i % tiles_per_seq, 0)),
                  pl.BlockSpec((tm, RET_QK_DIM), lambda i, j: (i % tiles_per_seq, 0))],
        out_specs=[pl.BlockSpec((tm, 2 * d), lambda i, j: (i, jnp.minimum(j, last_pair))),
                   pl.BlockSpec((tm, d), lambda i, j: (i, 0))],
        out_shape=[jax.ShapeDtypeStruct((n, PROJ_BLOCKS * d), BF16),
                   jax.ShapeDtypeStruct((n, d), BF16)],
        scratch_shapes=[pltpu.VMEM((tm, d), BF16)],
        compiler_params=_params("arbitrary", "arbitrary"),
        name="proj",
    )(x2, mod3, mod3, w_in3, b_in3, cos, sin)


def _ret_kernel(q_ref, k_ref, v_ref, g_ref, s0f_ref, s0b_ref, df_ref, db_ref, out_ref,
                sf_all, sb_all, p_scr, o_scr):
    c_ = RET_CHUNK
    n_chunks = q_ref.shape[1] // c_
    dn_t = (((0,), (0,)), ((), ()))
    lgf = _log_sigmoid(df_ref[0])
    lgb = _log_sigmoid(db_ref[0])
    ri = lax.broadcasted_iota(I32, (c_, c_), 0)
    ci = lax.broadcasted_iota(I32, (c_, c_), 1)
    diff = (ri - ci).astype(F32)
    mask = jnp.where(diff > 0, jnp.exp(diff * lgf), jnp.where(diff < 0, jnp.exp(-diff * lgb), 2.0))
    pos = lax.broadcasted_iota(I32, (c_, 1), 0).astype(F32)
    qdec_f = jnp.exp((pos + 1.0) * lgf)
    qdec_b = jnp.exp((c_ - pos) * lgb)
    kdec_f = jnp.exp((c_ - 1.0 - pos) * lgf)
    kdec_b = jnp.exp(pos * lgb)
    cdec_f = jnp.exp(c_ * lgf)
    cdec_b = jnp.exp(c_ * lgb)

    def chunk(c):
        return slice(c * c_, (c + 1) * c_)

    for c in range(n_chunks):
        k = k_ref[0, chunk(c), :].astype(F32)
        v = v_ref[0, chunk(c), :]
        sf_all[c] = lax.dot_general((k * kdec_f).astype(BF16), v, dn_t, preferred_element_type=F32)
        sb_all[c] = lax.dot_general((k * kdec_b).astype(BF16), v, dn_t, preferred_element_type=F32)

    sf = s0f_ref[0, 0]
    sb = s0b_ref[0, 0]
    for t in range(n_chunks):
        kv = sf_all[t]
        sf_all[t] = sf
        sf = sf * cdec_f + kv
        c = n_chunks - 1 - t
        kv = sb_all[c]
        sb_all[c] = sb
        sb = sb * cdec_b + kv

    for c in range(n_chunks):
        s = lax.dot_general(q_ref[0, chunk(c), :], k_ref[0, chunk(c), :], (((1,), (1,)), ((), ())),
                            preferred_element_type=F32)
        p_scr[c] = (s * mask).astype(BF16)

    for c in range(n_chunks):
        q = q_ref[0, chunk(c), :]
        o = jnp.dot(p_scr[c], v_ref[0, chunk(c), :], preferred_element_type=F32)
        o = o + qdec_f * jnp.dot(q, sf_all[c].astype(BF16), preferred_element_type=F32)
        o = o + qdec_b * jnp.dot(q, sb_all[c].astype(BF16), preferred_element_type=F32)
        o_scr[chunk(c), :] = o

    for c in range(n_chunks):
        o = o_scr[chunk(c), :]
        mu = jnp.mean(o, axis=-1, keepdims=True)
        d = o - mu
        var = jnp.mean(d * d, axis=-1, keepdims=True)
        on = d * lax.rsqrt(var + LN_EPS)
        out_ref[0, chunk(c), :] = (on * g_ref[0, chunk(c), :].astype(F32)).astype(BF16)


def _retention(proj3, s0f, s0b, dec_f, dec_b, d):
    b_, seq, _ = proj3.shape
    qb, kb = PB_Q * d // RET_QK_DIM, PB_K * d // RET_QK_DIM
    vb, gb = PB_V * d // RET_V_DIM, PB_G * d // RET_V_DIM
    n_chunks = seq // RET_CHUNK
    st_spec = pl.BlockSpec((1, 1, RET_QK_DIM, RET_V_DIM), lambda b, h: (b, h, 0, 0))
    st_all = pltpu.VMEM((n_chunks, RET_QK_DIM, RET_V_DIM), F32)
    return pl.pallas_call(
        _ret_kernel,
        grid=(b_, RET_HEADS),
        in_specs=[pl.BlockSpec((1, seq, RET_QK_DIM), lambda b, h: (b, 0, qb + h)),
                  pl.BlockSpec((1, seq, RET_QK_DIM), lambda b, h: (b, 0, kb + h)),
                  pl.BlockSpec((1, seq, RET_V_DIM), lambda b, h: (b, 0, vb + h)),
                  pl.BlockSpec((1, seq, RET_V_DIM), lambda b, h: (b, 0, gb + h)),
                  st_spec, st_spec,
                  pl.BlockSpec((1, 1, 1), lambda b, h: (h, 0, 0)),
                  pl.BlockSpec((1, 1, 1), lambda b, h: (h, 0, 0))],
        out_specs=pl.BlockSpec((1, seq, RET_V_DIM), lambda b, h: (b, 0, h)),
        out_shape=jax.ShapeDtypeStruct((b_, seq, RET_HEADS * RET_V_DIM), BF16),
        scratch_shapes=[st_all, st_all,
                        pltpu.VMEM((n_chunks, RET_CHUNK, RET_CHUNK), BF16),
                        pltpu.VMEM((seq, RET_V_DIM), F32)],
        compiler_params=_params("arbitrary", "arbitrary"),
        name="ret",
    )(proj3, proj3, proj3, proj3, s0f, s0b, dec_f, dec_b)


VEC_CONV_B, VEC_CLN_G, VEC_CLN_B, VEC_BCONV, VEC_BMIX, VEC_LN1_G, VEC_LN1_B = range(7)
CONV_ROWS = 64
YA_BLOCKS = 4
LANES = 128
RT_E1, RT_E2, RT_RANK1, RT_RANK2 = 0, 1, 2, 3
RT_W1, RT_W2 = 0, 1


def _merge_kernel(ret_ref, z_ref, zp_ref, zn_ref, ga_ref, gb_ref, x_ref, gt_ref, shf_ref, scf_ref,
                  wret_ref, wconv_ref, wmix_ref, dw_ref, vec_ref, wrh_ref, wrl_ref, br_ref,
                  h1_ref, t_ref, ri_ref, rw_ref, cnt_ref, zext, zc_scr, cnt_scr, ya_scr, *, tiles_per_seq):
    tm, d = x_ref.shape
    step = pl.program_id(0)
    ti = step % tiles_per_seq
    vec = vec_ref[...]

    def row(r):
        return vec[r:r + 1, :]


    has_prev = (ti > 0).astype(F32)
    has_next = (ti < tiles_per_seq - 1).astype(F32)
    zext[0:CONV_HALO, :] = zp_ref[...].astype(F32) * has_prev
    zext[CONV_HALO:CONV_HALO + tm, :] = z_ref[...].astype(F32)
    zext[CONV_HALO + tm:, :] = zn_ref[...].astype(F32) * has_next
    base = CONV_HALO - CONV_WIDTH // 2
    win_rows = CONV_ROWS + 2 * CONV_HALO

    n_lt = d // LANES
    kc = ret_ref.shape[1] // n_lt

    def conv_piece(r0, lt):
        ls = slice(lt * LANES, (lt + 1) * LANES)
        win = zext[pl.ds(r0, win_rows), ls]
        acc = jnp.zeros((CONV_ROWS, LANES), F32) + vec_ref[VEC_CONV_B:VEC_CONV_B + 1, ls]
        for s in range(SUBLANES):
            taps = [w for w in range(CONV_WIDTH) if (base + w) % SUBLANES == s]
            if not taps:
                continue
            sh = win if s == 0 else pltpu.roll(win, win_rows - s, 0)
            for w in taps:
                a = (base + w) - s
                acc = acc + sh[a:a + CONV_ROWS, :] * dw_ref[w:w + 1, ls]
        return acc

    def conv_rows(r, carry):
        r0 = pl.multiple_of(r * (2 * CONV_ROWS), 2 * CONV_ROWS)
        ya = None
        accs = []
        for lt in range(n_lt):
            part = jnp.dot(ret_ref[:, lt * kc:(lt + 1) * kc], wret_ref[r, lt * kc:(lt + 1) * kc, :],
                           preferred_element_type=F32)
            ya = part if ya is None else ya + part
            accs.append((conv_piece(r0, lt), conv_piece(r0 + CONV_ROWS, lt)))
        ya_scr[r] = ya
        for lt in range(n_lt):
            ls = slice(lt * LANES, (lt + 1) * LANES)
            zc_scr[pl.ds(r0, CONV_ROWS), ls] = accs[lt][0]
            zc_scr[pl.ds(r0 + CONV_ROWS, CONV_ROWS), ls] = accs[lt][1]
        return carry

    lax.fori_loop(0, YA_BLOCKS, conv_rows, 0)
    y_a = jnp.concatenate([ya_scr[q] for q in range(YA_BLOCKS)], axis=1)
    zc = _ln_rows(zc_scr[...], row(VEC_CLN_G), row(VEC_CLN_B))
    zc = zc * _sigmoid(zc)
    y_b = jnp.dot(zc.astype(BF16), wconv_ref[...], preferred_element_type=F32) + row(VEC_BCONV)

    mixed = ga_ref[...].astype(F32) * y_a + gb_ref[...].astype(F32) * y_b
    mix = jnp.dot(mixed.astype(BF16), wmix_ref[...], preferred_element_type=F32) + row(VEC_BMIX)
    h1 = _ln_rows(ALPHA * x_ref[...] + gt_ref[0] * mix, row(VEC_LN1_G), row(VEC_LN1_B))
    h1_ref[...] = h1
    t = h1 * (1.0 + scf_ref[0]) + shf_ref[0]
    t_ref[...] = t

    logits = _dot3(t, wrh_ref[...], wrl_ref[...]) + br_ref[...]

    lane = lax.broadcasted_iota(I32, logits.shape, 1).astype(F32)
    neg = -jnp.inf
    big = float(ROUTER_LANES)
    is_grp = lane < N_GROUPS
    gl = jnp.where(is_grp, logits, neg)
    gmax = jnp.max(gl, axis=-1, keepdims=True)
    gidx = jnp.min(jnp.where(gl == gmax, lane, big), axis=-1, keepdims=True)
    gsum = jnp.sum(jnp.where(is_grp, jnp.exp(gl - gmax), 0.0), axis=-1, keepdims=True)
    grp_w = 1.0 / gsum
    lo = N_GROUPS + gidx * EXPERTS_PER_GROUP
    el = jnp.where(lane >= lo, jnp.where(lane < lo + EXPERTS_PER_GROUP, logits, neg), neg)
    m1 = jnp.max(el, axis=-1, keepdims=True)
    i1 = jnp.min(jnp.where(el == m1, lane, big), axis=-1, keepdims=True)
    el2 = jnp.where(lane == i1, neg, el)
    m2 = jnp.max(el2, axis=-1, keepdims=True)
    i2 = jnp.min(jnp.where(el2 == m2, lane, big), axis=-1, keepdims=True)
    r = jnp.exp(m2 - m1)
    w1 = grp_w / (1.0 + r)
    w2 = grp_w * r / (1.0 + r)

    @pl.when(step == 0)
    def _():
        cnt_scr[...] = jnp.zeros_like(cnt_scr)

    oh1 = lane == i1
    oh2 = lane == i2
    oh = jnp.where(oh1, 1.0, jnp.where(oh2, 1.0, 0.0))
    tri = jnp.where(lax.broadcasted_iota(I32, (tm, tm), 0) > lax.broadcasted_iota(I32, (tm, tm), 1), 1.0, 0.0)
    before = jnp.dot(tri.astype(BF16), oh.astype(BF16), preferred_element_type=F32) + cnt_scr[0:1, :]
    rank1 = jnp.sum(jnp.where(oh1, before, 0.0), axis=-1, keepdims=True)
    rank2 = jnp.sum(jnp.where(oh2, before, 0.0), axis=-1, keepdims=True)
    cnt = cnt_scr[0:1, :] + jnp.sum(oh, axis=0, keepdims=True)
    cnt_scr[...] = jnp.broadcast_to(cnt, cnt_scr.shape)
    cnt_ref[...] = jnp.broadcast_to(cnt, cnt_ref.shape)

    e1 = i1 - N_GROUPS
    e2 = i2 - N_GROUPS
    ri = jnp.where(lane == RT_E1, e1, jnp.where(lane == RT_E2, e2,
                   jnp.where(lane == RT_RANK1, rank1, jnp.where(lane == RT_RANK2, rank2, 0.0))))
    ri_ref[...] = ri.astype(I32)
    rw_ref[...] = jnp.where(lane == RT_W1, w1, jnp.where(lane == RT_W2, w2, 0.0))


def _merge(ret2, proj2, z2, x2, mod3, wret, wconv, wmix, dw, vec, wrh, wrl, br, seq, tm):
    n, d = x2.shape
    assert tm == YA_BLOCKS * 2 * CONV_ROWS
    tiles_per_seq = seq // tm
    hb = tm // CONV_HALO
    last_hb = n // CONV_HALO - 1
    const = lambda i: (0, 0)
    bat = lambda k: (lambda i: (i // tiles_per_seq, 0, k))
    return pl.pallas_call(
        functools.partial(_merge_kernel, tiles_per_seq=tiles_per_seq),
        grid=(n // tm,),
        in_specs=[pl.BlockSpec((tm, ret2.shape[1]), lambda i: (i, 0)),
                  pl.BlockSpec((tm, d), lambda i: (i, 0)),
                  pl.BlockSpec((CONV_HALO, d), lambda i: (jnp.maximum(i * hb - 1, 0), 0)),
                  pl.BlockSpec((CONV_HALO, d), lambda i: (jnp.minimum((i + 1) * hb, last_hb), 0)),
                  pl.BlockSpec((tm, d), lambda i: (i, PB_GATE)),
                  pl.BlockSpec((tm, d), lambda i: (i, PB_GATE + 1)),
                  pl.BlockSpec((tm, d), lambda i: (i, 0)),
                  pl.BlockSpec((1, 1, d), bat(2)),
                  pl.BlockSpec((1, 1, d), bat(3)),
                  pl.BlockSpec((1, 1, d), bat(4)),
                  pl.BlockSpec(wret.shape, lambda i: (0, 0, 0)),
                  pl.BlockSpec(wconv.shape, const),
                  pl.BlockSpec(wmix.shape, const),
                  pl.BlockSpec(dw.shape, const),
                  pl.BlockSpec(vec.shape, const),
                  pl.BlockSpec(wrh.shape, const),
                  pl.BlockSpec(wrl.shape, const),
                  pl.BlockSpec(br.shape, const)],
        out_specs=[pl.BlockSpec((tm, d), lambda i: (i, 0)),
                   pl.BlockSpec((tm, d), lambda i: (i, 0)),
                   pl.BlockSpec((tm, ROUTER_LANES), lambda i: (i, 0)),
                   pl.BlockSpec((tm, ROUTER_LANES), lambda i: (i, 0)),
                   pl.BlockSpec((SUBLANES, ROUTER_LANES), const)],
        out_shape=[jax.ShapeDtypeStruct((n, d), F32),
                   jax.ShapeDtypeStruct((n, d), F32),
                   jax.ShapeDtypeStruct((n, ROUTER_LANES), I32),
                   jax.ShapeDtypeStruct((n, ROUTER_LANES), F32),
                   jax.ShapeDtypeStruct((SUBLANES, ROUTER_LANES), F32)],
        scratch_shapes=[pltpu.VMEM((tm + 2 * CONV_HALO, d), F32),
                        pltpu.VMEM((tm, d), F32),
                        pltpu.VMEM((SUBLANES, ROUTER_LANES), F32),
                        pltpu.VMEM((YA_BLOCKS, tm, d // YA_BLOCKS), F32)],
        compiler_params=_params("arbitrary"),
        name="merge",
    )(ret2, z2, z2, z2, proj2, proj2, x2, mod3, mod3, mod3,
      wret, wconv, wmix, dw, vec, wrh, wrl, br)


DISPATCH_TILE = 1024


def _dispatch_kernel(pos_ref, t_ref, xs_hbm, sem):
    tm = t_ref.shape[0]
    base = pl.program_id(0) * tm

    for r in range(tm):
        for k in range(EXPERT_TOP_K):
            dst = pos_ref[(base + r) * EXPERT_TOP_K + k]
            pltpu.make_async_copy(t_ref.at[pl.ds(r, 1)], xs_hbm.at[pl.ds(dst, 1)], sem).start(priority=k)
    for k in range(EXPERT_TOP_K):
        pltpu.make_async_copy(t_ref, xs_hbm.at[pl.ds(0, tm)], sem).wait()


def _dispatch(pos_flat, t2):
    n, d = t2.shape
    tm = min(DISPATCH_TILE, n)
    return pl.pallas_call(
        _dispatch_kernel,
        grid=(n // tm,),
        in_specs=[pl.BlockSpec(memory_space=pltpu.SMEM),
                  pl.BlockSpec((tm, d), lambda i: (i, 0))],
        out_specs=pl.BlockSpec(memory_space=pl.ANY),
        out_shape=jax.ShapeDtypeStruct((n * EXPERT_TOP_K, d), F32),
        scratch_shapes=[pltpu.SemaphoreType.DMA],
        compiler_params=_params("arbitrary"),
        name="dispatch",
    )(pos_flat, t2)


EXPERT_TILE = 512


def _experts_kernel(ie_ref, it_ref, lo_ref, hi_ref, nv_ref, xs_ref, wg_hbm, wu_hbm, wd_hbm, ys_ref,
                    wg_bf, wu_bf, wd_bf, wg_st, wu_st, wd_st, sem):
    k = pl.program_id(0)
    n_items = ie_ref.shape[0]
    prev = jnp.maximum(k - 1, 0)
    new_expert = (k == 0) | (ie_ref[k] != ie_ref[prev])
    new_tile = (k == 0) | (it_ref[k] != it_ref[prev])
    slot = nv_ref[1 + k]
    nxt = nv_ref[1 + n_items + k]

    def copies(e, s):
        return (pltpu.make_async_copy(wg_hbm.at[0, e], wg_st.at[s], sem.at[s]),
                pltpu.make_async_copy(wu_hbm.at[0, e], wu_st.at[s], sem.at[s]),
                pltpu.make_async_copy(wd_hbm.at[0, e], wd_st.at[s], sem.at[s]))

    @pl.when(k == 0)
    def _():
        for cp in copies(ie_ref[0], 0):
            cp.start()

    @pl.when(new_expert)
    def _():
        for cp in copies(ie_ref[k], slot):
            cp.wait()

        @pl.when(nxt >= 0)
        def _():
            for cp in copies(nxt, 1 - slot):
                cp.start()

        wg_bf[...] = wg_st[slot].astype(BF16)
        wu_bf[...] = wu_st[slot].astype(BF16)
        wd_bf[...] = wd_st[slot].astype(BF16)

    @pl.when(k < nv_ref[0])
    def _():
        x = xs_ref[...].astype(BF16)
        g = jnp.dot(x, wg_bf[...], preferred_element_type=F32)
        u = jnp.dot(x, wu_bf[...], preferred_element_type=F32)
        hid = (g * _sigmoid(g) * u).astype(BF16)
        y = jnp.dot(hid, wd_bf[...], preferred_element_type=F32)
        rows = lax.broadcasted_iota(I32, (xs_ref.shape[0], 1), 0)
        mine = (rows >= lo_ref[k]) & (rows < hi_ref[k])

        @pl.when(new_tile)
        def _():
            ys_ref[...] = jnp.where(mine, y, 0.0)

        @pl.when(jnp.logical_not(new_tile))
        def _():
            ys_ref[...] = jnp.where(mine, y, ys_ref[...])


def _experts(meta, xs, w_gate, w_up, w_down):
    p, d = xs.shape
    ff = w_gate.shape[-1]
    n_items = meta[0].shape[0]
    grid_spec = pltpu.PrefetchScalarGridSpec(
        num_scalar_prefetch=5,
        grid=(n_items,),
        in_specs=[pl.BlockSpec((EXPERT_TILE, d), lambda k, ie, it, lo, hi, nv: (it[k], 0)),
                  pl.BlockSpec(memory_space=pl.ANY),
                  pl.BlockSpec(memory_space=pl.ANY),
                  pl.BlockSpec(memory_space=pl.ANY)],
        out_specs=pl.BlockSpec((EXPERT_TILE, d), lambda k, ie, it, lo, hi, nv: (it[k], 0)),
        scratch_shapes=[pltpu.VMEM((d, ff), BF16), pltpu.VMEM((d, ff), BF16), pltpu.VMEM((ff, d), BF16),
                        pltpu.VMEM((2, d, ff), F32), pltpu.VMEM((2, d, ff), F32), pltpu.VMEM((2, ff, d), F32),
                        pltpu.SemaphoreType.DMA((2,))])
    return pl.pallas_call(
        _experts_kernel,
        grid_spec=grid_spec,
        out_shape=jax.ShapeDtypeStruct((p, d), F32),
        compiler_params=_params("arbitrary"),
        name="experts",
    )(*meta, xs, w_gate, w_up, w_down)


def _expert_work_items(counts, p):
    n_tiles = p // EXPERT_TILE
    n_items = n_tiles + N_EXPERTS - 1
    offs = jnp.concatenate([jnp.zeros((1,), I32), jnp.cumsum(counts)])
    first = offs[:-1] // EXPERT_TILE
    last = (offs[1:] - 1) // EXPERT_TILE
    per = jnp.where(counts > 0, last - first + 1, 0)
    ends = jnp.cumsum(per)
    starts = ends - per
    n_valid = ends[-1]
    k = jnp.arange(n_items, dtype=I32)
    kk = jnp.minimum(k, n_valid - 1)
    ie = jnp.sum((ends[None, :] <= kk[:, None]).astype(I32), axis=1)
    it = first[ie] + (kk - starts[ie])
    lo = jnp.maximum(offs[ie], it * EXPERT_TILE) - it * EXPERT_TILE
    hi = jnp.minimum(offs[ie + 1], (it + 1) * EXPERT_TILE) - it * EXPERT_TILE
    valid = k < n_valid
    lo = jnp.where(valid, lo, 0)
    hi = jnp.where(valid, hi, 0)
    change = jnp.concatenate([jnp.ones((1,), bool), ie[1:] != ie[:-1]])
    seg = jnp.cumsum(change.astype(I32)) - 1
    first_idx = jnp.argmax(seg[None, :] == k[:, None], axis=1)
    nxt_seg = seg + 1
    nxt_e = jnp.where(nxt_seg <= seg[-1], ie[first_idx[jnp.minimum(nxt_seg, n_items - 1)]], -1)
    nv = jnp.concatenate([n_valid.reshape(1), seg % 2, nxt_e]).astype(I32)
    return offs, (ie.astype(I32), it.astype(I32), lo.astype(I32), hi.astype(I32), nv)


COMBINE_HALF = 256


def _combine_kernel(pos_ref, ys_hbm, rw_ref, h1_ref, gt_ref, ln_ref, out_ref, buf, sem):
    i = pl.program_id(0)
    n_steps = pl.num_programs(0)
    hm = COMBINE_HALF
    base = i * (2 * hm)

    def slot_copy(s, k):
        return pltpu.make_async_copy(ys_hbm.at[pl.ds(0, hm)], buf.at[s, k], sem.at[s])

    def gather(first_tok, s):
        for r in range(hm):
            for k in range(EXPERT_TOP_K):
                src = pos_ref[(first_tok + r) * EXPERT_TOP_K + k]
                pltpu.make_async_copy(ys_hbm.at[pl.ds(src, 1)], buf.at[s, k, pl.ds(r, 1)],
                                      sem.at[s]).start(priority=k)

    def finish(s):
        for k in range(EXPERT_TOP_K):
            slot_copy(s, k).wait()
        rows = slice(s * hm, (s + 1) * hm)
        rw = rw_ref[rows, :]
        y = rw[:, RT_W1:RT_W1 + 1] * buf[s, 0] + rw[:, RT_W2:RT_W2 + 1] * buf[s, 1]
        out_ref[rows, :] = _ln_rows(ALPHA * h1_ref[rows, :] + gt_ref[0] * y, ln_ref[0:1, :], ln_ref[1:2, :])

    @pl.when(i == 0)
    def _():
        gather(0, 0)

    gather(base + hm, 1)
    finish(0)

    @pl.when(i + 1 < n_steps)
    def _():
        gather(base + 2 * hm, 0)

    finish(1)


def _combine(pos_flat, ys, rw, h1, mod3, ln2, seq):
    n, d = h1.shape
    tm = 2 * COMBINE_HALF
    tiles_per_seq = seq // tm
    return pl.pallas_call(
        _combine_kernel,
        grid=(n // tm,),
        in_specs=[pl.BlockSpec(memory_space=pltpu.SMEM),
                  pl.BlockSpec(memory_space=pl.ANY),
                  pl.BlockSpec((tm, ROUTER_LANES), lambda i: (i, 0)),
                  pl.BlockSpec((tm, d), lambda i: (i, 0)),
                  pl.BlockSpec((1, 1, d), lambda i: (i // tiles_per_seq, 0, 5)),
                  pl.BlockSpec(ln2.shape, lambda i: (0, 0))],
        out_specs=pl.BlockSpec((tm, d), lambda i: (i, 0)),
        out_shape=jax.ShapeDtypeStruct((n, d), F32),
        scratch_shapes=[pltpu.VMEM((2, EXPERT_TOP_K, COMBINE_HALF, d), F32),
                        pltpu.SemaphoreType.DMA((2,))],
        compiler_params=_params("arbitrary"),
        name="combine",
    )(pos_flat, ys, rw, h1, mod3, ln2)


def _rope_tables(seq):
    pos = np.arange(seq)
    r = (pos // GRID_W).astype(np.float32)
    col = (pos % GRID_W).astype(np.float32)
    n_freq = RET_QK_DIM // 4
    inv = (np.float32(ROPE_BASE) ** (-np.arange(n_freq, dtype=np.float32) / np.float32(n_freq))).astype(np.float32)
    ang = np.concatenate([r[:, None] * inv, col[:, None] * inv], axis=-1)
    ang = np.concatenate([ang, ang], axis=-1).astype(np.float32)
    sign = np.concatenate([-np.ones((RET_QK_DIM // 2,), np.float32), np.ones((RET_QK_DIM // 2,), np.float32)])
    return jnp.asarray(np.cos(ang), F32), jnp.asarray(np.sin(ang) * sign, F32)


def kernel(x, c, ctx, c_ctx, w_ada, b_ada, w_in, b_in, ret_decay_fwd, ret_decay_bwd, w_ret_out, conv_dw, conv_dw_b, conv_ln_g, conv_ln_b, w_conv_out, b_conv_out, w_mix_out, b_mix_out, ln1_g, ln1_b, w_router_grp, b_router_grp, w_router_exp, b_router_exp, w_exp_gate, w_exp_up, w_exp_down, ln2_g, ln2_b):
    b_, seq, d = x.shape
    n = b_ * seq
    assert w_ada.shape[0] == DEPTH
    mod_rows = SUBLANES
    assert b_ + 1 <= mod_rows

    cs = jnp.concatenate([c, c_ctx[None, :], jnp.zeros((mod_rows - b_ - 1, d), F32)], axis=0)
    mod = _ada(cs, w_ada[0], b_ada[0][None, :])
    mod3 = mod.reshape(mod_rows, 1, 6 * d)

    w_in3 = w_in[0].astype(BF16)
    b_in3 = b_in[0][None, :]
    dec_f = ret_decay_fwd[0].reshape(RET_HEADS, 1, 1)
    dec_b = ret_decay_bwd[0].reshape(RET_HEADS, 1, 1)

    s0f, s0b = _ctx_states(ctx, mod3, w_in3, b_in3, dec_f, dec_b, b_)

    cos, sin = _rope_tables(seq)
    x2 = x.reshape(n, d)
    proj, z = _proj(x2, mod3, w_in3, b_in3, cos, sin, seq, tm=min(1024, seq))

    ret = _retention(proj.reshape(b_, seq, PROJ_BLOCKS * d), s0f, s0b, dec_f, dec_b, d)

    vec = jnp.concatenate([conv_dw_b, conv_ln_g, conv_ln_b, b_conv_out, b_mix_out, ln1_g, ln1_b,
                           jnp.zeros((1, d), F32)], axis=0)
    tap_rows = -(-CONV_WIDTH // SUBLANES) * SUBLANES
    dw = jnp.concatenate([conv_dw[0], jnp.zeros((tap_rows - CONV_WIDTH, d), F32)], axis=0)
    pad = ROUTER_LANES - N_GROUPS - N_EXPERTS
    wr = jnp.concatenate([w_router_grp[0], w_router_exp[0], jnp.zeros((d, pad), F32)], axis=1)
    wr_hi = wr.astype(BF16)
    wr_lo = (wr - wr_hi.astype(F32)).astype(BF16)
    br = jnp.concatenate([b_router_grp[0], b_router_exp[0], jnp.zeros((pad,), F32)])[None, :]
    h1, t, route_i, route_w, cnt = _merge(
        ret.reshape(n, RET_HEADS * RET_V_DIM), proj, z, x2, mod3,
        w_ret_out[0].astype(BF16).reshape(-1, YA_BLOCKS, d // YA_BLOCKS).transpose(1, 0, 2),
        w_conv_out[0].astype(BF16), w_mix_out[0].astype(BF16),
        dw, vec, wr_hi, wr_lo, br, seq, tm=min(512, seq))

    counts = cnt[0, N_GROUPS:N_GROUPS + N_EXPERTS].astype(I32)
    offs, meta = _expert_work_items(counts, n * EXPERT_TOP_K)
    eid = route_i[:, RT_E1:RT_E2 + 1]
    seg_start = jnp.sum(jnp.where(eid[:, :, None] == jnp.arange(N_EXPERTS, dtype=I32), offs[:N_EXPERTS], 0), axis=-1)
    pos = seg_start + route_i[:, RT_RANK1:RT_RANK2 + 1]
    pos_flat = pos.reshape(n * EXPERT_TOP_K)

    xs = _dispatch(pos_flat, t)
    ys = _experts(meta, xs, w_exp_gate, w_exp_up, w_exp_down)
    ln2 = jnp.concatenate([ln2_g, ln2_b, jnp.zeros((SUBLANES - 2, d), F32)], axis=0)
    out = _combine(pos_flat, ys, route_w, h1, mod3, ln2, seq)
    return out.reshape(b_, seq, d)
```

```python
import functools

import jax
import jax.numpy as jnp
import numpy as np
from jax import lax
from jax.experimental import pallas as pl
from jax.experimental.pallas import tpu as pltpu

GRID_W = 64
RET_HEADS = 8
RET_QK_DIM = 128
RET_V_DIM = 256
RET_CHUNK = 256
ROPE_BASE = 10000.0
CONV_WIDTH = 31
CONV_HALO = 16
SUBLANES = 8
N_GROUPS = 4
EXPERTS_PER_GROUP = 8
N_EXPERTS = N_GROUPS * EXPERTS_PER_GROUP
EXPERT_TOP_K = 2
LN_EPS = 1e-5
DEPTH = 1
ALPHA = (2.0 * DEPTH) ** 0.25
ROUTER_LANES = 128

V7X_VMEM_LIMIT = 56 * 1024 * 1024

F32 = jnp.float32
BF16 = jnp.bfloat16
I32 = jnp.int32


def _params(*sem):
    return pltpu.CompilerParams(dimension_semantics=sem, vmem_limit_bytes=V7X_VMEM_LIMIT)


def _sigmoid(v):
    return 0.5 * jnp.tanh(0.5 * v) + 0.5


def _split_bf16(v):
    hi = v.astype(BF16)
    return hi, (v - hi.astype(F32)).astype(BF16)


def _dot3(a, b_hi, b_lo):
    a_hi, a_lo = _split_bf16(a)
    return (jnp.dot(a_hi, b_hi, preferred_element_type=F32)
            + jnp.dot(a_lo, b_hi, preferred_element_type=F32)
            + jnp.dot(a_hi, b_lo, preferred_element_type=F32))


def _ln_rows(v, g, b):
    mu = jnp.mean(v, axis=-1, keepdims=True)
    d = v - mu
    var = jnp.mean(d * d, axis=-1, keepdims=True)
    return d * lax.rsqrt(var + LN_EPS) * g + b


ADA_K_TILE = 256


def _ada_kernel(cs_ref, w_ref, b_ref, out_ref):
    k = pl.program_id(0)
    s = cs_ref[...]
    s = s * _sigmoid(s)
    w_hi, w_lo = _split_bf16(w_ref[...])
    part = _dot3(s, w_hi, w_lo)

    @pl.when(k == 0)
    def _():
        out_ref[...] = part + b_ref[...]

    @pl.when(k > 0)
    def _():
        out_ref[...] += part


def _ada(cs, w_ada, b_ada):
    rows, d = cs.shape
    cols = w_ada.shape[1]
    tk = ADA_K_TILE
    return pl.pallas_call(
        _ada_kernel,
        grid=(d // tk,),
        in_specs=[pl.BlockSpec((rows, tk), lambda k: (0, k)),
                  pl.BlockSpec((tk, cols), lambda k: (k, 0)),
                  pl.BlockSpec((1, cols), lambda k: (0, 0))],
        out_specs=pl.BlockSpec((rows, cols), lambda k: (0, 0)),
        out_shape=jax.ShapeDtypeStruct((rows, cols), F32),
        compiler_params=_params("arbitrary"),
        name="ada",
    )(cs, w_ada, b_ada)


def _log_sigmoid(v):
    return jnp.minimum(v, 0.0) - jnp.log(1.0 + jnp.exp(-jnp.abs(v)))


def _ctx_kernel(ctx_ref, sh_ref, sc_ref, wk_ref, wv0_ref, wv1_ref, bk_ref, bv0_ref, bv1_ref, df_ref, db_ref,
                sf_ref, sb_ref):
    lc = ctx_ref.shape[1]
    u = (ctx_ref[0] * (1.0 + sc_ref[0]) + sh_ref[0]).astype(BF16)
    k = (jnp.dot(u, wk_ref[...], preferred_element_type=F32) + bk_ref[...]) * RET_QK_DIM ** -0.5
    v = jnp.concatenate(
        [(jnp.dot(u, w[...], preferred_element_type=F32) + b[...]).astype(BF16)
         for w, b in ((wv0_ref, bv0_ref), (wv1_ref, bv1_ref))], axis=1)
    pos = lax.broadcasted_iota(I32, (lc, 1), 0).astype(F32)
    dn = (((0,), (0,)), ((), ()))
    for h in range(RET_HEADS):
        lgf = _log_sigmoid(df_ref[h])
        lgb = _log_sigmoid(db_ref[h])
        kh = k[:, h * RET_QK_DIM:(h + 1) * RET_QK_DIM]
        vh = v[:, h * RET_V_DIM:(h + 1) * RET_V_DIM]
        kf = (kh * jnp.exp((lc - 1.0 - pos) * lgf)).astype(BF16)
        kb = (kh * jnp.exp(pos * lgb)).astype(BF16)
        sf_ref[0, h] = lax.dot_general(kf, vh, dn, preferred_element_type=F32)
        sb_ref[0, h] = lax.dot_general(kb, vh, dn, preferred_element_type=F32)


def _ctx_states(ctx, mod3, w_in3, b_in3, dec_f, dec_b, ctx_row):
    b_, lc, d = ctx.shape
    assert RET_HEADS * RET_QK_DIM == d and RET_HEADS * RET_V_DIM == 2 * d
    st = jax.ShapeDtypeStruct((b_, RET_HEADS, RET_QK_DIM, RET_V_DIM), F32)
    st_spec = pl.BlockSpec((1, RET_HEADS, RET_QK_DIM, RET_V_DIM), lambda b: (b, 0, 0, 0))
    wblk = lambda blk: pl.BlockSpec((d, d), lambda b: (0, blk))
    bblk = lambda blk: pl.BlockSpec((1, d), lambda b: (0, blk))
    dec_spec = pl.BlockSpec((RET_HEADS, 1, 1), lambda b: (0, 0, 0))
    return pl.pallas_call(
        _ctx_kernel,
        grid=(b_,),
        in_specs=[pl.BlockSpec((1, lc, d), lambda b: (b, 0, 0)),
                  pl.BlockSpec((1, 1, d), lambda b: (ctx_row, 0, 0)),
                  pl.BlockSpec((1, 1, d), lambda b: (ctx_row, 0, 1)),
                  wblk(PB_K), wblk(PB_V), wblk(PB_V + 1),
                  bblk(PB_K), bblk(PB_V), bblk(PB_V + 1),
                  dec_spec, dec_spec],
        out_specs=[st_spec, st_spec],
        out_shape=[st, st],
        compiler_params=_params("arbitrary"),
        name="ctx",
    )(ctx, mod3, mod3, w_in3, w_in3, w_in3, b_in3, b_in3, b_in3, dec_f, dec_b)


PROJ_BLOCKS = 8
PB_Q, PB_K, PB_V, PB_G, PB_GATE = 0, 1, 2, 4, 6
WB_GLU, WB_GATE = 6, 8
PROJ_STEPS = PROJ_BLOCKS // 2 + 1


def _proj_kernel(x_ref, sh_ref, sc_ref, w_ref, b_ref, cos_ref, sin_ref, out_ref, z_ref, u_scr):
    j = pl.program_id(1)
    d = x_ref.shape[1]
    halves = (slice(0, d), slice(d, 2 * d))

    @pl.when(j == 0)
    def _():
        u_scr[...] = (x_ref[...] * (1.0 + sc_ref[0]) + sh_ref[0]).astype(BF16)

    def mm(blk):
        cols = slice(blk * d, (blk + 1) * d)
        return jnp.dot(u_scr[...], w_ref[:, cols], preferred_element_type=F32) + b_ref[:, cols]

    def rope(acc, half):
        cos = cos_ref[...]
        sin = sin_ref[...]
        for h in range(acc.shape[1] // RET_QK_DIM):
            seg = acc[:, h * RET_QK_DIM:(h + 1) * RET_QK_DIM]
            sl = slice(half.start + h * RET_QK_DIM, half.start + (h + 1) * RET_QK_DIM)
            out_ref[:, sl] = (seg * cos + pltpu.roll(seg, RET_QK_DIM // 2, 1) * sin).astype(BF16)

    def step(jj):
        if jj == 0:
            rope(mm(PB_Q), halves[0])
            rope(mm(PB_K) * RET_QK_DIM ** -0.5, halves[1])
        elif jj == 1:
            for hh, half in enumerate(halves):
                out_ref[:, half] = mm(PB_V + hh).astype(BF16)
        elif jj == 2:
            for hh, half in enumerate(halves):
                acc = mm(PB_G + hh)
                out_ref[:, half] = (acc * _sigmoid(acc)).astype(BF16)
        elif jj == 3:
            for hh, half in enumerate(halves):
                out_ref[:, half] = _sigmoid(mm(WB_GATE + hh)).astype(BF16)
        else:
            z_ref[...] = (mm(WB_GLU) * _sigmoid(mm(WB_GLU + 1))).astype(BF16)

    for jj in range(PROJ_STEPS):
        pl.when(j == jj)(functools.partial(step, jj))


def _proj(x2, mod3, w_in3, b_in3, cos, sin, seq, tm):
    n, d = x2.shape
    tiles_per_seq = seq // tm
    last_pair = PROJ_BLOCKS // 2 - 1
    return pl.pallas_call(
        _proj_kernel,
        grid=(n // tm, PROJ_STEPS),
        in_specs=[pl.BlockSpec((tm, d), lambda i, j: (i, 0)),
                  pl.BlockSpec((1, 1, d), lambda i, j: (i // tiles_per_seq, 0, 0)),
                  pl.BlockSpec((1, 1, d), lambda i, j: (i // tiles_per_seq, 0, 1)),
                  pl.BlockSpec(memory_space=pltpu.VMEM),
                  pl.BlockSpec(memory_space=pltpu.VMEM),
                  pl.BlockSpec((tm, RET_QK_DIM), lambda i, j: (i % tiles_per_seq, 0)),
                  pl.BlockSpec((tm, RET_QK_DIM), lambda i, j: (i % tiles_per_seq, 0))],
        out_specs=[pl.BlockSpec((tm, 2 * d), lambda i, j: (i, jnp.minimum(j, last_pair))),
                   pl.BlockSpec((tm, d), lambda i, j: (i, 0))],
        out_shape=[jax.ShapeDtypeStruct((n, PROJ_BLOCKS * d), BF16),
                   jax.ShapeDtypeStruct((n, d), BF16)],
        scratch_shapes=[pltpu.VMEM((tm, d), BF16)],
        compiler_params=_params("arbitrary", "arbitrary"),
        name="proj",
    )(x2, mod3, mod3, w_in3, b_in3, cos, sin)


def _ret_kernel(q_ref, k_ref, v_ref, g_ref, s0f_ref, s0b_ref, df_ref, db_ref, out_ref,
                sf_all, sb_all, p_scr, o_scr):
    c_ = RET_CHUNK
    n_chunks = q_ref.shape[1] // c_
    dn_t = (((0,), (0,)), ((), ()))
    lgf = _log_sigmoid(df_ref[0])
    lgb = _log_sigmoid(db_ref[0])
    ri = lax.broadcasted_iota(I32, (c_, c_), 0)
    ci = lax.broadcasted_iota(I32, (c_, c_), 1)
    diff = (ri - ci).astype(F32)
    mask = jnp.where(diff > 0, jnp.exp(diff * lgf), jnp.where(diff < 0, jnp.exp(-diff * lgb), 2.0))
    pos = lax.broadcasted_iota(I32, (c_, 1), 0).astype(F32)
    qdec_f = jnp.exp((pos + 1.0) * lgf)
    qdec_b = jnp.exp((c_ - pos) * lgb)
    kdec_f = jnp.exp((c_ - 1.0 - pos) * lgf)
    kdec_b = jnp.exp(pos * lgb)
    cdec_f = jnp.exp(c_ * lgf)
    cdec_b = jnp.exp(c_ * lgb)

    def chunk(c):
        return slice(c * c_, (c + 1) * c_)

    for c in range(n_chunks):
        k = k_ref[0, chunk(c), :].astype(F32)
        v = v_ref[0, chunk(c), :]
        sf_all[c] = lax.dot_general((k * kdec_f).astype(BF16), v, dn_t, preferred_element_type=F32)
        sb_all[c] = lax.dot_general((k * kdec_b).astype(BF16), v, dn_t, preferred_element_type=F32)

    sf = s0f_ref[0, 0]
    sb = s0b_ref[0, 0]
    for t in range(n_chunks):
        kv = sf_all[t]
        sf_all[t] = sf
        sf = sf * cdec_f + kv
        c = n_chunks - 1 - t
        kv = sb_all[c]
        sb_all[c] = sb
        sb = sb * cdec_b + kv

    for c in range(n_chunks):
        s = lax.dot_general(q_ref[0, chunk(c), :], k_ref[0, chunk(c), :], (((1,), (1,)), ((), ())),
                            preferred_element_type=F32)
        p_scr[c] = (s * mask).astype(BF16)

    for c in range(n_chunks):
        q = q_ref[0, chunk(c), :]
        o = jnp.dot(p_scr[c], v_ref[0, chunk(c), :], preferred_element_type=F32)
        o = o + qdec_f * jnp.dot(q, sf_all[c].astype(BF16), preferred_element_type=F32)
        o = o + qdec_b * jnp.dot(q, sb_all[c].astype(BF16), preferred_element_type=F32)
        o_scr[chunk(c), :] = o

    for c in range(n_chunks):
        o = o_scr[chunk(c), :]
        mu = jnp.mean(o, axis=-1, keepdims=True)
        d = o - mu
        var = jnp.mean(d * d, axis=-1, keepdims=True)
        on = d * lax.rsqrt(var + LN_EPS)
        out_ref[0, chunk(c), :] = (on * g_ref[0, chunk(c), :].astype(F32)).astype(BF16)


def _retention(proj3, s0f, s0b, dec_f, dec_b, d):
    b_, seq, _ = proj3.shape
    qb, kb = PB_Q * d // RET_QK_DIM, PB_K * d // RET_QK_DIM
    vb, gb = PB_V * d // RET_V_DIM, PB_G * d // RET_V_DIM
    n_chunks = seq // RET_CHUNK
    st_spec = pl.BlockSpec((1, 1, RET_QK_DIM, RET_V_DIM), lambda b, h: (b, h, 0, 0))
    st_all = pltpu.VMEM((n_chunks, RET_QK_DIM, RET_V_DIM), F32)
    return pl.pallas_call(
        _ret_kernel,
        grid=(b_, RET_HEADS),
        in_specs=[pl.BlockSpec((1, seq, RET_QK_DIM), lambda b, h: (b, 0, qb + h)),
                  pl.BlockSpec((1, seq, RET_QK_DIM), lambda b, h: (b, 0, kb + h)),
                  pl.BlockSpec((1, seq, RET_V_DIM), lambda b, h: (b, 0, vb + h)),
                  pl.BlockSpec((1, seq, RET_V_DIM), lambda b, h: (b, 0, gb + h)),
                  st_spec, st_spec,
                  pl.BlockSpec((1, 1, 1), lambda b, h: (h, 0, 0)),
                  pl.BlockSpec((1, 1, 1), lambda b, h: (h, 0, 0))],
        out_specs=pl.BlockSpec((1, seq, RET_V_DIM), lambda b, h: (b, 0, h)),
        out_shape=jax.ShapeDtypeStruct((b_, seq, RET_HEADS * RET_V_DIM), BF16),
        scratch_shapes=[st_all, st_all,
                        pltpu.VMEM((n_chunks, RET_CHUNK, RET_CHUNK), BF16),
                        pltpu.VMEM((seq, RET_V_DIM), F32)],
        compiler_params=_params("arbitrary", "arbitrary"),
        name="ret",
    )(proj3, proj3, proj3, proj3, s0f, s0b, dec_f, dec_b)


VEC_CONV_B, VEC_CLN_G, VEC_CLN_B, VEC_BCONV, VEC_BMIX, VEC_LN1_G, VEC_LN1_B = range(7)
CONV_ROWS = 64
YA_BLOCKS = 4
LANES = 128
RT_E1, RT_E2, RT_RANK1, RT_RANK2 = 0, 1, 2, 3
RT_W1, RT_W2 = 0, 1


def _merge_kernel(ret_ref, z_ref, zp_ref, zn_ref, ga_ref, gb_ref, x_ref, gt_ref, shf_ref, scf_ref,
                  wret_ref, wconv_ref, wmix_ref, dw_ref, vec_ref, wrh_ref, wrl_ref, br_ref,
                  h1_ref, t_ref, ri_ref, rw_ref, cnt_ref, zext, zc_scr, cnt_scr, ya_scr, *, tiles_per_seq):
    tm, d = x_ref.shape
    step = pl.program_id(0)
    ti = step % tiles_per_seq
    vec = vec_ref[...]

    def row(r):
        return vec[r:r + 1, :]


    has_prev = (ti > 0).astype(F32)
    has_next = (ti < tiles_per_seq - 1).astype(F32)
    zext[0:CONV_HALO, :] = zp_ref[...].astype(F32) * has_prev
    zext[CONV_HALO:CONV_HALO + tm, :] = z_ref[...].astype(F32)
    zext[CONV_HALO + tm:, :] = zn_ref[...].astype(F32) * has_next
    base = CONV_HALO - CONV_WIDTH // 2
    win_rows = CONV_ROWS + 2 * CONV_HALO

    n_lt = d // LANES
    kc = ret_ref.shape[1] // n_lt

    def conv_piece(r0, lt):
        ls = slice(lt * LANES, (lt + 1) * LANES)
        win = zext[pl.ds(r0, win_rows), ls]
        acc = jnp.zeros((CONV_ROWS, LANES), F32) + vec_ref[VEC_CONV_B:VEC_CONV_B + 1, ls]
        for s in range(SUBLANES):
            taps = [w for w in range(CONV_WIDTH) if (base + w) % SUBLANES == s]
            if not taps:
                continue
            sh = win if s == 0 else pltpu.roll(win, win_rows - s, 0)
            for w in taps:
                a = (base + w) - s
                acc = acc + sh[a:a + CONV_ROWS, :] * dw_ref[w:w + 1, ls]
        return acc

    def conv_rows(r, carry):
        r0 = pl.multiple_of(r * (2 * CONV_ROWS), 2 * CONV_ROWS)
        ya = None
        accs = []
        for lt in range(n_lt):
            part = jnp.dot(ret_ref[:, lt * kc:(lt + 1) * kc], wret_ref[r, lt * kc:(lt + 1) * kc, :],
                           preferred_element_type=F32)
            ya = part if ya is None else ya + part
            accs.append((conv_piece(r0, lt), conv_piece(r0 + CONV_ROWS, lt)))
        ya_scr[r] = ya
        for lt in range(n_lt):
            ls = slice(lt * LANES, (lt + 1) * LANES)
            zc_scr[pl.ds(r0, CONV_ROWS), ls] = accs[lt][0]
            zc_scr[pl.ds(r0 + CONV_ROWS, CONV_ROWS), ls] = accs[lt][1]
        return carry

    lax.fori_loop(0, YA_BLOCKS, conv_rows, 0)
    y_a = jnp.concatenate([ya_scr[q] for q in range(YA_BLOCKS)], axis=1)
    zc = _ln_rows(zc_scr[...], row(VEC_CLN_G), row(VEC_CLN_B))
    zc = zc * _sigmoid(zc)
    y_b = jnp.dot(zc.astype(BF16), wconv_ref[...], preferred_element_type=F32) + row(VEC_BCONV)

    mixed = ga_ref[...].astype(F32) * y_a + gb_ref[...].astype(F32) * y_b
    mix = jnp.dot(mixed.astype(BF16), wmix_ref[...], preferred_element_type=F32) + row(VEC_BMIX)
    h1 = _ln_rows(ALPHA * x_ref[...] + gt_ref[0] * mix, row(VEC_LN1_G), row(VEC_LN1_B))
    h1_ref[...] = h1
    t = h1 * (1.0 + scf_ref[0]) + shf_ref[0]
    t_ref[...] = t

    logits = _dot3(t, wrh_ref[...], wrl_ref[...]) + br_ref[...]

    lane = lax.broadcasted_iota(I32, logits.shape, 1).astype(F32)
    neg = -jnp.inf
    big = float(ROUTER_LANES)
    is_grp = lane < N_GROUPS
    gl = jnp.where(is_grp, logits, neg)
    gmax = jnp.max(gl, axis=-1, keepdims=True)
    gidx = jnp.min(jnp.where(gl == gmax, lane, big), axis=-1, keepdims=True)
    gsum = jnp.sum(jnp.where(is_grp, jnp.exp(gl - gmax), 0.0), axis=-1, keepdims=True)
    grp_w = 1.0 / gsum
    lo = N_GROUPS + gidx * EXPERTS_PER_GROUP
    el = jnp.where(lane >= lo, jnp.where(lane < lo + EXPERTS_PER_GROUP, logits, neg), neg)
    m1 = jnp.max(el, axis=-1, keepdims=True)
    i1 = jnp.min(jnp.where(el == m1, lane, big), axis=-1, keepdims=True)
    el2 = jnp.where(lane == i1, neg, el)
    m2 = jnp.max(el2, axis=-1, keepdims=True)
    i2 = jnp.min(jnp.where(el2 == m2, lane, big), axis=-1, keepdims=True)
    r = jnp.exp(m2 - m1)
    w1 = grp_w / (1.0 + r)
    w2 = grp_w * r / (1.0 + r)

    @pl.when(step == 0)
    def _():
        cnt_scr[...] = jnp.zeros_like(cnt_scr)

    oh1 = lane == i1
    oh2 = lane == i2
    oh = jnp.where(oh1, 1.0, jnp.where(oh2, 1.0, 0.0))
    tri = jnp.where(lax.broadcasted_iota(I32, (tm, tm), 0) > lax.broadcasted_iota(I32, (tm, tm), 1), 1.0, 0.0)
    before = jnp.dot(tri.astype(BF16), oh.astype(BF16), preferred_element_type=F32) + cnt_scr[0:1, :]
    rank1 = jnp.sum(jnp.where(oh1, before, 0.0), axis=-1, keepdims=True)
    rank2 = jnp.sum(jnp.where(oh2, before, 0.0), axis=-1, keepdims=True)
    cnt = cnt_scr[0:1, :] + jnp.sum(oh, axis=0, keepdims=True)
    cnt_scr[...] = jnp.broadcast_to(cnt, cnt_scr.shape)
    cnt_ref[...] = jnp.broadcast_to(cnt, cnt_ref.shape)

    e1 = i1 - N_GROUPS
    e2 = i2 - N_GROUPS
    ri = jnp.where(lane == RT_E1, e1, jnp.where(lane == RT_E2, e2,
                   jnp.where(lane == RT_RANK1, rank1, jnp.where(lane == RT_RANK2, rank2, 0.0))))
    ri_ref[...] = ri.astype(I32)
    rw_ref[...] = jnp.where(lane == RT_W1, w1, jnp.where(lane == RT_W2, w2, 0.0))


def _merge(ret2, proj2, z2, x2, mod3, wret, wconv, wmix, dw, vec, wrh, wrl, br, seq, tm):
    n, d = x2.shape
    assert tm == YA_BLOCKS * 2 * CONV_ROWS
    tiles_per_seq = seq // tm
    hb = tm // CONV_HALO
    last_hb = n // CONV_HALO - 1
    const = lambda i: (0, 0)
    bat = lambda k: (lambda i: (i // tiles_per_seq, 0, k))
    return pl.pallas_call(
        functools.partial(_merge_kernel, tiles_per_seq=tiles_per_seq),
        grid=(n // tm,),
        in_specs=[pl.BlockSpec((tm, ret2.shape[1]), lambda i: (i, 0)),
                  pl.BlockSpec((tm, d), lambda i: (i, 0)),
                  pl.BlockSpec((CONV_HALO, d), lambda i: (jnp.maximum(i * hb - 1, 0), 0)),
                  pl.BlockSpec((CONV_HALO, d), lambda i: (jnp.minimum((i + 1) * hb, last_hb), 0)),
                  pl.BlockSpec((tm, d), lambda i: (i, PB_GATE)),
                  pl.BlockSpec((tm, d), lambda i: (i, PB_GATE + 1)),
                  pl.BlockSpec((tm, d), lambda i: (i, 0)),
                  pl.BlockSpec((1, 1, d), bat(2)),
                  pl.BlockSpec((1, 1, d), bat(3)),
                  pl.BlockSpec((1, 1, d), bat(4)),
                  pl.BlockSpec(wret.shape, lambda i: (0, 0, 0)),
                  pl.BlockSpec(wconv.shape, const),
                  pl.BlockSpec(wmix.shape, const),
                  pl.BlockSpec(dw.shape, const),
                  pl.BlockSpec(vec.shape, const),
                  pl.BlockSpec(wrh.shape, const),
                  pl.BlockSpec(wrl.shape, const),
                  pl.BlockSpec(br.shape, const)],
        out_specs=[pl.BlockSpec((tm, d), lambda i: (i, 0)),
                   pl.BlockSpec((tm, d), lambda i: (i, 0)),
                   pl.BlockSpec((tm, ROUTER_LANES), lambda i: (i, 0)),
                   pl.BlockSpec((tm, ROUTER_LANES), lambda i: (i, 0)),
                   pl.BlockSpec((SUBLANES, ROUTER_LANES), const)],
        out_shape=[jax.ShapeDtypeStruct((n, d), F32),
                   jax.ShapeDtypeStruct((n, d), F32),
                   jax.ShapeDtypeStruct((n, ROUTER_LANES), I32),
                   jax.ShapeDtypeStruct((n, ROUTER_LANES), F32),
                   jax.ShapeDtypeStruct((SUBLANES, ROUTER_LANES), F32)],
        scratch_shapes=[pltpu.VMEM((tm + 2 * CONV_HALO, d), F32),
                        pltpu.VMEM((tm, d), F32),
                        pltpu.VMEM((SUBLANES, ROUTER_LANES), F32),
                        pltpu.VMEM((YA_BLOCKS, tm, d // YA_BLOCKS), F32)],
        compiler_params=_params("arbitrary"),
        name="merge",
    )(ret2, z2, z2, z2, proj2, proj2, x2, mod3, mod3, mod3,
      wret, wconv, wmix, dw, vec, wrh, wrl, br)


DISPATCH_TILE = 1024


def _dispatch_kernel(pos_ref, t_ref, xs_hbm, sem):
    tm = t_ref.shape[0]
    base = pl.program_id(0) * tm

    for r in range(tm):
        for k in range(EXPERT_TOP_K):
            dst = pos_ref[(base + r) * EXPERT_TOP_K + k]
            pltpu.make_async_copy(t_ref.at[pl.ds(r, 1)], xs_hbm.at[pl.ds(dst, 1)], sem).start(priority=k)
    for k in range(EXPERT_TOP_K):
        pltpu.make_async_copy(t_ref, xs_hbm.at[pl.ds(0, tm)], sem).wait()


def _dispatch(pos_flat, t2):
    n, d = t2.shape
    tm = min(DISPATCH_TILE, n)
    return pl.pallas_call(
        _dispatch_kernel,
        grid=(n // tm,),
        in_specs=[pl.BlockSpec(memory_space=pltpu.SMEM),
                  pl.BlockSpec((tm, d), lambda i: (i, 0))],
        out_specs=pl.BlockSpec(memory_space=pl.ANY),
        out_shape=jax.ShapeDtypeStruct((n * EXPERT_TOP_K, d), F32),
        scratch_shapes=[pltpu.SemaphoreType.DMA],
        compiler_params=_params("arbitrary"),
        name="dispatch",
    )(pos_flat, t2)


EXPERT_TILE = 512


def _experts_kernel(ie_ref, it_ref, lo_ref, hi_ref, nv_ref, xs_ref, wg_hbm, wu_hbm, wd_hbm, ys_ref,
                    wg_bf, wu_bf, wd_bf, wg_st, wu_st, wd_st, sem):
    k = pl.program_id(0)
    n_items = ie_ref.shape[0]
    prev = jnp.maximum(k - 1, 0)
    new_expert = (k == 0) | (ie_ref[k] != ie_ref[prev])
    new_tile = (k == 0) | (it_ref[k] != it_ref[prev])
    slot = nv_ref[1 + k]
    nxt = nv_ref[1 + n_items + k]

    def copies(e, s):
        return (pltpu.make_async_copy(wg_hbm.at[0, e], wg_st.at[s], sem.at[s]),
                pltpu.make_async_copy(wu_hbm.at[0, e], wu_st.at[s], sem.at[s]),
                pltpu.make_async_copy(wd_hbm.at[0, e], wd_st.at[s], sem.at[s]))

    @pl.when(k == 0)
    def _():
        for cp in copies(ie_ref[0], 0):
            cp.start()

    @pl.when(new_expert)
    def _():
        for cp in copies(ie_ref[k], slot):
            cp.wait()

        @pl.when(nxt >= 0)
        def _():
            for cp in copies(nxt, 1 - slot):
                cp.start(priority=1)

        wg_bf[...] = wg_st[slot].astype(BF16)
        wu_bf[...] = wu_st[slot].astype(BF16)
        wd_bf[...] = wd_st[slot].astype(BF16)

    @pl.when(k < nv_ref[0])
    def _():
        x = xs_ref[...].astype(BF16)
        g = jnp.dot(x, wg_bf[...], preferred_element_type=F32)
        u = jnp.dot(x, wu_bf[...], preferred_element_type=F32)
        hid = (g * _sigmoid(g) * u).astype(BF16)
        y = jnp.dot(hid, wd_bf[...], preferred_element_type=F32)
        rows = lax.broadcasted_iota(I32, (xs_ref.shape[0], 1), 0)
        mine = (rows >= lo_ref[k]) & (rows < hi_ref[k])

        @pl.when(new_tile)
        def _():
            ys_ref[...] = jnp.where(mine, y, 0.0)

        @pl.when(jnp.logical_not(new_tile))
        def _():
            ys_ref[...] = jnp.where(mine, y, ys_ref[...])


def _experts(meta, xs, w_gate, w_up, w_down):
    p, d = xs.shape
    ff = w_gate.shape[-1]
    n_items = meta[0].shape[0]
    grid_spec = pltpu.PrefetchScalarGridSpec(
        num_scalar_prefetch=5,
        grid=(n_items,),
        in_specs=[pl.BlockSpec((EXPERT_TILE, d), lambda k, ie, it, lo, hi, nv: (it[k], 0)),
                  pl.BlockSpec(memory_space=pl.ANY),
                  pl.BlockSpec(memory_space=pl.ANY),
                  pl.BlockSpec(memory_space=pl.ANY)],
        out_specs=pl.BlockSpec((EXPERT_TILE, d), lambda k, ie, it, lo, hi, nv: (it[k], 0)),
        scratch_shapes=[pltpu.VMEM((d, ff), BF16), pltpu.VMEM((d, ff), BF16), pltpu.VMEM((ff, d), BF16),
                        pltpu.VMEM((2, d, ff), F32), pltpu.VMEM((2, d, ff), F32), pltpu.VMEM((2, ff, d), F32),
                        pltpu.SemaphoreType.DMA((2,))])
    return pl.pallas_call(
        _experts_kernel,
        grid_spec=grid_spec,
        out_shape=jax.ShapeDtypeStruct((p, d), F32),
        compiler_params=_params("arbitrary"),
        name="experts",
    )(*meta, xs, w_gate, w_up, w_down)


def _expert_work_items(counts, p):
    n_tiles = p // EXPERT_TILE
    n_items = n_tiles + N_EXPERTS - 1
    offs = jnp.concatenate([jnp.zeros((1,), I32), jnp.cumsum(counts)])
    first = offs[:-1] // EXPERT_TILE
    last = (offs[1:] - 1) // EXPERT_TILE
    per = jnp.where(counts > 0, last - first + 1, 0)
    ends = jnp.cumsum(per)
    starts = ends - per
    n_valid = ends[-1]
    k = jnp.arange(n_items, dtype=I32)
    kk = jnp.minimum(k, n_valid - 1)
    ie = jnp.sum((ends[None, :] <= kk[:, None]).astype(I32), axis=1)
    it = first[ie] + (kk - starts[ie])
    lo = jnp.maximum(offs[ie], it * EXPERT_TILE) - it * EXPERT_TILE
    hi = jnp.minimum(offs[ie + 1], (it + 1) * EXPERT_TILE) - it * EXPERT_TILE
    valid = k < n_valid
    lo = jnp.where(valid, lo, 0)
    hi = jnp.where(valid, hi, 0)
    change = jnp.concatenate([jnp.ones((1,), bool), ie[1:] != ie[:-1]])
    seg = jnp.cumsum(change.astype(I32)) - 1
    first_idx = jnp.argmax(seg[None, :] == k[:, None], axis=1)
    nxt_seg = seg + 1
    nxt_e = jnp.where(nxt_seg <= seg[-1], ie[first_idx[jnp.minimum(nxt_seg, n_items - 1)]], -1)
    nv = jnp.concatenate([n_valid.reshape(1), seg % 2, nxt_e]).astype(I32)
    return offs, (ie.astype(I32), it.astype(I32), lo.astype(I32), hi.astype(I32), nv)


COMBINE_HALF = 256


def _combine_kernel(pos_ref, ys_hbm, rw_ref, h1_ref, gt_ref, ln_ref, out_ref, buf, sem):
    i = pl.program_id(0)
    n_steps = pl.num_programs(0)
    hm = COMBINE_HALF
    base = i * (2 * hm)

    def slot_copy(s, k):
        return pltpu.make_async_copy(ys_hbm.at[pl.ds(0, hm)], buf.at[s, k], sem.at[s])

    def gather(first_tok, s):
        for r in range(hm):
            for k in range(EXPERT_TOP_K):
                src = pos_ref[(first_tok + r) * EXPERT_TOP_K + k]
                pltpu.make_async_copy(ys_hbm.at[pl.ds(src, 1)], buf.at[s, k, pl.ds(r, 1)],
                                      sem.at[s]).start(priority=k)

    def finish(s):
        for k in range(EXPERT_TOP_K):
            slot_copy(s, k).wait()
        rows = slice(s * hm, (s + 1) * hm)
        rw = rw_ref[rows, :]
        y = rw[:, RT_W1:RT_W1 + 1] * buf[s, 0] + rw[:, RT_W2:RT_W2 + 1] * buf[s, 1]
        out_ref[rows, :] = _ln_rows(ALPHA * h1_ref[rows, :] + gt_ref[0] * y, ln_ref[0:1, :], ln_ref[1:2, :])

    @pl.when(i == 0)
    def _():
        gather(0, 0)

    gather(base + hm, 1)
    finish(0)

    @pl.when(i + 1 < n_steps)
    def _():
        gather(base + 2 * hm, 0)

    finish(1)


def _combine(pos_flat, ys, rw, h1, mod3, ln2, seq):
    n, d = h1.shape
    tm = 2 * COMBINE_HALF
    tiles_per_seq = seq // tm
    return pl.pallas_call(
        _combine_kernel,
        grid=(n // tm,),
        in_specs=[pl.BlockSpec(memory_space=pltpu.SMEM),
                  pl.BlockSpec(memory_space=pl.ANY),
                  pl.BlockSpec((tm, ROUTER_LANES), lambda i: (i, 0)),
                  pl.BlockSpec((tm, d), lambda i: (i, 0)),
                  pl.BlockSpec((1, 1, d), lambda i: (i // tiles_per_seq, 0, 5)),
                  pl.BlockSpec(ln2.shape, lambda i: (0, 0))],
        out_specs=pl.BlockSpec((tm, d), lambda i: (i, 0)),
        out_shape=jax.ShapeDtypeStruct((n, d), F32),
        scratch_shapes=[pltpu.VMEM((2, EXPERT_TOP_K, COMBINE_HALF, d), F32),
                        pltpu.SemaphoreType.DMA((2,))],
        compiler_params=_params("arbitrary"),
        name="combine",
    )(pos_flat, ys, rw, h1, mod3, ln2)


def _rope_tables(seq):
    pos = np.arange(seq)
    r = (pos // GRID_W).astype(np.float32)
    col = (pos % GRID_W).astype(np.float32)
    n_freq = RET_QK_DIM // 4
    inv = (np.float32(ROPE_BASE) ** (-np.arange(n_freq, dtype=np.float32) / np.float32(n_freq))).astype(np.float32)
    ang = np.concatenate([r[:, None] * inv, col[:, None] * inv], axis=-1)
    ang = np.concatenate([ang, ang], axis=-1).astype(np.float32)
    sign = np.concatenate([-np.ones((RET_QK_DIM // 2,), np.float32), np.ones((RET_QK_DIM // 2,), np.float32)])
    return jnp.asarray(np.cos(ang), F32), jnp.asarray(np.sin(ang) * sign, F32)


def kernel(x, c, ctx, c_ctx, w_ada, b_ada, w_in, b_in, ret_decay_fwd, ret_decay_bwd, w_ret_out, conv_dw, conv_dw_b, conv_ln_g, conv_ln_b, w_conv_out, b_conv_out, w_mix_out, b_mix_out, ln1_g, ln1_b, w_router_grp, b_router_grp, w_router_exp, b_router_exp, w_exp_gate, w_exp_up, w_exp_down, ln2_g, ln2_b):
    b_, seq, d = x.shape
    n = b_ * seq
    assert w_ada.shape[0] == DEPTH
    mod_rows = SUBLANES
    assert b_ + 1 <= mod_rows

    cs = jnp.concatenate([c, c_ctx[None, :], jnp.zeros((mod_rows - b_ - 1, d), F32)], axis=0)
    mod = _ada(cs, w_ada[0], b_ada[0][None, :])
    mod3 = mod.reshape(mod_rows, 1, 6 * d)

    w_in3 = w_in[0].astype(BF16)
    b_in3 = b_in[0][None, :]
    dec_f = ret_decay_fwd[0].reshape(RET_HEADS, 1, 1)
    dec_b = ret_decay_bwd[0].reshape(RET_HEADS, 1, 1)

    s0f, s0b = _ctx_states(ctx, mod3, w_in3, b_in3, dec_f, dec_b, b_)

    cos, sin = _rope_tables(seq)
    x2 = x.reshape(n, d)
    proj, z = _proj(x2, mod3, w_in3, b_in3, cos, sin, seq, tm=min(1024, seq))

    ret = _retention(proj.reshape(b_, seq, PROJ_BLOCKS * d), s0f, s0b, dec_f, dec_b, d)

    vec = jnp.concatenate([conv_dw_b, conv_ln_g, conv_ln_b, b_conv_out, b_mix_out, ln1_g, ln1_b,
                           jnp.zeros((1, d), F32)], axis=0)
    tap_rows = -(-CONV_WIDTH // SUBLANES) * SUBLANES
    dw = jnp.concatenate([conv_dw[0], jnp.zeros((tap_rows - CONV_WIDTH, d), F32)], axis=0)
    pad = ROUTER_LANES - N_GROUPS - N_EXPERTS
    wr = jnp.concatenate([w_router_grp[0], w_router_exp[0], jnp.zeros((d, pad), F32)], axis=1)
    wr_hi = wr.astype(BF16)
    wr_lo = (wr - wr_hi.astype(F32)).astype(BF16)
    br = jnp.concatenate([b_router_grp[0], b_router_exp[0], jnp.zeros((pad,), F32)])[None, :]
    h1, t, route_i, route_w, cnt = _merge(
        ret.reshape(n, RET_HEADS * RET_V_DIM), proj, z, x2, mod3,
        w_ret_out[0].astype(BF16).reshape(-1, YA_BLOCKS, d // YA_BLOCKS).transpose(1, 0, 2),
        w_conv_out[0].astype(BF16), w_mix_out[0].astype(BF16),
        dw, vec, wr_hi, wr_lo, br, seq, tm=min(512, seq))

    counts = cnt[0, N_GROUPS:N_GROUPS + N_EXPERTS].astype(I32)
    offs, meta = _expert_work_items(counts, n * EXPERT_TOP_K)
    eid = route_i[:, RT_E1:RT_E2 + 1]
    seg_start = jnp.sum(jnp.where(eid[:, :, None] == jnp.arange(N_EXPERTS, dtype=I32), offs[:N_EXPERTS], 0), axis=-1)
    pos = seg_start + route_i[:, RT_RANK1:RT_RANK2 + 1]
    pos_flat = pos.reshape(n * EXPERT_TOP_K)

    xs = _dispatch(pos_flat, t)
    ys = _experts(meta, xs, w_exp_gate, w_exp_up, w_exp_down)
    ln2 = jnp.concatenate([ln2_g, ln2_b, jnp.zeros((SUBLANES - 2, d), F32)], axis=0)
    out = _combine(pos_flat, ys, route_w, h1, mod3, ln2, seq)
    return out.reshape(b_, seq, d)
```
